```python
import jax, jax.numpy as jnp
from jax import lax
import numpy as np

D_MODEL = 2048
BATCH = 4
SEQ = 4096
DEPTH = 4

N_EVEN = (DEPTH + 1) // 2
N_ODD = DEPTH // 2
NORM_EPS = 1e-6

CONV_DIM = D_MODEL // 2
CONV_WIDTH = 3
RWKV_DIM = D_MODEL // 2
RWKV_HEAD = 64
RWKV_HEADS = RWKV_DIM // RWKV_HEAD
W_LORA = 64
A_LORA = 64
G_LORA = 160
GN_EPS = 64e-5
RWKV_IN = 3 * RWKV_DIM + W_LORA + A_LORA + G_LORA
EVEN_IN = 3 * CONV_DIM + RWKV_IN
EVEN_OUT = CONV_DIM + RWKV_DIM
POOL_WINDOWS = (2, 4, 8, 16)
POOL_GROUP = D_MODEL // 16
POOL_DIM = len(POOL_WINDOWS) * POOL_GROUP
MLA_HEADS = 12
Q_LORA = 512
KV_LORA = 512
QK_NOPE = 128
QK_ROPE = 64
V_HEAD = 128
ROPE_THETA = 10000.0
ATTN_BLOCK = 128
ODD_IN = POOL_DIM + Q_LORA + KV_LORA + QK_ROPE
ODD_OUT = POOL_DIM + MLA_HEADS * V_HEAD
D_FF = ((8 * D_MODEL + 3 * 256 - 1) // (3 * 256)) * 256

kernel_name = 'hybrid_conv_rwkv7_pool_mla_trunk'


def rms_norm(x, g):
    xf = x.astype(jnp.float32)
    y = xf * lax.rsqrt(jnp.mean(xf * xf, axis=-1, keepdims=True) + NORM_EPS)
    return (y * g.astype(jnp.float32)).astype(x.dtype)


def shift_seq(u, n):
    return jnp.pad(u, ((0, 0), (n, 0), (0, 0)))[:, :u.shape[1]]


def short_conv_mixer(p, conv_w):
    b_gate, c_gate, h = jnp.split(p, 3, axis=-1)
    u = c_gate * h
    y = conv_w[:, 2] * u + conv_w[:, 1] * shift_seq(u, 1) + conv_w[:, 0] * shift_seq(u, 2)
    return b_gate * y


def wkv7_scan(r, decay, k, v, kk, a):
    bsz, _, nh, n = r.shape

    def step(state, inp):
        r_t, w_t, k_t, v_t, kk_t, a_t = inp
        sa = jnp.einsum('bhvk,bhk->bhv', state, -kk_t)
        state = (state * w_t[:, :, None, :]
                 + sa[..., None] * (kk_t * a_t)[:, :, None, :]
                 + v_t[..., None] * k_t[:, :, None, :])
        y_t = jnp.einsum('bhvk,bhk->bhv', state, r_t)
        return state, y_t

    xs = tuple(jnp.moveaxis(t, 1, 0) for t in (r, decay, k, v, kk, a))
    init = jnp.zeros((bsz, nh, n, n), jnp.float32)
    _, ys = lax.scan(step, init, xs)
    return jnp.moveaxis(ys, 0, 1)


def rwkv7_mixer(p, mu, w0, w2, a0, a2, g2, k_k, k_a, r_k, ln_w, ln_b):
    bsz, s = p.shape[:2]
    p = p.astype(jnp.float32)
    p = p + (shift_seq(p, 1) - p) * mu
    r, k, v, xw, xa, xg = jnp.split(
        p, [RWKV_DIM, 2 * RWKV_DIM, 3 * RWKV_DIM, 3 * RWKV_DIM + W_LORA,
            3 * RWKV_DIM + W_LORA + A_LORA], axis=-1)
    w = -jax.nn.softplus(-(w0 + jnp.tanh(xw) @ w2)) - 0.5
    a = jax.nn.sigmoid(a0 + xa @ a2)
    g = jax.nn.sigmoid(xg) @ g2
    hs = lambda t: t.reshape(bsz, s, RWKV_HEADS, RWKV_HEAD)
    kk = hs(k * k_k)
    kk = kk / jnp.maximum(jnp.linalg.norm(kk, axis=-1, keepdims=True), 1e-12)
    k = k * (1.0 + (a - 1.0) * k_a)
    r, k, v, a, w = hs(r), hs(k), hs(v), hs(a), hs(w)
    decay = jnp.exp(-jnp.exp(w))
    y = wkv7_scan(r, decay, k, v, kk, a)
    mean = jnp.mean(y, axis=-1, keepdims=True)
    var = jnp.mean(jnp.square(y - mean), axis=-1, keepdims=True)
    y = ((y - mean) * lax.rsqrt(var + GN_EPS)).reshape(bsz, s, RWKV_DIM) * ln_w + ln_b
    bonus = jnp.sum(r * k * r_k, axis=-1, keepdims=True) * v
    y = y + bonus.reshape(bsz, s, RWKV_DIM)
    return y * g


def pool_mixer(u, pool_w, pool_scale):
    bsz, s = u.shape[:2]
    uf = u.astype(jnp.float32)
    cs = jnp.cumsum(uf, axis=1)
    pos = jnp.arange(s)
    diffs = []
    for gi, win in enumerate(POOL_WINDOWS):
        sl = slice(gi * POOL_GROUP, (gi + 1) * POOL_GROUP)
        c = cs[..., sl]
        count = jnp.minimum(pos + 1, win).astype(jnp.float32)[None, :, None]
        diffs.append((c - shift_seq(c, win)) / count - uf[..., sl])
    d = jnp.stack(diffs, axis=2)
    y = jnp.einsum('bsgi,gio->bsgo', d, pool_w).reshape(bsz, s, POOL_DIM)
    return y * pool_scale


def rope_tables(positions):
    inv = 1.0 / (ROPE_THETA ** (jnp.arange(0, QK_ROPE, 2, dtype=jnp.float32) / QK_ROPE))
    ang = positions.astype(jnp.float32)[..., None] * inv
    return jnp.cos(ang), jnp.sin(ang)


def apply_rope(t, cos, sin):
    t = t.astype(jnp.float32)
    t1, t2 = jnp.split(t, 2, axis=-1)
    c, s = cos[:, :, None, :], sin[:, :, None, :]
    return jnp.concatenate([t1 * c - t2 * s, t1 * s + t2 * c], axis=-1)


def mla_mixer(q_lat, kv_lat, k_pe, q_norm, w_uq, kv_norm, w_ukv, cos, sin):
    bsz, s = q_lat.shape[:2]
    q = (rms_norm(q_lat, q_norm) @ w_uq).reshape(bsz, s, MLA_HEADS, QK_NOPE + QK_ROPE)
    kv = (rms_norm(kv_lat, kv_norm) @ w_ukv).reshape(bsz, s, MLA_HEADS, QK_NOPE + V_HEAD)
    q_nope = q[..., :QK_NOPE].astype(jnp.float32)
    q_pe = apply_rope(q[..., QK_NOPE:], cos, sin)
    k_nope = kv[..., :QK_NOPE].astype(jnp.float32)
    v = kv[..., QK_NOPE:].astype(jnp.float32)
    k_pe = apply_rope(k_pe[:, :, None, :], cos, sin)[:, :, 0]
    scale = (QK_NOPE + QK_ROPE) ** -0.5
    outs = []
    for i in range(s // ATTN_BLOCK):
        s0, e = i * ATTN_BLOCK, (i + 1) * ATTN_BLOCK
        sc = (jnp.einsum('bqhd,bkhd->bhqk', q_nope[:, s0:e], k_nope[:, :e])
              + jnp.einsum('bqhr,bkr->bhqk', q_pe[:, s0:e], k_pe[:, :e])) * scale
        mask = (s0 + jnp.arange(ATTN_BLOCK))[:, None] >= jnp.arange(e)[None, :]
        pr = jax.nn.softmax(jnp.where(mask, sc, -1e30), axis=-1)
        outs.append(jnp.einsum('bhqk,bkhd->bqhd', pr, v[:, :e]))
    o = jnp.concatenate(outs, axis=1)
    return o.reshape(bsz, s, MLA_HEADS * V_HEAD)


def swiglu(x, wg, wu, wd):
    return (jax.nn.silu(x @ wg) * (x @ wu)) @ wd


def setup_inputs(seed: int = 0) -> dict:
    key = jax.random.key(seed)
    ks = iter(jax.random.split(key, 40))
    nrm = lambda shape, fan_in: jax.random.normal(next(ks), shape, jnp.float32) * (fan_in ** -0.5)
    gain = lambda shape: 1.0 + 0.02 * jax.random.normal(next(ks), shape, jnp.float32)
    x = jax.random.normal(next(ks), (BATCH, SEQ, D_MODEL), jnp.float32)
    offs = jax.random.randint(next(ks), (BATCH, 1), 0, 1024, dtype=jnp.int32)
    positions = (offs + jnp.arange(SEQ, dtype=jnp.int32)[None, :]).astype(jnp.int32)
    return {
        'x': x,
        'positions': positions,
        'ev_norm': gain((N_EVEN, D_MODEL)),
        'ev_w_in': nrm((N_EVEN, D_MODEL, EVEN_IN), D_MODEL),
        'ev_conv_w': nrm((N_EVEN, CONV_DIM, CONV_WIDTH), CONV_WIDTH),
        'ev_mu': jax.random.uniform(next(ks), (N_EVEN, RWKV_IN), jnp.float32),
        'ev_w0': jax.random.uniform(next(ks), (N_EVEN, RWKV_DIM), jnp.float32, -6.5, -1.5),
        'ev_w2': nrm((N_EVEN, W_LORA, RWKV_DIM), W_LORA),
        'ev_a0': 0.1 * jax.random.normal(next(ks), (N_EVEN, RWKV_DIM), jnp.float32),
        'ev_a2': nrm((N_EVEN, A_LORA, RWKV_DIM), A_LORA),
        'ev_g2': nrm((N_EVEN, G_LORA, RWKV_DIM), G_LORA),
        'ev_k_k': 0.85 + 0.05 * jax.random.normal(next(ks), (N_EVEN, RWKV_DIM), jnp.float32),
        'ev_k_a': 1.0 + 0.05 * jax.random.normal(next(ks), (N_EVEN, RWKV_DIM), jnp.float32),
        'ev_r_k': 0.1 * jax.random.normal(next(ks), (N_EVEN, RWKV_HEADS, RWKV_HEAD), jnp.float32),
        'ev_ln_w': gain((N_EVEN, RWKV_DIM)),
        'ev_ln_b': 0.02 * jax.random.normal(next(ks), (N_EVEN, RWKV_DIM), jnp.float32),
        'ev_w_out': nrm((N_EVEN, EVEN_OUT, D_MODEL), EVEN_OUT),
        'od_norm': gain((N_ODD, D_MODEL)),
        'od_w_in': nrm((N_ODD, D_MODEL, ODD_IN), D_MODEL),
        'od_pool_w': nrm((N_ODD, len(POOL_WINDOWS), POOL_GROUP, POOL_GROUP), POOL_GROUP),
        'od_pool_scale': 0.5 + 0.1 * jax.random.normal(next(ks), (N_ODD, POOL_DIM), jnp.float32),
        'od_q_norm': gain((N_ODD, Q_LORA)),
        'od_w_uq': nrm((N_ODD, Q_LORA, MLA_HEADS * (QK_NOPE + QK_ROPE)), Q_LORA),
        'od_kv_norm': gain((N_ODD, KV_LORA)),
        'od_w_ukv': nrm((N_ODD, KV_LORA, MLA_HEADS * (QK_NOPE + V_HEAD)), KV_LORA),
        'od_w_out': nrm((N_ODD, ODD_OUT, D_MODEL), ODD_OUT),
        'ffn_norm': gain((DEPTH, D_MODEL)),
        'ffn_w_gate': nrm((DEPTH, D_MODEL, D_FF), D_MODEL),
        'ffn_w_up': nrm((DEPTH, D_MODEL, D_FF), D_MODEL),
        'ffn_w_down': nrm((DEPTH, D_FF, D_MODEL), D_FF),
        'final_norm': gain((D_MODEL,)),
    }


def reference(x, positions, ev_norm, ev_w_in, ev_conv_w, ev_mu, ev_w0, ev_w2, ev_a0, ev_a2,
              ev_g2, ev_k_k, ev_k_a, ev_r_k, ev_ln_w, ev_ln_b, ev_w_out, od_norm, od_w_in,
              od_pool_w, od_pool_scale, od_q_norm, od_w_uq, od_kv_norm, od_w_ukv, od_w_out,
              ffn_norm, ffn_w_gate, ffn_w_up, ffn_w_down, final_norm):
    dt = x.dtype
    cos, sin = rope_tables(positions)
    h = x
    for layer in range(DEPTH):
        j = layer // 2
        if layer % 2 == 0:
            p = rms_norm(h, ev_norm[j]) @ ev_w_in[j]
            ya = short_conv_mixer(p[..., :3 * CONV_DIM], ev_conv_w[j])
            yb = rwkv7_mixer(p[..., 3 * CONV_DIM:], ev_mu[j], ev_w0[j], ev_w2[j], ev_a0[j],
                             ev_a2[j], ev_g2[j], ev_k_k[j], ev_k_a[j], ev_r_k[j],
                             ev_ln_w[j], ev_ln_b[j])
            mix = jnp.concatenate([ya.astype(dt), yb.astype(dt)], axis=-1)
            h = h + mix @ ev_w_out[j]
        else:
            p = rms_norm(h, od_norm[j]) @ od_w_in[j]
            o1 = POOL_DIM
            o2 = o1 + Q_LORA
            o3 = o2 + KV_LORA
            yc = pool_mixer(p[..., :o1], od_pool_w[j], od_pool_scale[j])
            yd = mla_mixer(p[..., o1:o2], p[..., o2:o3], p[..., o3:], od_q_norm[j],
                           od_w_uq[j], od_kv_norm[j], od_w_ukv[j], cos, sin)
            mix = jnp.concatenate([yc.astype(dt), yd.astype(dt)], axis=-1)
            h = h + mix @ od_w_out[j]
        h = h + swiglu(rms_norm(h, ffn_norm[layer]), ffn_w_gate[layer], ffn_w_up[layer],
                       ffn_w_down[layer])
    return rms_norm(h, final_norm)
```

```python
import functools

import jax
import jax.numpy as jnp
from jax import lax
from jax.experimental import pallas as pl
from jax.experimental.pallas import tpu as pltpu

F32 = jnp.float32
BF16 = jnp.bfloat16

D_MODEL = 2048
NORM_EPS = 1e-6
CONV_DIM = 1024
RWKV_DIM = 1024
RWKV_HEAD = 64
W_LORA = 64
A_LORA = 64
G_LORA = 160
GN_EPS = 64e-5
POOL_WINDOWS = (2, 4, 8, 16)
POOL_GROUP = 128
POOL_DIM = 512
MLA_HEADS = 12
Q_LORA = 512
KV_LORA = 512
QK_NOPE = 128
QK_ROPE = 64
V_HEAD = 128
ROPE_THETA = 10000.0
D_FF = 5632

LANES = 128
SUBLANES = 8
VMEM_LIMIT = 56 * 1024 * 1024
QK_PAD = 256
WKV_CHUNK = 64
EV_COLS = 6656
EV_LORA = 512
OD_COLS = 1792


def _cparams(sem):
    return pltpu.CompilerParams(dimension_semantics=sem, vmem_limit_bytes=VMEM_LIMIT)


def _rms(x, g):
    ms = jnp.mean(x * x, axis=-1, keepdims=True)
    return x * lax.rsqrt(ms + NORM_EPS) * g


def _sigmoid(x):
    return 1.0 / (1.0 + jnp.exp(-x))


def _dot(a, b):
    return jnp.dot(a, b, preferred_element_type=F32)


def _dot_nt(a, b):
    return lax.dot_general(a, b, (((1,), (1,)), ((), ())), preferred_element_type=F32)


def _dot_tn(a, b):
    return lax.dot_general(a, b, (((0,), (0,)), ((), ())), preferred_element_type=F32)


def _norm_mm_kernel(x_ref, g_ref, w_ref, o_ref, xn_ref):
    @pl.when(pl.program_id(1) == 0)
    def _():
        xn_ref[...] = _rms(x_ref[...], g_ref[...]).astype(BF16)

    o_ref[...] = _dot(xn_ref[...], w_ref[...]).astype(o_ref.dtype)


def _norm_matmul(x, g, w, *, tm, tn, out_dtype):
    m, k = x.shape
    n = w.shape[1]
    return pl.pallas_call(
        _norm_mm_kernel,
        grid=(m // tm, n // tn),
        in_specs=[
            pl.BlockSpec((tm, k), lambda i, j: (i, 0)),
            pl.BlockSpec((1, k), lambda i, j: (0, 0)),
            pl.BlockSpec((k, tn), lambda i, j: (0, j)),
        ],
        out_specs=pl.BlockSpec((tm, tn), lambda i, j: (i, j)),
        out_shape=jax.ShapeDtypeStruct((m, n), out_dtype),
        scratch_shapes=[pltpu.VMEM((tm, k), BF16)],
        compiler_params=_cparams(("parallel", "arbitrary")),
        name="norm_matmul",
    )(x, g, w)


def _proj_res_kernel(a_ref, b_ref, wa_ref, wb_ref, h_ref, o_ref):
    o_ref[...] = h_ref[...] + _dot(a_ref[...], wa_ref[...]) + _dot(b_ref[...], wb_ref[...])


def _proj_residual(a, b, wa, wb, h, *, tm, tn):
    m, n = h.shape
    ka, kb = a.shape[1], b.shape[1]
    return pl.pallas_call(
        _proj_res_kernel,
        grid=(m // tm, n // tn),
        in_specs=[
            pl.BlockSpec((tm, ka), lambda i, j: (i, 0)),
            pl.BlockSpec((tm, kb), lambda i, j: (i, 0)),
            pl.BlockSpec((ka, tn), lambda i, j: (0, j)),
            pl.BlockSpec((kb, tn), lambda i, j: (0, j)),
            pl.BlockSpec((tm, tn), lambda i, j: (i, j)),
        ],
        out_specs=pl.BlockSpec((tm, tn), lambda i, j: (i, j)),
        out_shape=jax.ShapeDtypeStruct((m, n), F32),
        compiler_params=_cparams(("parallel", "arbitrary")),
        name="proj_residual",
    )(a, b, wa, wb, h)


def _ffn_kernel(x_ref, g_ref, wg_ref, wu_ref, wd_ref, o_ref, xn_ref):
    @pl.when(pl.program_id(1) == 0)
    def _():
        x = x_ref[...]
        xn_ref[...] = _rms(x, g_ref[...]).astype(BF16)
        o_ref[...] = x

    xn = xn_ref[...]
    a = _dot(xn, wg_ref[...])
    b = _dot(xn, wu_ref[...])
    act = (a * _sigmoid(a) * b).astype(BF16)
    o_ref[...] += _dot(act, wd_ref[...])


def _ffn(x, g, wg, wu, wd, *, tm, tf):
    m, d = x.shape
    f = wg.shape[1]
    return pl.pallas_call(
        _ffn_kernel,
        grid=(m // tm, f // tf),
        in_specs=[
            pl.BlockSpec((tm, d), lambda i, j: (i, 0)),
            pl.BlockSpec((1, d), lambda i, j: (0, 0)),
            pl.BlockSpec((d, tf), lambda i, j: (0, j)),
            pl.BlockSpec((d, tf), lambda i, j: (0, j)),
            pl.BlockSpec((tf, d), lambda i, j: (j, 0)),
        ],
        out_specs=pl.BlockSpec((tm, d), lambda i, j: (i, 0)),
        out_shape=jax.ShapeDtypeStruct((m, d), F32),
        scratch_shapes=[pltpu.VMEM((tm, d), BF16)],
        compiler_params=_cparams(("parallel", "arbitrary")),
        name="ffn",
    )(x, g, wg, wu, wd)


def _final_norm_kernel(x_ref, g_ref, o_ref):
    o_ref[...] = _rms(x_ref[...], g_ref[...])


def _final_norm(x, g, *, tm):
    m, d = x.shape
    return pl.pallas_call(
        _final_norm_kernel,
        grid=(m // tm,),
        in_specs=[pl.BlockSpec((tm, d), lambda i: (i, 0)), pl.BlockSpec((1, d), lambda i: (0, 0))],
        out_specs=pl.BlockSpec((tm, d), lambda i: (i, 0)),
        out_shape=jax.ShapeDtypeStruct((m, d), F32),
        compiler_params=_cparams(("parallel",)),
        name="final_norm",
    )(x, g)


def _fill_history(buf_ref, prev, cur, hist, is_first):
    buf_ref[pl.ds(0, hist), :] = jnp.where(is_first, 0.0, prev)
    buf_ref[pl.ds(hist, cur.shape[0]), :] = cur


def _conv_kernel(bg_ref, cg_ref, hh_ref, cgp_ref, hhp_ref, w_ref, o_ref, buf_ref):
    ts = o_ref.shape[1]
    u = cg_ref[0] * hh_ref[0]
    up = cgp_ref[0] * hhp_ref[0]
    _fill_history(buf_ref, up, u, SUBLANES, pl.program_id(1) == 0)
    u1 = buf_ref[pl.ds(SUBLANES - 1, ts), :]
    u2 = buf_ref[pl.ds(SUBLANES - 2, ts), :]
    w = w_ref[...]
    y = w[2:3] * u + w[1:2] * u1 + w[0:1] * u2
    o_ref[0] = (bg_ref[0] * y).astype(o_ref.dtype)


def _conv_mixer(p, conv_w_t, *, ts, tc):
    b, s, _ = p.shape
    nc = CONV_DIM // tc
    rb = ts // SUBLANES

    def cur(off):
        return pl.BlockSpec((1, ts, tc), lambda bi, i, c: (bi, i, off * nc + c))

    def prev(off):
        return pl.BlockSpec((1, SUBLANES, tc),
                            lambda bi, i, c: (bi, jnp.maximum(i * rb - 1, 0), off * nc + c))

    return pl.pallas_call(
        _conv_kernel,
        grid=(b, s // ts, nc),
        in_specs=[cur(0), cur(1), cur(2), prev(1), prev(2),
                  pl.BlockSpec((3, tc), lambda bi, i, c: (0, c))],
        out_specs=pl.BlockSpec((1, ts, tc), lambda bi, i, c: (bi, i, c)),
        out_shape=jax.ShapeDtypeStruct((b, s, CONV_DIM), BF16),
        scratch_shapes=[pltpu.VMEM((ts + SUBLANES, tc), F32)],
        compiler_params=_cparams(("parallel", "parallel", "parallel")),
        name="conv_mixer",
    )(p, p, p, p, p, conv_w_t)


def _head_sum(x):
    r = lax.broadcasted_iota(jnp.int32, (LANES, LANES), 0) // RWKV_HEAD
    c = lax.broadcasted_iota(jnp.int32, (LANES, LANES), 1) // RWKV_HEAD
    ones = jnp.where(r == c, 1.0, 0.0).astype(BF16)
    hi = x.astype(BF16)
    lo = (x - hi.astype(F32)).astype(BF16)
    parts = []
    for g in range(x.shape[1] // LANES):
        sl = slice(g * LANES, (g + 1) * LANES)
        parts.append(_dot(hi[:, sl], ones) + _dot(lo[:, sl], ones))
    return jnp.concatenate(parts, axis=1)


def _rwkv_prep_kernel(x_ref, xl_ref, xp_ref, xlp_ref, mu_ref, mul_ref, w0_ref, w2_ref, a0_ref,
                      a2_ref, g2_ref, kk_ref, ka_ref, rk_ref,
                      r_out, k_out, v_out, kkn_out, a_out, lw_out, g_out, bonus_out,
                      buf_ref, bufl_ref):
    ts = r_out.shape[1]
    first = pl.program_id(1) == 0
    d = RWKV_DIM

    x = x_ref[0]
    _fill_history(buf_ref, xp_ref[0], x, SUBLANES, first)
    xs = buf_ref[pl.ds(SUBLANES - 1, ts), :]
    xm = x + (xs - x) * mu_ref[...]
    xl = xl_ref[0]
    _fill_history(bufl_ref, xlp_ref[0], xl, SUBLANES, first)
    xls = bufl_ref[pl.ds(SUBLANES - 1, ts), :]
    xlm = xl + (xls - xl) * mul_ref[...]

    r = xm[:, 0:d]
    k = xm[:, d:2 * d]
    v = xm[:, 2 * d:3 * d]
    xw = xlm[:, 0:LANES]
    xa = xlm[:, LANES:2 * LANES]
    xg = xlm[:, 2 * LANES:4 * LANES]

    z = -(w0_ref[...] + _dot(jnp.tanh(xw).astype(BF16), w2_ref[...]))
    softplus = jnp.maximum(z, 0.0) + jnp.log(1.0 + jnp.exp(-jnp.abs(z)))
    w = -softplus - 0.5
    a = _sigmoid(a0_ref[...] + _dot(xa.astype(BF16), a2_ref[...]))
    g = _dot(_sigmoid(xg).astype(BF16), g2_ref[...])

    kk = k * kk_ref[...]
    nrm = jnp.sqrt(_head_sum(kk * kk))
    kk = kk / jnp.maximum(nrm, 1e-12)
    k = k * (1.0 + (a - 1.0) * ka_ref[...])
    bonus = _head_sum(r * k * rk_ref[...]) * v

    r_out[0] = r
    k_out[0] = k
    v_out[0] = v
    kkn_out[0] = kk
    a_out[0] = a
    lw_out[0] = -jnp.exp(w)
    g_out[0] = g
    bonus_out[0] = bonus


def _rwkv_prep(p, mu, mul, w0, w2, a0, a2, g2, k_k, k_a, r_k, *, ts):
    b, s, _ = p.shape
    d = RWKV_DIM
    rb = ts // SUBLANES
    row = lambda shape: pl.BlockSpec(shape, lambda bi, i: (0, 0))
    out = pl.BlockSpec((1, ts, d), lambda bi, i: (bi, i, 0))
    return pl.pallas_call(
        _rwkv_prep_kernel,
        grid=(b, s // ts),
        in_specs=[
            pl.BlockSpec((1, ts, 3 * d), lambda bi, i: (bi, i, 1)),
            pl.BlockSpec((1, ts, EV_LORA), lambda bi, i: (bi, i, 6 * d // EV_LORA)),
            pl.BlockSpec((1, SUBLANES, 3 * d), lambda bi, i: (bi, jnp.maximum(i * rb - 1, 0), 1)),
            pl.BlockSpec((1, SUBLANES, EV_LORA),
                         lambda bi, i: (bi, jnp.maximum(i * rb - 1, 0), 6 * d // EV_LORA)),
            row((1, 3 * d)), row((1, EV_LORA)), row((1, d)), row((LANES, d)), row((1, d)),
            row((LANES, d)), row((2 * LANES, d)), row((1, d)), row((1, d)), row((1, d)),
        ],
        out_specs=[out] * 8,
        out_shape=[jax.ShapeDtypeStruct((b, s, d), F32)] * 8,
        scratch_shapes=[pltpu.VMEM((ts + SUBLANES, 3 * d), F32),
                        pltpu.VMEM((ts + SUBLANES, EV_LORA), F32)],
        compiler_params=_cparams(("parallel", "parallel")),
        name="rwkv_prep",
    )(p, p, p, p, mu, mul, w0, w2, a0, a2, g2, k_k, k_a, r_k)


def _wkv_kernel(r_ref, k_ref, v_ref, kk_ref, a_ref, lw_ref, y_ref, t_ref):
    n = WKV_CHUNK
    ts = y_ref.shape[1]

    @pl.when(pl.program_id(2) == 0)
    def _():
        t_ref[...] = jnp.zeros_like(t_ref)

    head0 = lax.broadcasted_iota(jnp.int32, (n, LANES), 1) < RWKV_HEAD
    row = lax.broadcasted_iota(jnp.int32, (2 * n, 2 * n), 0)
    col = lax.broadcasted_iota(jnp.int32, (2 * n, 2 * n), 1)
    same = (row // n) == (col // n)
    x = row ^ col
    level = jnp.zeros_like(x)
    for bit in range(1, 6):
        level = level + jnp.where(x >= (1 << bit), 1, 0)
    level = jnp.where(same & (row > col), level, -1)
    strict = level >= 0
    incl = jnp.where(same & (row >= col), 1, 0) > 0
    incl2 = jnp.concatenate([incl, incl], axis=1)
    eye = jnp.where(row == col, 1.0, 0.0)
    tri = jnp.where(lax.broadcasted_iota(jnp.int32, (n, n), 0)
                    >= lax.broadcasted_iota(jnp.int32, (n, n), 1), 1.0, 0.0)

    def stack(t):
        z = jnp.zeros_like(t)
        return jnp.concatenate([jnp.where(head0, t, z), jnp.where(head0, z, t)], axis=0).astype(BF16)

    def chunk(c, t):
        sl = pl.ds(pl.multiple_of(c * n, n), n)
        r = r_ref[0, sl, :]
        k = k_ref[0, sl, :]
        v = v_ref[0, sl, :]
        kk = kk_ref[0, sl, :]
        a = a_ref[0, sl, :]
        lw = lw_ref[0, sl, :]
        cw = jnp.dot(tri, lw, preferred_element_type=F32, precision=lax.Precision.HIGHEST)
        cw_end = cw[n - 1:n, :]
        p_inv = jnp.exp(-cw)
        p_to_end = jnp.exp(cw_end - cw)
        kka = kk * a
        xa = stack(-kk * jnp.exp(cw - lw))
        xr = stack(r * jnp.exp(cw))
        yb = stack(kka * p_inv)
        yk = stack(k * p_inv)
        bt = stack(kka * p_to_end)
        kt = stack(k * p_to_end)
        vs = stack(v)

        xar = jnp.concatenate([xa, xr], axis=0)
        sc = _dot_nt(xar, jnp.concatenate([yb, yk], axis=0))
        a_ab = jnp.where(strict, sc[0:2 * n, 0:2 * n], 0.0)
        a_ak = jnp.where(strict, sc[0:2 * n, 2 * n:4 * n], 0.0)
        a_r = jnp.where(incl2, sc[2 * n:4 * n, :], 0.0).astype(BF16)

        m = eye + jnp.where(level == 0, a_ab, 0.0)
        for lev in range(1, 6):
            e = jnp.where(level == lev, a_ab, 0.0).astype(BF16)
            mb = m.astype(BF16)
            m = m + _dot(mb, _dot(e, mb).astype(BF16))

        tb = t.astype(BF16)
        xt = _dot(xar, tb)
        rhs = xt[0:2 * n] + _dot(a_ak.astype(BF16), vs)
        u = _dot(m.astype(BF16), rhs.astype(BF16)).astype(BF16)
        uv = jnp.concatenate([u, vs], axis=0)
        ys = xt[2 * n:4 * n] + _dot(a_r, uv)
        y_ref[0, sl, :] = ys[0:n] + ys[n:2 * n]
        decay = jnp.exp(cw_end)
        t_new = t * jnp.transpose(jnp.broadcast_to(decay, (LANES, LANES))) \
            + _dot_tn(jnp.concatenate([bt, kt], axis=0), uv)
        return t_new

    t_ref[...] = lax.fori_loop(0, ts // n, chunk, t_ref[...])


def _wkv(r, k, v, kk, a, lw, *, ts):
    b, s, d = r.shape
    spec = pl.BlockSpec((1, ts, LANES), lambda bi, p, i: (bi, i, p))
    return pl.pallas_call(
        _wkv_kernel,
        grid=(b, d // LANES, s // ts),
        in_specs=[spec] * 6,
        out_specs=spec,
        out_shape=jax.ShapeDtypeStruct((b, s, d), F32),
        scratch_shapes=[pltpu.VMEM((LANES, LANES), F32)],
        compiler_params=_cparams(("parallel", "parallel", "arbitrary")),
        name="wkv7",
    )(r, k, v, kk, a, lw)


def _rwkv_post_kernel(y_ref, bonus_ref, g_ref, lnw_ref, lnb_ref, o_ref):
    y = y_ref[...]
    inv_n = 1.0 / RWKV_HEAD
    mean = _head_sum(y) * inv_n
    yc = y - mean
    var = _head_sum(yc * yc) * inv_n
    yn = yc * lax.rsqrt(var + GN_EPS) * lnw_ref[...] + lnb_ref[...]
    o_ref[...] = ((yn + bonus_ref[...]) * g_ref[...]).astype(o_ref.dtype)


def _rwkv_post(y, bonus, g, ln_w, ln_b, *, tm):
    m, d = y.shape
    blk = pl.BlockSpec((tm, d), lambda i: (i, 0))
    row = pl.BlockSpec((1, d), lambda i: (0, 0))
    return pl.pallas_call(
        _rwkv_post_kernel,
        grid=(m // tm,),
        in_specs=[blk, blk, blk, row, row],
        out_specs=blk,
        out_shape=jax.ShapeDtypeStruct((m, d), BF16),
        compiler_params=_cparams(("parallel",)),
        name="rwkv_post",
    )(y, bonus, g, ln_w, ln_b)


def _pool_kernel(u_ref, up_ref, w_ref, sc_ref, o_ref, buf_ref):
    ts = o_ref.shape[1]
    hist = max(POOL_WINDOWS)
    i = pl.program_id(1)
    u = u_ref[0]
    _fill_history(buf_ref, up_ref[0], u, hist, i == 0)
    pos = i * ts + lax.broadcasted_iota(jnp.int32, (ts, 1), 0)
    outs = []
    for gi, win in enumerate(POOL_WINDOWS):
        cols = pl.ds(gi * POOL_GROUP, POOL_GROUP)
        acc = buf_ref[pl.ds(hist, ts), cols]
        for j in range(1, win):
            acc = acc + buf_ref[pl.ds(hist - j, ts), cols]
        count = jnp.minimum(pos + 1, win).astype(F32)
        dg = acc / count - u[:, gi * POOL_GROUP:(gi + 1) * POOL_GROUP]
        outs.append(_dot(dg.astype(BF16), w_ref[gi]))
    y = jnp.concatenate(outs, axis=1) * sc_ref[...]
    o_ref[0] = y.astype(o_ref.dtype)


def _pool_mixer(p, pool_w, pool_scale, *, ts):
    b, s, _ = p.shape
    hist = max(POOL_WINDOWS)
    rb = ts // hist
    return pl.pallas_call(
        _pool_kernel,
        grid=(b, s // ts),
        in_specs=[
            pl.BlockSpec((1, ts, POOL_DIM), lambda bi, i: (bi, i, 0)),
            pl.BlockSpec((1, hist, POOL_DIM), lambda bi, i: (bi, jnp.maximum(i * rb - 1, 0), 0)),
            pl.BlockSpec((len(POOL_WINDOWS), POOL_GROUP, POOL_GROUP), lambda bi, i: (0, 0, 0)),
            pl.BlockSpec((1, POOL_DIM), lambda bi, i: (0, 0)),
        ],
        out_specs=pl.BlockSpec((1, ts, POOL_DIM), lambda bi, i: (bi, i, 0)),
        out_shape=jax.ShapeDtypeStruct((b, s, POOL_DIM), BF16),
        scratch_shapes=[pltpu.VMEM((ts + hist, POOL_DIM), F32)],
        compiler_params=_cparams(("parallel", "parallel")),
        name="pool_mixer",
    )(p, p, pool_w, pool_scale)


def _mla_prep_kernel(ql_ref, kvl_ref, kpe_ref, kper_ref, ang_ref, qn_ref, kvn_ref,
                     wq_ref, wqr_ref, wk_ref, wv_ref, q_out, k_out, v_out):
    cos = jnp.cos(ang_ref[...])
    sin = jnp.sin(ang_ref[...])
    qn = _rms(ql_ref[...], qn_ref[...]).astype(BF16)
    kvn = _rms(kvl_ref[...], kvn_ref[...]).astype(BF16)
    scale = (QK_NOPE + QK_ROPE) ** -0.5
    qa = _dot(qn, wq_ref[...])
    qr = _dot(qn, wqr_ref[...])
    ka = _dot(kvn, wk_ref[...])
    kpe = kpe_ref[...] * cos + kper_ref[...] * sin
    for h in range(MLA_HEADS):
        lo = h * QK_PAD
        q_out[:, lo:lo + LANES] = (qa[:, lo:lo + LANES] * scale).astype(BF16)
        q_pe = qa[:, lo + LANES:lo + 2 * LANES] * cos + qr[:, h * LANES:(h + 1) * LANES] * sin
        q_out[:, lo + LANES:lo + 2 * LANES] = (q_pe * scale).astype(BF16)
        k_out[:, lo:lo + LANES] = ka[:, lo:lo + LANES].astype(BF16)
        k_out[:, lo + LANES:lo + 2 * LANES] = kpe.astype(BF16)
    v_out[...] = _dot(kvn, wv_ref[...]).astype(BF16)


def _mla_prep(p, ang, q_norm, kv_norm, wq, wqr, wk, wv, *, tm):
    m = p.shape[0]
    hq = MLA_HEADS * QK_PAD
    hv = MLA_HEADS * V_HEAD
    full = lambda a: pl.BlockSpec(a.shape, lambda i: (0, 0))
    return pl.pallas_call(
        _mla_prep_kernel,
        grid=(m // tm,),
        in_specs=[
            pl.BlockSpec((tm, Q_LORA), lambda i: (i, 1)),
            pl.BlockSpec((tm, KV_LORA), lambda i: (i, 2)),
            pl.BlockSpec((tm, LANES), lambda i: (i, 12)),
            pl.BlockSpec((tm, LANES), lambda i: (i, 13)),
            pl.BlockSpec((tm, LANES), lambda i: (i, 0)),
            full(q_norm), full(kv_norm), full(wq), full(wqr), full(wk), full(wv),
        ],
        out_specs=[pl.BlockSpec((tm, hq), lambda i: (i, 0)),
                   pl.BlockSpec((tm, hq), lambda i: (i, 0)),
                   pl.BlockSpec((tm, hv), lambda i: (i, 0))],
        out_shape=[jax.ShapeDtypeStruct((m, hq), BF16),
                   jax.ShapeDtypeStruct((m, hq), BF16),
                   jax.ShapeDtypeStruct((m, hv), BF16)],
        compiler_params=_cparams(("parallel",)),
        name="mla_prep",
    )(p, p, p, p, ang, q_norm, kv_norm, wq, wqr, wk, wv)


def _attn_kernel(q_ref, k_ref, v_ref, o_ref, *, tq):
    qi = pl.program_id(2)
    q = q_ref[0]

    def block(j, carry, masked):
        m, l, acc = carry
        sl = pl.ds(pl.multiple_of(j * tq, tq), tq)
        s = _dot_nt(q, k_ref[0, sl, :])
        if masked:
            r = lax.broadcasted_iota(jnp.int32, (tq, tq), 0)
            c = lax.broadcasted_iota(jnp.int32, (tq, tq), 1)
            s = jnp.where(r >= c, s, -1e30)
        m_new = jnp.maximum(m, jnp.max(s, axis=-1, keepdims=True))
        alpha = jnp.exp(m - m_new)
        p = jnp.exp(s - m_new)
        l = alpha * l + jnp.sum(p, axis=-1, keepdims=True)
        acc = alpha * acc + _dot(p.astype(BF16), v_ref[0, sl, :])
        return m_new, l, acc

    init = (jnp.full((tq, 1), -1e30, F32), jnp.zeros((tq, 1), F32), jnp.zeros((tq, V_HEAD), F32))
    carry = lax.fori_loop(0, qi, lambda j, c: block(j, c, False), init)
    _, l, acc = block(qi, carry, True)
    o_ref[0] = (acc / l).astype(o_ref.dtype)


def _attention(q, k, v, *, tq):
    b, s, _ = q.shape
    return pl.pallas_call(
        functools.partial(_attn_kernel, tq=tq),
        grid=(b, MLA_HEADS, s // tq),
        in_specs=[
            pl.BlockSpec((1, tq, QK_PAD), lambda bi, h, i: (bi, i, h)),
            pl.BlockSpec((1, s, QK_PAD), lambda bi, h, i: (bi, 0, h)),
            pl.BlockSpec((1, s, V_HEAD), lambda bi, h, i: (bi, 0, h)),
        ],
        out_specs=pl.BlockSpec((1, tq, V_HEAD), lambda bi, h, i: (bi, i, h)),
        out_shape=jax.ShapeDtypeStruct((b, s, MLA_HEADS * V_HEAD), BF16),
        compiler_params=_cparams(("parallel", "parallel", "arbitrary")),
        name="mla_attention",
    )(q, k, v)


def _pad_cols(w, n):
    return jnp.pad(w, ((0, 0), (0, n - w.shape[1])))


def _pad_rows(w, n):
    return jnp.pad(w, ((0, n - w.shape[0]), (0, 0)))


def _rot_cols(w):
    half = w.shape[-1] // 2
    return jnp.concatenate([-w[..., half:], w[..., :half]], axis=-1)


def _even_weights(w_in, mu):
    c = 3 * CONV_DIM + 3 * RWKV_DIM
    xw = w_in[:, c:c + W_LORA]
    xa = w_in[:, c + W_LORA:c + W_LORA + A_LORA]
    xg = w_in[:, c + W_LORA + A_LORA:]
    w = jnp.concatenate([w_in[:, :c], _pad_cols(xw, LANES), _pad_cols(xa, LANES),
                         _pad_cols(xg, 2 * LANES)], axis=1).astype(BF16)
    r = 3 * RWKV_DIM
    mu_l = jnp.concatenate([
        jnp.pad(mu[r:r + W_LORA], (0, LANES - W_LORA)),
        jnp.pad(mu[r + W_LORA:r + W_LORA + A_LORA], (0, LANES - A_LORA)),
        jnp.pad(mu[r + W_LORA + A_LORA:], (0, 2 * LANES - G_LORA))])
    return w, mu[:r][None, :], mu_l[None, :]


def _odd_weights(w_in, w_uq, w_ukv):
    o3 = POOL_DIM + Q_LORA + KV_LORA
    kpe = w_in[:, o3:]
    w = jnp.concatenate([w_in[:, :o3], _pad_cols(kpe, LANES), _pad_cols(_rot_cols(kpe), LANES)],
                        axis=1).astype(BF16)
    uq = w_uq.reshape(Q_LORA, MLA_HEADS, QK_NOPE + QK_ROPE)
    zq = jnp.zeros((Q_LORA, MLA_HEADS, QK_PAD - QK_NOPE - QK_ROPE), F32)
    wq = jnp.concatenate([uq, zq], axis=-1).reshape(Q_LORA, MLA_HEADS * QK_PAD).astype(BF16)
    zr = jnp.zeros((Q_LORA, MLA_HEADS, LANES - QK_ROPE), F32)
    wqr = jnp.concatenate([_rot_cols(uq[..., QK_NOPE:]), zr], axis=-1)
    wqr = wqr.reshape(Q_LORA, MLA_HEADS * LANES).astype(BF16)
    ukv = w_ukv.reshape(KV_LORA, MLA_HEADS, QK_NOPE + V_HEAD)
    zk = jnp.zeros((KV_LORA, MLA_HEADS, QK_PAD - QK_NOPE), F32)
    wk = jnp.concatenate([ukv[..., :QK_NOPE], zk], axis=-1).reshape(KV_LORA, MLA_HEADS * QK_PAD)
    wv = ukv[..., QK_NOPE:].reshape(KV_LORA, MLA_HEADS * V_HEAD)
    return w, wq, wqr, wk.astype(BF16), wv.astype(BF16)


def _tile(n, pref):
    return pref if n % pref == 0 else n


def kernel(x, positions, ev_norm, ev_w_in, ev_conv_w, ev_mu, ev_w0, ev_w2, ev_a0, ev_a2, ev_g2, ev_k_k, ev_k_a, ev_r_k, ev_ln_w, ev_ln_b, ev_w_out, od_norm, od_w_in, od_pool_w, od_pool_scale, od_q_norm, od_w_uq, od_kv_norm, od_w_ukv, od_w_out, ffn_norm, ffn_w_gate, ffn_w_up, ffn_w_down, final_norm):
    b, s, d = x.shape
    m = b * s
    depth = ffn_norm.shape[0]
    tm = _tile(m, 512)
    ts = _tile(s, 512)
    ts_prep = _tile(s, 256)

    inv = 1.0 / (ROPE_THETA ** (jnp.arange(0, QK_ROPE, 2, dtype=F32) / QK_ROPE))
    inv = jnp.concatenate([inv, inv, jnp.zeros((LANES - QK_ROPE,), F32)])
    ang = (positions.astype(F32)[..., None] * inv).reshape(m, LANES)

    h = x.reshape(m, d)
    for layer in range(depth):
        j = layer // 2
        if layer % 2 == 0:
            w_in, mu, mu_l = _even_weights(ev_w_in[j], ev_mu[j])
            p = _norm_matmul(h, ev_norm[j][None, :], w_in, tm=tm, tn=512, out_dtype=F32)
            p = p.reshape(b, s, EV_COLS)
            ya = _conv_mixer(p, ev_conv_w[j].T, ts=ts, tc=512)
            row = lambda t: t[None, :]
            r, k, v, kk, a, lw, g, bonus = _rwkv_prep(
                p, mu, mu_l, row(ev_w0[j]), _pad_rows(ev_w2[j], LANES).astype(BF16), row(ev_a0[j]),
                _pad_rows(ev_a2[j], LANES).astype(BF16), _pad_rows(ev_g2[j], 2 * LANES).astype(BF16),
                row(ev_k_k[j]), row(ev_k_a[j]), row(ev_r_k[j].reshape(-1)), ts=ts_prep)
            y = _wkv(r, k, v, kk, a, lw, ts=ts)
            yb = _rwkv_post(y.reshape(m, -1), bonus.reshape(m, -1), g.reshape(m, -1),
                            row(ev_ln_w[j]), row(ev_ln_b[j]), tm=tm)
            w_out = ev_w_out[j].astype(BF16)
            h = _proj_residual(ya.reshape(m, -1), yb, w_out[:CONV_DIM], w_out[CONV_DIM:], h,
                               tm=tm, tn=512)
        else:
            w_in, wq, wqr, wk, wv = _odd_weights(od_w_in[j], od_w_uq[j], od_w_ukv[j])
            p = _norm_matmul(h, od_norm[j][None, :], w_in, tm=tm, tn=256, out_dtype=F32)
            yc = _pool_mixer(p.reshape(b, s, OD_COLS), od_pool_w[j].astype(BF16),
                             od_pool_scale[j][None, :], ts=ts)
            q, kq, vv = _mla_prep(p, ang, od_q_norm[j][None, :], od_kv_norm[j][None, :],
                                  wq, wqr, wk, wv, tm=tm)
            hq = MLA_HEADS * QK_PAD
            yd = _attention(q.reshape(b, s, hq), kq.reshape(b, s, hq),
                            vv.reshape(b, s, MLA_HEADS * V_HEAD), tq=ts)
            w_out = od_w_out[j].astype(BF16)
            h = _proj_residual(yc.reshape(m, -1), yd.reshape(m, -1), w_out[:POOL_DIM],
                               w_out[POOL_DIM:], h, tm=tm, tn=512)
        h = _ffn(h, ffn_norm[layer][None, :], ffn_w_gate[layer].astype(BF16),
                 ffn_w_up[layer].astype(BF16), ffn_w_down[layer].astype(BF16), tm=tm, tf=512)
    return _final_norm(h, final_norm[None, :], tm=tm).reshape(b, s, d)
```

```python
import functools

import jax
import jax.numpy as jnp
from jax import lax
from jax.experimental import pallas as pl
from jax.experimental.pallas import tpu as pltpu

F32 = jnp.float32
BF16 = jnp.bfloat16

D_MODEL = 2048
NORM_EPS = 1e-6
CONV_DIM = 1024
RWKV_DIM = 1024
RWKV_HEAD = 64
W_LORA = 64
A_LORA = 64
G_LORA = 160
GN_EPS = 64e-5
POOL_WINDOWS = (2, 4, 8, 16)
POOL_GROUP = 128
POOL_DIM = 512
MLA_HEADS = 12
Q_LORA = 512
KV_LORA = 512
QK_NOPE = 128
QK_ROPE = 64
V_HEAD = 128
ROPE_THETA = 10000.0
D_FF = 5632
LOG2_E = 1.4426950408889634

LANES = 128
SUBLANES = 8
VMEM_LIMIT = 56 * 1024 * 1024
QK_PAD = 256
WKV_CHUNK = 64
EV_COLS = 6656
EV_LORA = 512
OD_COLS = 1792


def _cparams(sem):
    return pltpu.CompilerParams(dimension_semantics=sem, vmem_limit_bytes=VMEM_LIMIT)


def _rms(x, g):
    ms = jnp.mean(x * x, axis=-1, keepdims=True)
    return x * lax.rsqrt(ms + NORM_EPS) * g


def _sigmoid(x):
    return 1.0 / (1.0 + jnp.exp(-x))


def _dot(a, b):
    return jnp.dot(a, b, preferred_element_type=F32)


def _dot_nt(a, b):
    return lax.dot_general(a, b, (((1,), (1,)), ((), ())), preferred_element_type=F32)


def _dot_tn(a, b):
    return lax.dot_general(a, b, (((0,), (0,)), ((), ())), preferred_element_type=F32)


def _norm_mm_kernel(x_ref, g_ref, w_ref, o_ref, xn_ref):
    @pl.when(pl.program_id(1) == 0)
    def _():
        xn_ref[...] = _rms(x_ref[...], g_ref[...]).astype(BF16)

    o_ref[...] = _dot(xn_ref[...], w_ref[...]).astype(o_ref.dtype)


def _norm_matmul(x, g, w, *, tm, tn, out_dtype):
    m, k = x.shape
    n = w.shape[1]
    return pl.pallas_call(
        _norm_mm_kernel,
        grid=(m // tm, n // tn),
        in_specs=[
            pl.BlockSpec((tm, k), lambda i, j: (i, 0)),
            pl.BlockSpec((1, k), lambda i, j: (0, 0)),
            pl.BlockSpec((k, tn), lambda i, j: (0, j)),
        ],
        out_specs=pl.BlockSpec((tm, tn), lambda i, j: (i, j)),
        out_shape=jax.ShapeDtypeStruct((m, n), out_dtype),
        scratch_shapes=[pltpu.VMEM((tm, k), BF16)],
        compiler_params=_cparams(("parallel", "arbitrary")),
        name="norm_matmul",
    )(x, g, w)


def _proj_res_kernel(a_ref, b_ref, wa_ref, wb_ref, h_ref, o_ref):
    o_ref[...] = h_ref[...] + _dot(a_ref[...], wa_ref[...]) + _dot(b_ref[...], wb_ref[...])


def _proj_residual(a, b, wa, wb, h, *, tm, tn):
    m, n = h.shape
    ka, kb = a.shape[1], b.shape[1]
    return pl.pallas_call(
        _proj_res_kernel,
        grid=(m // tm, n // tn),
        in_specs=[
            pl.BlockSpec((tm, ka), lambda i, j: (i, 0)),
            pl.BlockSpec((tm, kb), lambda i, j: (i, 0)),
            pl.BlockSpec((ka, tn), lambda i, j: (0, j)),
            pl.BlockSpec((kb, tn), lambda i, j: (0, j)),
            pl.BlockSpec((tm, tn), lambda i, j: (i, j)),
        ],
        out_specs=pl.BlockSpec((tm, tn), lambda i, j: (i, j)),
        out_shape=jax.ShapeDtypeStruct((m, n), F32),
        compiler_params=_cparams(("parallel", "arbitrary")),
        name="proj_residual",
    )(a, b, wa, wb, h)


def _ffn_kernel(x_ref, g_ref, wg_ref, wu_ref, wd_ref, o_ref, xn_ref):
    @pl.when(pl.program_id(1) == 0)
    def _():
        x = x_ref[...]
        xn_ref[...] = _rms(x, g_ref[...]).astype(BF16)
        o_ref[...] = x

    xn = xn_ref[...]
    a = _dot(xn, wg_ref[...])
    b = _dot(xn, wu_ref[...])
    act = (a * _sigmoid(a) * b).astype(BF16)
    o_ref[...] += _dot(act, wd_ref[...])


def _ffn(x, g, wg, wu, wd, *, tm, tf):
    m, d = x.shape
    f = wg.shape[1]
    return pl.pallas_call(
        _ffn_kernel,
        grid=(m // tm, f // tf),
        in_specs=[
            pl.BlockSpec((tm, d), lambda i, j: (i, 0)),
            pl.BlockSpec((1, d), lambda i, j: (0, 0)),
            pl.BlockSpec((d, tf), lambda i, j: (0, j)),
            pl.BlockSpec((d, tf), lambda i, j: (0, j)),
            pl.BlockSpec((tf, d), lambda i, j: (j, 0)),
        ],
        out_specs=pl.BlockSpec((tm, d), lambda i, j: (i, 0)),
        out_shape=jax.ShapeDtypeStruct((m, d), F32),
        scratch_shapes=[pltpu.VMEM((tm, d), BF16)],
        compiler_params=_cparams(("parallel", "arbitrary")),
        name="ffn",
    )(x, g, wg, wu, wd)


def _final_norm_kernel(x_ref, g_ref, o_ref):
    o_ref[...] = _rms(x_ref[...], g_ref[...])


def _final_norm(x, g, *, tm):
    m, d = x.shape
    return pl.pallas_call(
        _final_norm_kernel,
        grid=(m // tm,),
        in_specs=[pl.BlockSpec((tm, d), lambda i: (i, 0)), pl.BlockSpec((1, d), lambda i: (0, 0))],
        out_specs=pl.BlockSpec((tm, d), lambda i: (i, 0)),
        out_shape=jax.ShapeDtypeStruct((m, d), F32),
        compiler_params=_cparams(("parallel",)),
        name="final_norm",
    )(x, g)


def _fill_history(buf_ref, prev, cur, hist, is_first):
    buf_ref[pl.ds(0, hist), :] = jnp.where(is_first, 0.0, prev)
    buf_ref[pl.ds(hist, cur.shape[0]), :] = cur


def _conv_kernel(bg_ref, cg_ref, hh_ref, cgp_ref, hhp_ref, w_ref, o_ref, buf_ref):
    ts = o_ref.shape[1]
    u = cg_ref[0] * hh_ref[0]
    up = cgp_ref[0] * hhp_ref[0]
    _fill_history(buf_ref, up, u, SUBLANES, pl.program_id(1) == 0)
    u1 = buf_ref[pl.ds(SUBLANES - 1, ts), :]
    u2 = buf_ref[pl.ds(SUBLANES - 2, ts), :]
    w = w_ref[...]
    y = w[2:3] * u + w[1:2] * u1 + w[0:1] * u2
    o_ref[0] = (bg_ref[0] * y).astype(o_ref.dtype)


def _conv_mixer(p, conv_w_t, *, ts, tc):
    b, s, _ = p.shape
    nc = CONV_DIM // tc
    rb = ts // SUBLANES

    def cur(off):
        return pl.BlockSpec((1, ts, tc), lambda bi, i, c: (bi, i, off * nc + c))

    def prev(off):
        return pl.BlockSpec((1, SUBLANES, tc),
                            lambda bi, i, c: (bi, jnp.maximum(i * rb - 1, 0), off * nc + c))

    return pl.pallas_call(
        _conv_kernel,
        grid=(b, s // ts, nc),
        in_specs=[cur(0), cur(1), cur(2), prev(1), prev(2),
                  pl.BlockSpec((3, tc), lambda bi, i, c: (0, c))],
        out_specs=pl.BlockSpec((1, ts, tc), lambda bi, i, c: (bi, i, c)),
        out_shape=jax.ShapeDtypeStruct((b, s, CONV_DIM), BF16),
        scratch_shapes=[pltpu.VMEM((ts + SUBLANES, tc), F32)],
        compiler_params=_cparams(("parallel", "parallel", "parallel")),
        name="conv_mixer",
    )(p, p, p, p, p, conv_w_t)


def _head_sum(x):
    r = lax.broadcasted_iota(jnp.int32, (LANES, LANES), 0) // RWKV_HEAD
    c = lax.broadcasted_iota(jnp.int32, (LANES, LANES), 1) // RWKV_HEAD
    ones = jnp.where(r == c, 1.0, 0.0).astype(BF16)
    hi = x.astype(BF16)
    lo = (x - hi.astype(F32)).astype(BF16)
    parts = []
    for g in range(x.shape[1] // LANES):
        sl = slice(g * LANES, (g + 1) * LANES)
        parts.append(_dot(hi[:, sl], ones) + _dot(lo[:, sl], ones))
    return jnp.concatenate(parts, axis=1)


def _rwkv_prep_kernel(x_ref, xl_ref, xp_ref, xlp_ref, mu_ref, mul_ref, w0_ref, w2_ref, a0_ref,
                      a2_ref, g2_ref, kk_ref, ka_ref, rk_ref,
                      r_out, k_out, v_out, kkn_out, a_out, lw_out, g_out, bonus_out,
                      buf_ref, bufl_ref):
    ts = r_out.shape[1]
    first = pl.program_id(1) == 0
    d = RWKV_DIM

    x = x_ref[0]
    _fill_history(buf_ref, xp_ref[0], x, SUBLANES, first)
    xs = buf_ref[pl.ds(SUBLANES - 1, ts), :]
    xm = x + (xs - x) * mu_ref[...]
    xl = xl_ref[0]
    _fill_history(bufl_ref, xlp_ref[0], xl, SUBLANES, first)
    xls = bufl_ref[pl.ds(SUBLANES - 1, ts), :]
    xlm = xl + (xls - xl) * mul_ref[...]

    r = xm[:, 0:d]
    k = xm[:, d:2 * d]
    v = xm[:, 2 * d:3 * d]
    xw = xlm[:, 0:LANES]
    xa = xlm[:, LANES:2 * LANES]
    xg = xlm[:, 2 * LANES:4 * LANES]

    z = -(w0_ref[...] + _dot(jnp.tanh(xw).astype(BF16), w2_ref[...]))
    softplus = jnp.maximum(z, 0.0) + jnp.log(1.0 + jnp.exp(-jnp.abs(z)))
    w = -softplus - 0.5
    a = _sigmoid(a0_ref[...] + _dot(xa.astype(BF16), a2_ref[...]))
    g = _dot(_sigmoid(xg).astype(BF16), g2_ref[...])

    kk = k * kk_ref[...]
    nrm = jnp.sqrt(_head_sum(kk * kk))
    kk = kk / jnp.maximum(nrm, 1e-12)
    k = k * (1.0 + (a - 1.0) * ka_ref[...])
    bonus = _head_sum(r * k * rk_ref[...]) * v

    r_out[0] = r
    k_out[0] = k
    v_out[0] = v
    kkn_out[0] = kk
    a_out[0] = a
    lw_out[0] = -jnp.exp(w)
    g_out[0] = g
    bonus_out[0] = bonus


def _rwkv_prep(p, mu, mul, w0, w2, a0, a2, g2, k_k, k_a, r_k, *, ts):
    b, s, _ = p.shape
    d = RWKV_DIM
    rb = ts // SUBLANES
    row = lambda shape: pl.BlockSpec(shape, lambda bi, i: (0, 0))
    out = pl.BlockSpec((1, ts, d), lambda bi, i: (bi, i, 0))
    return pl.pallas_call(
        _rwkv_prep_kernel,
        grid=(b, s // ts),
        in_specs=[
            pl.BlockSpec((1, ts, 3 * d), lambda bi, i: (bi, i, 1)),
            pl.BlockSpec((1, ts, EV_LORA), lambda bi, i: (bi, i, 6 * d // EV_LORA)),
            pl.BlockSpec((1, SUBLANES, 3 * d), lambda bi, i: (bi, jnp.maximum(i * rb - 1, 0), 1)),
            pl.BlockSpec((1, SUBLANES, EV_LORA),
                         lambda bi, i: (bi, jnp.maximum(i * rb - 1, 0), 6 * d // EV_LORA)),
            row((1, 3 * d)), row((1, EV_LORA)), row((1, d)), row((LANES, d)), row((1, d)),
            row((LANES, d)), row((2 * LANES, d)), row((1, d)), row((1, d)), row((1, d)),
        ],
        out_specs=[out] * 8,
        out_shape=[jax.ShapeDtypeStruct((b, s, d), F32)] * 8,
        scratch_shapes=[pltpu.VMEM((ts + SUBLANES, 3 * d), F32),
                        pltpu.VMEM((ts + SUBLANES, EV_LORA), F32)],
        compiler_params=_cparams(("parallel", "parallel")),
        name="rwkv_prep",
    )(p, p, p, p, mu, mul, w0, w2, a0, a2, g2, k_k, k_a, r_k)


WKV_LEVELS = 6
_MASK_STRICT = WKV_LEVELS
_MASK_INCL = WKV_LEVELS + 1
_MASK_EYE = WKV_LEVELS + 2


def _wkv_kernel(r_ref, k_ref, v_ref, kk_ref, a_ref, lw_ref, y_ref, t_ref, mask_ref):
    n = WKV_CHUNK
    ts = y_ref.shape[1]
    npair = y_ref.shape[2] // LANES

    @pl.when(pl.program_id(2) == 0)
    def _():
        t_ref[...] = jnp.zeros_like(t_ref)
        row = lax.broadcasted_iota(jnp.int32, (2 * n, 2 * n), 0)
        col = lax.broadcasted_iota(jnp.int32, (2 * n, 2 * n), 1)
        same = (row // n) == (col // n)
        x = row ^ col
        level = jnp.zeros_like(x)
        for bit in range(1, WKV_LEVELS):
            level = level + jnp.where(x >= (1 << bit), 1, 0)
        level = jnp.where(same & (row > col), level, -1)
        for lev in range(WKV_LEVELS):
            mask_ref[lev] = jnp.where(level == lev, 1.0, 0.0)
        mask_ref[_MASK_STRICT] = jnp.where(level >= 0, 1.0, 0.0)
        mask_ref[_MASK_INCL] = jnp.where(same & (row >= col), 1.0, 0.0)
        mask_ref[_MASK_EYE] = jnp.where(row == col, 1.0, 0.0)

    head0 = lax.broadcasted_iota(jnp.int32, (n, LANES), 1) < RWKV_HEAD
    tri = jnp.where(lax.broadcasted_iota(jnp.int32, (n, n), 0)
                    >= lax.broadcasted_iota(jnp.int32, (n, n), 1), 1.0, 0.0).astype(BF16)

    def stack(t):
        z = jnp.zeros_like(t)
        return jnp.concatenate([jnp.where(head0, t, z), jnp.where(head0, z, t)], axis=0).astype(BF16)

    def operands(sl, q, cw):
        lanes = slice(q * LANES, (q + 1) * LANES)
        r = r_ref[0, sl, lanes]
        k = k_ref[0, sl, lanes]
        kk = kk_ref[0, sl, lanes]
        lw = lw_ref[0, sl, lanes]
        cw_end = cw[n - 1:n, :]
        p_inv = jnp.exp(-cw)
        p_to_end = jnp.exp(cw_end - cw)
        kka = kk * a_ref[0, sl, lanes]
        xar = jnp.concatenate([stack(-kk * jnp.exp(cw - lw)), stack(r * jnp.exp(cw))], axis=0)
        ybk = jnp.concatenate([stack(kka * p_inv), stack(k * p_inv)], axis=0)
        return dict(xar=xar, ybk=ybk, bt=stack(kka * p_to_end), kt=stack(k * p_to_end),
                    vs=stack(v_ref[0, sl, lanes]), decay=jnp.exp(cw_end))

    def chunk(c, carry):
        sl = pl.ds(pl.multiple_of(c * n, n), n)
        lw = lw_ref[0, sl, :]
        hi = lw.astype(BF16)
        r1 = lw - hi.astype(F32)
        mid = r1.astype(BF16)
        lo = (r1 - mid.astype(F32)).astype(BF16)
        cw = _dot(tri, hi) + _dot(tri, mid) + _dot(tri, lo)
        pairs = range(npair)
        ops = [operands(sl, q, cw[:, q * LANES:(q + 1) * LANES]) for q in pairs]
        strict = mask_ref[_MASK_STRICT] > 0.5
        incl = mask_ref[_MASK_INCL] > 0.5
        sc = [_dot_nt(o["xar"], o["ybk"]) for o in ops]
        a_ab = [jnp.where(strict, s[0:2 * n, 0:2 * n], 0.0) for s in sc]
        a_ak = [jnp.where(strict, s[0:2 * n, 2 * n:4 * n], 0.0).astype(BF16) for s in sc]
        a_rb = [jnp.where(incl, s[2 * n:4 * n, 0:2 * n], 0.0).astype(BF16) for s in sc]
        a_rk = [jnp.where(incl, s[2 * n:4 * n, 2 * n:4 * n], 0.0).astype(BF16) for s in sc]
        m = [mask_ref[_MASK_EYE] + a * mask_ref[0] for a in a_ab]
        for lev in range(1, WKV_LEVELS):
            mb = [x.astype(BF16) for x in m]
            em = [_dot((a * mask_ref[lev]).astype(BF16), x).astype(BF16) for a, x in zip(a_ab, mb)]
            m = [x + _dot(xb, y) for x, xb, y in zip(m, mb, em)]
        t = [t_ref[q] for q in pairs]
        xt = [_dot(o["xar"], x.astype(BF16)) for o, x in zip(ops, t)]
        rhs = [x[0:2 * n] + _dot(a, o["vs"]) for x, a, o in zip(xt, a_ak, ops)]
        u = [_dot(x.astype(BF16), y.astype(BF16)).astype(BF16) for x, y in zip(m, rhs)]
        for q in pairs:
            o = ops[q]
            ys = xt[q][2 * n:4 * n] + _dot(a_rb[q], u[q]) + _dot(a_rk[q], o["vs"])
            y_ref[0, sl, q * LANES:(q + 1) * LANES] = ys[0:n] + ys[n:2 * n]
            t_ref[q] = t[q] * jnp.transpose(jnp.broadcast_to(o["decay"], (LANES, LANES))) \
                + _dot_tn(o["bt"], u[q]) + _dot_tn(o["kt"], o["vs"])
        return carry

    lax.fori_loop(0, ts // n, chunk, 0)


def _wkv(r, k, v, kk, a, lw, *, ts, npair):
    b, s, d = r.shape
    wd = npair * LANES
    spec = pl.BlockSpec((1, ts, wd), lambda bi, p, i: (bi, i, p))
    return pl.pallas_call(
        _wkv_kernel,
        grid=(b, d // wd, s // ts),
        in_specs=[spec] * 6,
        out_specs=spec,
        out_shape=jax.ShapeDtypeStruct((b, s, d), F32),
        scratch_shapes=[pltpu.VMEM((npair, LANES, LANES), F32),
                        pltpu.VMEM((WKV_LEVELS + 3, LANES, LANES), F32)],
        compiler_params=_cparams(("parallel", "parallel", "arbitrary")),
        name="wkv7",
    )(r, k, v, kk, a, lw)


def _rwkv_post_kernel(y_ref, bonus_ref, g_ref, lnw_ref, lnb_ref, o_ref):
    y = y_ref[...]
    inv_n = 1.0 / RWKV_HEAD
    mean = _head_sum(y) * inv_n
    yc = y - mean
    var = _head_sum(yc * yc) * inv_n
    yn = yc * lax.rsqrt(var + GN_EPS) * lnw_ref[...] + lnb_ref[...]
    o_ref[...] = ((yn + bonus_ref[...]) * g_ref[...]).astype(o_ref.dtype)


def _rwkv_post(y, bonus, g, ln_w, ln_b, *, tm):
    m, d = y.shape
    blk = pl.BlockSpec((tm, d), lambda i: (i, 0))
    row = pl.BlockSpec((1, d), lambda i: (0, 0))
    return pl.pallas_call(
        _rwkv_post_kernel,
        grid=(m // tm,),
        in_specs=[blk, blk, blk, row, row],
        out_specs=blk,
        out_shape=jax.ShapeDtypeStruct((m, d), BF16),
        compiler_params=_cparams(("parallel",)),
        name="rwkv_post",
    )(y, bonus, g, ln_w, ln_b)


def _pool_kernel(u_ref, up_ref, w_ref, sc_ref, o_ref, buf_ref):
    ts = o_ref.shape[1]
    hist = max(POOL_WINDOWS)
    i = pl.program_id(1)
    u = u_ref[0]
    _fill_history(buf_ref, up_ref[0], u, hist, i == 0)
    pos = i * ts + lax.broadcasted_iota(jnp.int32, (ts, 1), 0)
    outs = []
    for gi, win in enumerate(POOL_WINDOWS):
        cols = pl.ds(gi * POOL_GROUP, POOL_GROUP)
        acc = buf_ref[pl.ds(hist, ts), cols]
        for j in range(1, win):
            acc = acc + buf_ref[pl.ds(hist - j, ts), cols]
        count = jnp.minimum(pos + 1, win).astype(F32)
        dg = acc / count - u[:, gi * POOL_GROUP:(gi + 1) * POOL_GROUP]
        outs.append(_dot(dg.astype(BF16), w_ref[gi]))
    y = jnp.concatenate(outs, axis=1) * sc_ref[...]
    o_ref[0] = y.astype(o_ref.dtype)


def _pool_mixer(p, pool_w, pool_scale, *, ts):
    b, s, _ = p.shape
    hist = max(POOL_WINDOWS)
    rb = ts // hist
    return pl.pallas_call(
        _pool_kernel,
        grid=(b, s // ts),
        in_specs=[
            pl.BlockSpec((1, ts, POOL_DIM), lambda bi, i: (bi, i, 0)),
            pl.BlockSpec((1, hist, POOL_DIM), lambda bi, i: (bi, jnp.maximum(i * rb - 1, 0), 0)),
            pl.BlockSpec((len(POOL_WINDOWS), POOL_GROUP, POOL_GROUP), lambda bi, i: (0, 0, 0)),
            pl.BlockSpec((1, POOL_DIM), lambda bi, i: (0, 0)),
        ],
        out_specs=pl.BlockSpec((1, ts, POOL_DIM), lambda bi, i: (bi, i, 0)),
        out_shape=jax.ShapeDtypeStruct((b, s, POOL_DIM), BF16),
        scratch_shapes=[pltpu.VMEM((ts + hist, POOL_DIM), F32)],
        compiler_params=_cparams(("parallel", "parallel")),
        name="pool_mixer",
    )(p, p, pool_w, pool_scale)


def _mla_prep_kernel(ql_ref, kvl_ref, kpe_ref, kper_ref, ang_ref, qn_ref, kvn_ref,
                     wq_ref, wqr_ref, wk_ref, wv_ref, q_out, k_out, v_out):
    cos = jnp.cos(ang_ref[...])
    sin = jnp.sin(ang_ref[...])
    qn = _rms(ql_ref[...], qn_ref[...]).astype(BF16)
    kvn = _rms(kvl_ref[...], kvn_ref[...]).astype(BF16)
    scale = (QK_NOPE + QK_ROPE) ** -0.5 * LOG2_E
    qa = _dot(qn, wq_ref[...])
    qr = _dot(qn, wqr_ref[...])
    ka = _dot(kvn, wk_ref[...])
    kpe = kpe_ref[...] * cos + kper_ref[...] * sin
    for h in range(MLA_HEADS):
        lo = h * QK_PAD
        q_out[:, lo:lo + LANES] = (qa[:, lo:lo + LANES] * scale).astype(BF16)
        q_pe = qa[:, lo + LANES:lo + 2 * LANES] * cos + qr[:, h * LANES:(h + 1) * LANES] * sin
        q_out[:, lo + LANES:lo + 2 * LANES] = (q_pe * scale).astype(BF16)
        k_out[:, lo:lo + LANES] = ka[:, lo:lo + LANES].astype(BF16)
        k_out[:, lo + LANES:lo + 2 * LANES] = kpe.astype(BF16)
    v_out[...] = _dot_nt(wv_ref[...], kvn).astype(BF16)


def _mla_prep(p, ang, q_norm, kv_norm, wq, wqr, wk, wv, *, tm):
    m = p.shape[0]
    hq = MLA_HEADS * QK_PAD
    hv = MLA_HEADS * V_HEAD
    full = lambda a: pl.BlockSpec(a.shape, lambda i: (0, 0))
    return pl.pallas_call(
        _mla_prep_kernel,
        grid=(m // tm,),
        in_specs=[
            pl.BlockSpec((tm, Q_LORA), lambda i: (i, 1)),
            pl.BlockSpec((tm, KV_LORA), lambda i: (i, 2)),
            pl.BlockSpec((tm, LANES), lambda i: (i, 12)),
            pl.BlockSpec((tm, LANES), lambda i: (i, 13)),
            pl.BlockSpec((tm, LANES), lambda i: (i, 0)),
            full(q_norm), full(kv_norm), full(wq), full(wqr), full(wk), full(wv),
        ],
        out_specs=[pl.BlockSpec((tm, hq), lambda i: (i, 0)),
                   pl.BlockSpec((tm, hq), lambda i: (i, 0)),
                   pl.BlockSpec((hv, tm), lambda i: (0, i))],
        out_shape=[jax.ShapeDtypeStruct((m, hq), BF16),
                   jax.ShapeDtypeStruct((m, hq), BF16),
                   jax.ShapeDtypeStruct((hv, m), BF16)],
        compiler_params=_cparams(("parallel",)),
        name="mla_prep",
    )(p, p, p, p, ang, q_norm, kv_norm, wq, wqr, wk, wv)


def _attn_kernel(q_ref, k_ref, vt_ref, o_ref, *, tq, tk, nh):
    qi = pl.program_id(2)
    heads = range(nh)
    qs = [q_ref[0, :, h * QK_PAD:(h + 1) * QK_PAD] for h in heads]
    diag = tq // tk

    def block(j, carry, masked):
        m, l, acc = carry
        sl = pl.ds(pl.multiple_of(j * tk, tk), tk)
        s = [_dot_nt(k_ref[0, sl, h * QK_PAD:(h + 1) * QK_PAD], qs[h]) for h in heads]
        if masked:
            key = j * tk + lax.broadcasted_iota(jnp.int32, (tk, tq), 0)
            qry = qi * tq + lax.broadcasted_iota(jnp.int32, (tk, tq), 1)
            s = [jnp.where(key <= qry, x, -1e30) for x in s]
        m_new = [jnp.maximum(a, jnp.max(x, axis=0, keepdims=True)) for a, x in zip(m, s)]
        alpha = [jnp.exp2(a - b) for a, b in zip(m, m_new)]
        p = [jnp.exp2(x - a) for x, a in zip(s, m_new)]
        l = [a * b + jnp.sum(x, axis=0, keepdims=True) for a, b, x in zip(alpha, l, p)]
        pv = [_dot(vt_ref[h * V_HEAD:(h + 1) * V_HEAD, sl], p[h].astype(BF16)) for h in heads]
        acc = [a * b + x for a, b, x in zip(alpha, acc, pv)]
        return m_new, l, acc

    init = ([jnp.full((1, tq), -1e30, F32) for _ in heads], [jnp.zeros((1, tq), F32) for _ in heads],
            [jnp.zeros((V_HEAD, tq), F32) for _ in heads])
    carry = lax.fori_loop(0, qi * diag, lambda j, c: block(j, c, False), init)
    for d in range(diag):
        carry = block(qi * diag + d, carry, True)
    _, l, acc = carry
    for h in heads:
        o_ref[0, :, h * V_HEAD:(h + 1) * V_HEAD] = jnp.transpose(acc[h] / l[h]).astype(o_ref.dtype)


def _attention(q, k, vt, *, tq, tk, nh):
    b, s, _ = q.shape
    return pl.pallas_call(
        functools.partial(_attn_kernel, tq=tq, tk=tk, nh=nh),
        grid=(b, MLA_HEADS // nh, s // tq),
        in_specs=[
            pl.BlockSpec((1, tq, nh * QK_PAD), lambda bi, h, i: (bi, i, h)),
            pl.BlockSpec((1, s, nh * QK_PAD), lambda bi, h, i: (bi, 0, h)),
            pl.BlockSpec((nh * V_HEAD, s), lambda bi, h, i: (h, bi)),
        ],
        out_specs=pl.BlockSpec((1, tq, nh * V_HEAD), lambda bi, h, i: (bi, i, h)),
        out_shape=jax.ShapeDtypeStruct((b, s, MLA_HEADS * V_HEAD), BF16),
        compiler_params=_cparams(("parallel", "parallel", "arbitrary")),
        name="mla_attention",
    )(q, k, vt)


def _pad_cols(w, n):
    return jnp.pad(w, ((0, 0), (0, n - w.shape[1])))


def _pad_rows(w, n):
    return jnp.pad(w, ((0, n - w.shape[0]), (0, 0)))


def _rot_cols(w):
    half = w.shape[-1] // 2
    return jnp.concatenate([-w[..., half:], w[..., :half]], axis=-1)


def _even_weights(w_in, mu):
    c = 3 * CONV_DIM + 3 * RWKV_DIM
    xw = w_in[:, c:c + W_LORA]
    xa = w_in[:, c + W_LORA:c + W_LORA + A_LORA]
    xg = w_in[:, c + W_LORA + A_LORA:]
    w = jnp.concatenate([w_in[:, :c], _pad_cols(xw, LANES), _pad_cols(xa, LANES),
                         _pad_cols(xg, 2 * LANES)], axis=1).astype(BF16)
    r = 3 * RWKV_DIM
    mu_l = jnp.concatenate([
        jnp.pad(mu[r:r + W_LORA], (0, LANES - W_LORA)),
        jnp.pad(mu[r + W_LORA:r + W_LORA + A_LORA], (0, LANES - A_LORA)),
        jnp.pad(mu[r + W_LORA + A_LORA:], (0, 2 * LANES - G_LORA))])
    return w, mu[:r][None, :], mu_l[None, :]


def _odd_weights(w_in, w_uq, w_ukv):
    o3 = POOL_DIM + Q_LORA + KV_LORA
    kpe = w_in[:, o3:]
    w = jnp.concatenate([w_in[:, :o3], _pad_cols(kpe, LANES), _pad_cols(_rot_cols(kpe), LANES)],
                        axis=1).astype(BF16)
    uq = w_uq.reshape(Q_LORA, MLA_HEADS, QK_NOPE + QK_ROPE)
    zq = jnp.zeros((Q_LORA, MLA_HEADS, QK_PAD - QK_NOPE - QK_ROPE), F32)
    wq = jnp.concatenate([uq, zq], axis=-1).reshape(Q_LORA, MLA_HEADS * QK_PAD).astype(BF16)
    zr = jnp.zeros((Q_LORA, MLA_HEADS, LANES - QK_ROPE), F32)
    wqr = jnp.concatenate([_rot_cols(uq[..., QK_NOPE:]), zr], axis=-1)
    wqr = wqr.reshape(Q_LORA, MLA_HEADS * LANES).astype(BF16)
    ukv = w_ukv.reshape(KV_LORA, MLA_HEADS, QK_NOPE + V_HEAD)
    zk = jnp.zeros((KV_LORA, MLA_HEADS, QK_PAD - QK_NOPE), F32)
    wk = jnp.concatenate([ukv[..., :QK_NOPE], zk], axis=-1).reshape(KV_LORA, MLA_HEADS * QK_PAD)
    wv = ukv[..., QK_NOPE:].reshape(KV_LORA, MLA_HEADS * V_HEAD)
    return w, wq, wqr, wk.astype(BF16), wv.T.astype(BF16)


def _tile(n, pref):
    return pref if n % pref == 0 else n


def kernel(x, positions, ev_norm, ev_w_in, ev_conv_w, ev_mu, ev_w0, ev_w2, ev_a0, ev_a2, ev_g2, ev_k_k, ev_k_a, ev_r_k, ev_ln_w, ev_ln_b, ev_w_out, od_norm, od_w_in, od_pool_w, od_pool_scale, od_q_norm, od_w_uq, od_kv_norm, od_w_ukv, od_w_out, ffn_norm, ffn_w_gate, ffn_w_up, ffn_w_down, final_norm):
    b, s, d = x.shape
    m = b * s
    depth = ffn_norm.shape[0]
    tm = _tile(m, 512)
    tm_in = _tile(m, 1024)
    ts = _tile(s, 512)
    ts_prep = _tile(s, 256)

    inv = 1.0 / (ROPE_THETA ** (jnp.arange(0, QK_ROPE, 2, dtype=F32) / QK_ROPE))
    inv = jnp.concatenate([inv, inv, jnp.zeros((LANES - QK_ROPE,), F32)])
    ang = (positions.astype(F32)[..., None] * inv).reshape(m, LANES)

    h = x.reshape(m, d)
    for layer in range(depth):
        j = layer // 2
        if layer % 2 == 0:
            w_in, mu, mu_l = _even_weights(ev_w_in[j], ev_mu[j])
            p = _norm_matmul(h, ev_norm[j][None, :], w_in, tm=tm_in, tn=512, out_dtype=F32)
            p = p.reshape(b, s, EV_COLS)
            ya = _conv_mixer(p, ev_conv_w[j].T, ts=ts, tc=512)
            row = lambda t: t[None, :]
            r, k, v, kk, a, lw, g, bonus = _rwkv_prep(
                p, mu, mu_l, row(ev_w0[j]), _pad_rows(ev_w2[j], LANES).astype(BF16), row(ev_a0[j]),
                _pad_rows(ev_a2[j], LANES).astype(BF16), _pad_rows(ev_g2[j], 2 * LANES).astype(BF16),
                row(ev_k_k[j]), row(ev_k_a[j]), row(ev_r_k[j].reshape(-1)), ts=ts_prep)
            y = _wkv(r, k, v, kk, a, lw, ts=ts, npair=8)
            yb = _rwkv_post(y.reshape(m, -1), bonus.reshape(m, -1), g.reshape(m, -1),
                            row(ev_ln_w[j]), row(ev_ln_b[j]), tm=tm)
            w_out = ev_w_out[j].astype(BF16)
            h = _proj_residual(ya.reshape(m, -1), yb, w_out[:CONV_DIM], w_out[CONV_DIM:], h,
                               tm=tm, tn=d)
        else:
            w_in, wq, wqr, wk, wv = _odd_weights(od_w_in[j], od_w_uq[j], od_w_ukv[j])
            p = _norm_matmul(h, od_norm[j][None, :], w_in, tm=tm_in, tn=256, out_dtype=F32)
            yc = _pool_mixer(p.reshape(b, s, OD_COLS), od_pool_w[j].astype(BF16),
                             od_pool_scale[j][None, :], ts=ts)
            q, kq, vv = _mla_prep(p, ang, od_q_norm[j][None, :], od_kv_norm[j][None, :],
                                  wq, wqr, wk, wv, tm=_tile(m, 256))
            hq = MLA_HEADS * QK_PAD
            yd = _attention(q.reshape(b, s, hq), kq.reshape(b, s, hq), vv, tq=_tile(s, 512),
                            tk=_tile(s, 512), nh=4)
            w_out = od_w_out[j].astype(BF16)
            h = _proj_residual(yc.reshape(m, -1), yd.reshape(m, -1), w_out[:POOL_DIM],
                               w_out[POOL_DIM:], h, tm=tm, tn=d)
        h = _ffn(h, ffn_norm[layer][None, :], ffn_w_gate[layer].astype(BF16),
                 ffn_w_up[layer].astype(BF16), ffn_w_down[layer].astype(BF16), tm=tm, tf=512)
    return _final_norm(h, final_norm[None, :], tm=tm).reshape(b, s, d)
```

```python
import functools

import jax
import jax.numpy as jnp
from jax import lax
from jax.experimental import pallas as pl
from jax.experimental.pallas import tpu as pltpu

F32 = jnp.float32
BF16 = jnp.bfloat16

D_MODEL = 2048
NORM_EPS = 1e-6
CONV_DIM = 1024
RWKV_DIM = 1024
RWKV_HEAD = 64
W_LORA = 64
A_LORA = 64
G_LORA = 160
GN_EPS = 64e-5
POOL_WINDOWS = (2, 4, 8, 16)
POOL_GROUP = 128
POOL_DIM = 512
MLA_HEADS = 12
Q_LORA = 512
KV_LORA = 512
QK_NOPE = 128
QK_ROPE = 64
V_HEAD = 128
ROPE_THETA = 10000.0
D_FF = 5632
LOG2_E = 1.4426950408889634

LANES = 128
HIST = 16
VMEM_LIMIT = 56 * 1024 * 1024
QK_PAD = 256
ATTN_SUM_ROWS = 16
VT_ROWS = V_HEAD + ATTN_SUM_ROWS
WKV_CHUNK = 64
EV_COLS = 6656
EV_LORA = 512
OD_COLS = 1792


def _cparams(sem):
    return pltpu.CompilerParams(dimension_semantics=sem, vmem_limit_bytes=VMEM_LIMIT)


def _rms(x, g):
    ms = jnp.mean(x * x, axis=-1, keepdims=True)
    return x * lax.rsqrt(ms + NORM_EPS) * g


def _sigmoid(x):
    return 1.0 / (1.0 + jnp.exp(-x))


def _dot(a, b):
    return jnp.dot(a, b, preferred_element_type=F32)


def _dot_nt(a, b):
    return lax.dot_general(a, b, (((1,), (1,)), ((), ())), preferred_element_type=F32)


def _dot_tn(a, b):
    return lax.dot_general(a, b, (((0,), (0,)), ((), ())), preferred_element_type=F32)


def _norm_mm_kernel(x_ref, g_ref, w_ref, o_ref, xn_ref):
    @pl.when(pl.program_id(1) == 0)
    def _():
        xn_ref[...] = _rms(x_ref[...], g_ref[...]).astype(BF16)

    o_ref[...] = _dot(xn_ref[...], w_ref[...]).astype(o_ref.dtype)


def _norm_matmul(x, g, w, layer, *, tm, tn, out_dtype):
    m, k = x.shape
    n = w.shape[2]
    return pl.pallas_call(
        _norm_mm_kernel,
        grid=(m // tm, n // tn),
        in_specs=[
            pl.BlockSpec((tm, k), lambda i, j: (i, 0)),
            pl.BlockSpec((1, k), lambda i, j: (0, 0)),
            pl.BlockSpec((None, k, tn), lambda i, j: (layer, 0, j)),
        ],
        out_specs=pl.BlockSpec((tm, tn), lambda i, j: (i, j)),
        out_shape=jax.ShapeDtypeStruct((m, n), out_dtype),
        scratch_shapes=[pltpu.VMEM((tm, k), BF16)],
        compiler_params=_cparams(("parallel", "arbitrary")),
        name="norm_matmul",
    )(x, g, w)


def _proj_res_kernel(a_ref, b_ref, w_ref, h_ref, o_ref):
    ka = a_ref.shape[1]
    o_ref[...] = h_ref[...] + _dot(a_ref[...], w_ref[0:ka, :]) + _dot(b_ref[...], w_ref[ka:, :])


def _proj_residual(a, b, w, layer, h, *, tm, tn):
    m, n = h.shape
    ka, kb = a.shape[1], b.shape[1]
    return pl.pallas_call(
        _proj_res_kernel,
        grid=(m // tm, n // tn),
        in_specs=[
            pl.BlockSpec((tm, ka), lambda i, j: (i, 0)),
            pl.BlockSpec((tm, kb), lambda i, j: (i, 0)),
            pl.BlockSpec((None, ka + kb, tn), lambda i, j: (layer, 0, j)),
            pl.BlockSpec((tm, tn), lambda i, j: (i, j)),
        ],
        out_specs=pl.BlockSpec((tm, tn), lambda i, j: (i, j)),
        out_shape=jax.ShapeDtypeStruct((m, n), F32),
        compiler_params=_cparams(("parallel", "arbitrary")),
        name="proj_residual",
    )(a, b, w, h)


def _ffn_kernel(x_ref, g_ref, wg_ref, wu_ref, wd_ref, o_ref, xn_ref):
    @pl.when(pl.program_id(1) == 0)
    def _():
        x = x_ref[...]
        xn_ref[...] = _rms(x, g_ref[...]).astype(BF16)
        o_ref[...] = x

    xn = xn_ref[...]
    a = _dot(xn, wg_ref[...])
    b = _dot(xn, wu_ref[...])
    act = (a * _sigmoid(a) * b).astype(BF16)
    o_ref[...] += _dot(act, wd_ref[...])


def _ffn(x, g, wg, wu, wd, layer, *, tm, tf):
    m, d = x.shape
    f = wg.shape[2]
    return pl.pallas_call(
        _ffn_kernel,
        grid=(m // tm, f // tf),
        in_specs=[
            pl.BlockSpec((tm, d), lambda i, j: (i, 0)),
            pl.BlockSpec((1, d), lambda i, j: (0, 0)),
            pl.BlockSpec((None, d, tf), lambda i, j: (layer, 0, j)),
            pl.BlockSpec((None, d, tf), lambda i, j: (layer, 0, j)),
            pl.BlockSpec((None, tf, d), lambda i, j: (layer, j, 0)),
        ],
        out_specs=pl.BlockSpec((tm, d), lambda i, j: (i, 0)),
        out_shape=jax.ShapeDtypeStruct((m, d), F32),
        scratch_shapes=[pltpu.VMEM((tm, d), BF16)],
        compiler_params=_cparams(("parallel", "arbitrary")),
        name="ffn",
    )(x, g, wg, wu, wd)


def _final_norm_kernel(x_ref, g_ref, o_ref):
    o_ref[...] = _rms(x_ref[...], g_ref[...])


def _final_norm(x, g, *, tm):
    m, d = x.shape
    return pl.pallas_call(
        _final_norm_kernel,
        grid=(m // tm,),
        in_specs=[pl.BlockSpec((tm, d), lambda i: (i, 0)), pl.BlockSpec((1, d), lambda i: (0, 0))],
        out_specs=pl.BlockSpec((tm, d), lambda i: (i, 0)),
        out_shape=jax.ShapeDtypeStruct((m, d), F32),
        compiler_params=_cparams(("parallel",)),
        name="final_norm",
    )(x, g)


def _fill_history(buf_ref, prev, cur, hist, is_first):
    buf_ref[pl.ds(0, hist), :] = jnp.where(is_first, 0.0, prev)
    buf_ref[pl.ds(hist, cur.shape[0]), :] = cur


def _conv_kernel(bg_ref, cg_ref, hh_ref, cgp_ref, hhp_ref, w_ref, o_ref, buf_ref):
    ts = o_ref.shape[1]
    u = cg_ref[0].astype(F32) * hh_ref[0].astype(F32)
    up = cgp_ref[0].astype(F32) * hhp_ref[0].astype(F32)
    _fill_history(buf_ref, up, u, HIST, pl.program_id(1) == 0)
    u1 = buf_ref[pl.ds(HIST - 1, ts), :]
    u2 = buf_ref[pl.ds(HIST - 2, ts), :]
    w = w_ref[...]
    y = w[2:3] * u + w[1:2] * u1 + w[0:1] * u2
    o_ref[0] = (bg_ref[0].astype(F32) * y).astype(o_ref.dtype)


def _conv_mixer(p, conv_w_t, *, ts, tc):
    b, s, _ = p.shape
    nc = CONV_DIM // tc
    rb = ts // HIST

    def cur(off):
        return pl.BlockSpec((1, ts, tc), lambda bi, i, c: (bi, i, off * nc + c))

    def prev(off):
        return pl.BlockSpec((1, HIST, tc),
                            lambda bi, i, c: (bi, jnp.maximum(i * rb - 1, 0), off * nc + c))

    return pl.pallas_call(
        _conv_kernel,
        grid=(b, s // ts, nc),
        in_specs=[cur(0), cur(1), cur(2), prev(1), prev(2),
                  pl.BlockSpec((3, tc), lambda bi, i, c: (0, c))],
        out_specs=pl.BlockSpec((1, ts, tc), lambda bi, i, c: (bi, i, c)),
        out_shape=jax.ShapeDtypeStruct((b, s, CONV_DIM), BF16),
        scratch_shapes=[pltpu.VMEM((ts + HIST, tc), F32)],
        compiler_params=_cparams(("parallel", "parallel", "parallel")),
        name="conv_mixer",
    )(p, p, p, p, p, conv_w_t)


def _head_sum(x):
    r = lax.broadcasted_iota(jnp.int32, (LANES, LANES), 0) // RWKV_HEAD
    c = lax.broadcasted_iota(jnp.int32, (LANES, LANES), 1) // RWKV_HEAD
    ones = jnp.where(r == c, 1.0, 0.0).astype(BF16)
    hi = x.astype(BF16)
    lo = (x - hi.astype(F32)).astype(BF16)
    parts = []
    for g in range(x.shape[1] // LANES):
        sl = slice(g * LANES, (g + 1) * LANES)
        parts.append(_dot(hi[:, sl], ones) + _dot(lo[:, sl], ones))
    return jnp.concatenate(parts, axis=1)


def _rwkv_prep_kernel(x_ref, xl_ref, xp_ref, xlp_ref, mu_ref, mul_ref, w0_ref, w2_ref, a0_ref,
                      a2_ref, g2_ref, kk_ref, ka_ref, rk_ref,
                      r_out, k_out, v_out, kkn_out, a_out, lw_out, g_out, bonus_out,
                      buf_ref, bufl_ref):
    ts = r_out.shape[1]
    first = pl.program_id(1) == 0
    d = RWKV_DIM

    x = x_ref[0].astype(F32)
    _fill_history(buf_ref, xp_ref[0].astype(F32), x, HIST, first)
    xs = buf_ref[pl.ds(HIST - 1, ts), :]
    xm = x + (xs - x) * mu_ref[...]
    xl = xl_ref[0].astype(F32)
    _fill_history(bufl_ref, xlp_ref[0].astype(F32), xl, HIST, first)
    xls = bufl_ref[pl.ds(HIST - 1, ts), :]
    xlm = xl + (xls - xl) * mul_ref[...]

    r = xm[:, 0:d]
    k = xm[:, d:2 * d]
    v = xm[:, 2 * d:3 * d]
    xw = xlm[:, 0:LANES]
    xa = xlm[:, LANES:2 * LANES]
    xg = xlm[:, 2 * LANES:4 * LANES]

    z = -(w0_ref[...] + _dot(jnp.tanh(xw).astype(BF16), w2_ref[...]))
    softplus = jnp.maximum(z, 0.0) + jnp.log(1.0 + jnp.exp(-jnp.abs(z)))
    w = -softplus - 0.5
    a = _sigmoid(a0_ref[...] + _dot(xa.astype(BF16), a2_ref[...]))
    g = _dot(_sigmoid(xg).astype(BF16), g2_ref[...])

    kk = k * kk_ref[...]
    nrm = jnp.sqrt(_head_sum(kk * kk))
    kk = kk / jnp.maximum(nrm, 1e-12)
    k = k * (1.0 + (a - 1.0) * ka_ref[...])
    bonus = _head_sum(r * k * rk_ref[...]) * v

    r_out[0] = r
    k_out[0] = k
    v_out[0] = v
    kkn_out[0] = kk
    a_out[0] = a
    lw_out[0] = -jnp.exp(w)
    g_out[0] = g
    bonus_out[0] = bonus


def _rwkv_prep(p, mu, mul, w0, w2, a0, a2, g2, k_k, k_a, r_k, *, ts):
    b, s, _ = p.shape
    d = RWKV_DIM
    rb = ts // HIST
    row = lambda shape: pl.BlockSpec(shape, lambda bi, i: (0, 0))
    out = pl.BlockSpec((1, ts, d), lambda bi, i: (bi, i, 0))
    return pl.pallas_call(
        _rwkv_prep_kernel,
        grid=(b, s // ts),
        in_specs=[
            pl.BlockSpec((1, ts, 3 * d), lambda bi, i: (bi, i, 1)),
            pl.BlockSpec((1, ts, EV_LORA), lambda bi, i: (bi, i, 6 * d // EV_LORA)),
            pl.BlockSpec((1, HIST, 3 * d), lambda bi, i: (bi, jnp.maximum(i * rb - 1, 0), 1)),
            pl.BlockSpec((1, HIST, EV_LORA),
                         lambda bi, i: (bi, jnp.maximum(i * rb - 1, 0), 6 * d // EV_LORA)),
            row((1, 3 * d)), row((1, EV_LORA)), row((1, d)), row((LANES, d)), row((1, d)),
            row((LANES, d)), row((2 * LANES, d)), row((1, d)), row((1, d)), row((1, d)),
        ],
        out_specs=[out] * 8,
        out_shape=[jax.ShapeDtypeStruct((b, s, d), F32)] * 8,
        scratch_shapes=[pltpu.VMEM((ts + HIST, 3 * d), F32),
                        pltpu.VMEM((ts + HIST, EV_LORA), F32)],
        compiler_params=_cparams(("parallel", "parallel")),
        name="rwkv_prep",
    )(p, p, p, p, mu, mul, w0, w2, a0, a2, g2, k_k, k_a, r_k)


WKV_LEVELS = 6
_MASK_STRICT = WKV_LEVELS
_MASK_INCL = WKV_LEVELS + 1
_MASK_EYE = WKV_LEVELS + 2


def _wkv_kernel(r_ref, k_ref, v_ref, kk_ref, a_ref, lw_ref, y_ref, t_ref, mask_ref):
    n = WKV_CHUNK
    ts = y_ref.shape[1]
    npair = y_ref.shape[2] // LANES

    @pl.when(pl.program_id(2) == 0)
    def _():
        t_ref[...] = jnp.zeros_like(t_ref)
        row = lax.broadcasted_iota(jnp.int32, (2 * n, 2 * n), 0)
        col = lax.broadcasted_iota(jnp.int32, (2 * n, 2 * n), 1)
        same = (row // n) == (col // n)
        x = row ^ col
        level = jnp.zeros_like(x)
        for bit in range(1, WKV_LEVELS):
            level = level + jnp.where(x >= (1 << bit), 1, 0)
        level = jnp.where(same & (row > col), level, -1)
        for lev in range(WKV_LEVELS):
            mask_ref[lev] = jnp.where(level == lev, 1.0, 0.0)
        mask_ref[_MASK_STRICT] = jnp.where(level >= 0, 1.0, 0.0)
        mask_ref[_MASK_INCL] = jnp.where(same & (row >= col), 1.0, 0.0)
        mask_ref[_MASK_EYE] = jnp.where(row == col, 1.0, 0.0)

    head0 = lax.broadcasted_iota(jnp.int32, (n, LANES), 1) < RWKV_HEAD
    tri = jnp.where(lax.broadcasted_iota(jnp.int32, (n, n), 0)
                    >= lax.broadcasted_iota(jnp.int32, (n, n), 1), 1.0, 0.0).astype(BF16)

    def stack(t):
        z = jnp.zeros_like(t)
        return jnp.concatenate([jnp.where(head0, t, z), jnp.where(head0, z, t)], axis=0).astype(BF16)

    def operands(sl, q, cw):
        lanes = slice(q * LANES, (q + 1) * LANES)
        r = r_ref[0, sl, lanes]
        k = k_ref[0, sl, lanes]
        kk = kk_ref[0, sl, lanes]
        lw = lw_ref[0, sl, lanes]
        cw_end = cw[n - 1:n, :]
        p_inv = jnp.exp(-cw)
        p_to_end = jnp.exp(cw_end - cw)
        kka = kk * a_ref[0, sl, lanes]
        xar = jnp.concatenate([stack(-kk * jnp.exp(cw - lw)), stack(r * jnp.exp(cw))], axis=0)
        ybk = jnp.concatenate([stack(kka * p_inv), stack(k * p_inv)], axis=0)
        return dict(xar=xar, ybk=ybk, bt=stack(kka * p_to_end), kt=stack(k * p_to_end),
                    vs=stack(v_ref[0, sl, lanes]), decay=jnp.exp(cw_end))

    def chunk(c, carry):
        sl = pl.ds(pl.multiple_of(c * n, n), n)
        lw = lw_ref[0, sl, :]
        hi = lw.astype(BF16)
        r1 = lw - hi.astype(F32)
        mid = r1.astype(BF16)
        lo = (r1 - mid.astype(F32)).astype(BF16)
        cw = _dot(tri, hi) + _dot(tri, mid) + _dot(tri, lo)
        pairs = range(npair)
        ops = [operands(sl, q, cw[:, q * LANES:(q + 1) * LANES]) for q in pairs]
        strict = mask_ref[_MASK_STRICT] > 0.5
        incl = mask_ref[_MASK_INCL] > 0.5
        sc = [_dot_nt(o["xar"], o["ybk"]) for o in ops]
        a_ab = [jnp.where(strict, s[0:2 * n, 0:2 * n], 0.0) for s in sc]
        a_ak = [jnp.where(strict, s[0:2 * n, 2 * n:4 * n], 0.0).astype(BF16) for s in sc]
        a_rb = [jnp.where(incl, s[2 * n:4 * n, 0:2 * n], 0.0).astype(BF16) for s in sc]
        a_rk = [jnp.where(incl, s[2 * n:4 * n, 2 * n:4 * n], 0.0).astype(BF16) for s in sc]
        m = [mask_ref[_MASK_EYE] + a * mask_ref[0] for a in a_ab]
        for lev in range(1, WKV_LEVELS):
            mb = [x.astype(BF16) for x in m]
            em = [_dot((a * mask_ref[lev]).astype(BF16), x).astype(BF16) for a, x in zip(a_ab, mb)]
            m = [x + _dot(xb, y) for x, xb, y in zip(m, mb, em)]
        t = [t_ref[q] for q in pairs]
        xt = [_dot(o["xar"], x.astype(BF16)) for o, x in zip(ops, t)]
        rhs = [x[0:2 * n] + _dot(a, o["vs"]) for x, a, o in zip(xt, a_ak, ops)]
        u = [_dot(x.astype(BF16), y.astype(BF16)).astype(BF16) for x, y in zip(m, rhs)]
        for q in pairs:
            o = ops[q]
            ys = xt[q][2 * n:4 * n] + _dot(a_rb[q], u[q]) + _dot(a_rk[q], o["vs"])
            y_ref[0, sl, q * LANES:(q + 1) * LANES] = ys[0:n] + ys[n:2 * n]
            t_ref[q] = t[q] * jnp.transpose(jnp.broadcast_to(o["decay"], (LANES, LANES))) \
                + _dot_tn(o["bt"], u[q]) + _dot_tn(o["kt"], o["vs"])
        return carry

    lax.fori_loop(0, ts // n, chunk, 0)


def _wkv(r, k, v, kk, a, lw, *, ts, npair):
    b, s, d = r.shape
    wd = npair * LANES
    spec = pl.BlockSpec((1, ts, wd), lambda bi, p, i: (bi, i, p))
    return pl.pallas_call(
        _wkv_kernel,
        grid=(b, d // wd, s // ts),
        in_specs=[spec] * 6,
        out_specs=spec,
        out_shape=jax.ShapeDtypeStruct((b, s, d), F32),
        scratch_shapes=[pltpu.VMEM((npair, LANES, LANES), F32),
                        pltpu.VMEM((WKV_LEVELS + 3, LANES, LANES), F32)],
        compiler_params=_cparams(("parallel", "parallel", "arbitrary")),
        name="wkv7",
    )(r, k, v, kk, a, lw)


def _rwkv_post_kernel(y_ref, bonus_ref, g_ref, lnw_ref, lnb_ref, o_ref):
    y = y_ref[...]
    inv_n = 1.0 / RWKV_HEAD
    mean = _head_sum(y) * inv_n
    yc = y - mean
    var = _head_sum(yc * yc) * inv_n
    yn = yc * lax.rsqrt(var + GN_EPS) * lnw_ref[...] + lnb_ref[...]
    o_ref[...] = ((yn + bonus_ref[...]) * g_ref[...]).astype(o_ref.dtype)


def _rwkv_post(y, bonus, g, ln_w, ln_b, *, tm):
    m, d = y.shape
    blk = pl.BlockSpec((tm, d), lambda i: (i, 0))
    row = pl.BlockSpec((1, d), lambda i: (0, 0))
    return pl.pallas_call(
        _rwkv_post_kernel,
        grid=(m // tm,),
        in_specs=[blk, blk, blk, row, row],
        out_specs=blk,
        out_shape=jax.ShapeDtypeStruct((m, d), BF16),
        compiler_params=_cparams(("parallel",)),
        name="rwkv_post",
    )(y, bonus, g, ln_w, ln_b)


def _pool_kernel(u_ref, up_ref, w_ref, sc_ref, o_ref, buf_ref):
    ts = o_ref.shape[1]
    hist = max(POOL_WINDOWS)
    i = pl.program_id(1)
    u = u_ref[0].astype(F32)
    _fill_history(buf_ref, up_ref[0].astype(F32), u, hist, i == 0)
    pos = i * ts + lax.broadcasted_iota(jnp.int32, (ts, 1), 0)
    outs = []
    for gi, win in enumerate(POOL_WINDOWS):
        cols = pl.ds(gi * POOL_GROUP, POOL_GROUP)
        acc = buf_ref[pl.ds(hist, ts), cols]
        for j in range(1, win):
            acc = acc + buf_ref[pl.ds(hist - j, ts), cols]
        count = jnp.minimum(pos + 1, win).astype(F32)
        dg = acc / count - u[:, gi * POOL_GROUP:(gi + 1) * POOL_GROUP]
        outs.append(_dot(dg.astype(BF16), w_ref[gi]))
    y = jnp.concatenate(outs, axis=1) * sc_ref[...]
    o_ref[0] = y.astype(o_ref.dtype)


def _pool_mixer(p, pool_w, pool_scale, *, ts):
    b, s, _ = p.shape
    hist = max(POOL_WINDOWS)
    rb = ts // hist
    return pl.pallas_call(
        _pool_kernel,
        grid=(b, s // ts),
        in_specs=[
            pl.BlockSpec((1, ts, POOL_DIM), lambda bi, i: (bi, i, 0)),
            pl.BlockSpec((1, hist, POOL_DIM), lambda bi, i: (bi, jnp.maximum(i * rb - 1, 0), 0)),
            pl.BlockSpec((len(POOL_WINDOWS), POOL_GROUP, POOL_GROUP), lambda bi, i: (0, 0, 0)),
            pl.BlockSpec((1, POOL_DIM), lambda bi, i: (0, 0)),
        ],
        out_specs=pl.BlockSpec((1, ts, POOL_DIM), lambda bi, i: (bi, i, 0)),
        out_shape=jax.ShapeDtypeStruct((b, s, POOL_DIM), BF16),
        scratch_shapes=[pltpu.VMEM((ts + hist, POOL_DIM), F32)],
        compiler_params=_cparams(("parallel", "parallel")),
        name="pool_mixer",
    )(p, p, pool_w, pool_scale)


def _mla_prep_kernel(ql_ref, kvl_ref, kpe_ref, kper_ref, ang_ref, qn_ref, kvn_ref,
                     wq_ref, wqr_ref, wk_ref, wv_ref, q_out, k_out, v_out):
    cos = jnp.cos(ang_ref[...])
    sin = jnp.sin(ang_ref[...])
    qn = _rms(ql_ref[...].astype(F32), qn_ref[...]).astype(BF16)
    kvn = _rms(kvl_ref[...].astype(F32), kvn_ref[...]).astype(BF16)
    scale = (QK_NOPE + QK_ROPE) ** -0.5 * LOG2_E
    qa = _dot(qn, wq_ref[...])
    qr = _dot(qn, wqr_ref[...])
    ka = _dot(kvn, wk_ref[...])
    kpe = kpe_ref[...].astype(F32) * cos + kper_ref[...].astype(F32) * sin
    for h in range(MLA_HEADS):
        lo = h * QK_PAD
        q_out[:, lo:lo + LANES] = (qa[:, lo:lo + LANES] * scale).astype(BF16)
        q_pe = qa[:, lo + LANES:lo + 2 * LANES] * cos + qr[:, h * LANES:(h + 1) * LANES] * sin
        q_out[:, lo + LANES:lo + 2 * LANES] = (q_pe * scale).astype(BF16)
        k_out[:, lo:lo + LANES] = ka[:, lo:lo + LANES].astype(BF16)
        k_out[:, lo + LANES:lo + 2 * LANES] = kpe.astype(BF16)
    vt = _dot_nt(wv_ref[...], kvn).astype(BF16)
    ones = jnp.ones((ATTN_SUM_ROWS, vt.shape[1]), BF16)
    for h in range(MLA_HEADS):
        v_out[h * VT_ROWS:h * VT_ROWS + V_HEAD, :] = vt[h * V_HEAD:(h + 1) * V_HEAD, :]
        v_out[h * VT_ROWS + V_HEAD:(h + 1) * VT_ROWS, :] = ones


def _mla_prep(p, ang, q_norm, kv_norm, wq, wqr, wk, wv, layer, *, tm):
    m = p.shape[0]
    hq = MLA_HEADS * QK_PAD
    hv = MLA_HEADS * VT_ROWS
    row = lambda a: pl.BlockSpec(a.shape, lambda i: (0, 0))
    full = lambda a: pl.BlockSpec((None,) + a.shape[1:], lambda i: (layer, 0, 0))
    return pl.pallas_call(
        _mla_prep_kernel,
        grid=(m // tm,),
        in_specs=[
            pl.BlockSpec((tm, Q_LORA), lambda i: (i, 1)),
            pl.BlockSpec((tm, KV_LORA), lambda i: (i, 2)),
            pl.BlockSpec((tm, LANES), lambda i: (i, 12)),
            pl.BlockSpec((tm, LANES), lambda i: (i, 13)),
            pl.BlockSpec((tm, LANES), lambda i: (i, 0)),
            row(q_norm), row(kv_norm), full(wq), full(wqr), full(wk), full(wv),
        ],
        out_specs=[pl.BlockSpec((tm, hq), lambda i: (i, 0)),
                   pl.BlockSpec((tm, hq), lambda i: (i, 0)),
                   pl.BlockSpec((hv, tm), lambda i: (0, i))],
        out_shape=[jax.ShapeDtypeStruct((m, hq), BF16),
                   jax.ShapeDtypeStruct((m, hq), BF16),
                   jax.ShapeDtypeStruct((hv, m), BF16)],
        compiler_params=_cparams(("parallel",)),
        name="mla_prep",
    )(p, p, p, p, ang, q_norm, kv_norm, wq, wqr, wk, wv)


def _attn_kernel(q_ref, k_ref, vt_ref, o_ref, *, tq, tk, nh):
    qi = pl.program_id(2)
    heads = range(nh)
    qs = [q_ref[0, :, h * QK_PAD:(h + 1) * QK_PAD] for h in heads]
    diag = tq // tk

    def block(j, carry, masked):
        m, acc = carry
        sl = pl.ds(pl.multiple_of(j * tk, tk), tk)
        s = [_dot_nt(k_ref[0, sl, h * QK_PAD:(h + 1) * QK_PAD], qs[h]) for h in heads]
        if masked:
            key = j * tk + lax.broadcasted_iota(jnp.int32, (tk, tq), 0)
            qry = qi * tq + lax.broadcasted_iota(jnp.int32, (tk, tq), 1)
            s = [jnp.where(key <= qry, x, -1e30) for x in s]
        m_new = [jnp.maximum(a, jnp.max(x, axis=0, keepdims=True)) for a, x in zip(m, s)]
        alpha = [jnp.exp2(a - b) for a, b in zip(m, m_new)]
        p = [jnp.exp2(x - a).astype(BF16) for x, a in zip(s, m_new)]
        pv = [_dot(vt_ref[h * VT_ROWS:(h + 1) * VT_ROWS, sl], p[h]) for h in heads]
        acc = [a * b + x for a, b, x in zip(alpha, acc, pv)]
        return m_new, acc

    init = ([jnp.full((1, tq), -1e30, F32) for _ in heads],
            [jnp.zeros((V_HEAD + ATTN_SUM_ROWS, tq), F32) for _ in heads])
    carry = lax.fori_loop(0, qi * diag, lambda j, c: block(j, c, False), init)
    for d in range(diag):
        carry = block(qi * diag + d, carry, True)
    _, acc = carry
    for h in heads:
        out = acc[h][0:V_HEAD] / acc[h][V_HEAD:V_HEAD + 1]
        o_ref[0, :, h * V_HEAD:(h + 1) * V_HEAD] = jnp.transpose(out).astype(o_ref.dtype)


def _attention(q, k, vt, *, tq, tk, nh):
    b, s, _ = q.shape
    return pl.pallas_call(
        functools.partial(_attn_kernel, tq=tq, tk=tk, nh=nh),
        grid=(b, MLA_HEADS // nh, s // tq),
        in_specs=[
            pl.BlockSpec((1, tq, nh * QK_PAD), lambda bi, h, i: (bi, i, h)),
            pl.BlockSpec((1, s, nh * QK_PAD), lambda bi, h, i: (bi, 0, h)),
            pl.BlockSpec((nh * VT_ROWS, s), lambda bi, h, i: (h, bi)),
        ],
        out_specs=pl.BlockSpec((1, tq, nh * V_HEAD), lambda bi, h, i: (bi, i, h)),
        out_shape=jax.ShapeDtypeStruct((b, s, MLA_HEADS * V_HEAD), BF16),
        compiler_params=_cparams(("parallel", "parallel", "arbitrary")),
        name="mla_attention",
    )(q, k, vt)


def _pad_last(w, n):
    return jnp.pad(w, [(0, 0)] * (w.ndim - 1) + [(0, n - w.shape[-1])])


def _pad_rows(w, n):
    return jnp.pad(w, ((0, n - w.shape[0]), (0, 0)))


def _rot_cols(w):
    half = w.shape[-1] // 2
    return jnp.concatenate([-w[..., half:], w[..., :half]], axis=-1)


def _even_weights(w_in, mu):
    c = 3 * CONV_DIM + 3 * RWKV_DIM
    split = lambda t, base: (t[..., base:base + W_LORA],
                             t[..., base + W_LORA:base + W_LORA + A_LORA],
                             t[..., base + W_LORA + A_LORA:])
    xw, xa, xg = split(w_in, c)
    w = jnp.concatenate([w_in[..., :c], _pad_last(xw, LANES), _pad_last(xa, LANES),
                         _pad_last(xg, 2 * LANES)], axis=-1).astype(BF16)
    r = 3 * RWKV_DIM
    mw, ma, mg = split(mu, r)
    mu_l = jnp.concatenate([_pad_last(mw, LANES), _pad_last(ma, LANES), _pad_last(mg, 2 * LANES)],
                           axis=-1)
    return w, mu[:, None, :r], mu_l[:, None, :]


def _odd_weights(w_in, w_uq, w_ukv):
    n = w_in.shape[0]
    o3 = POOL_DIM + Q_LORA + KV_LORA
    kpe = w_in[..., o3:]
    w = jnp.concatenate([w_in[..., :o3], _pad_last(kpe, LANES), _pad_last(_rot_cols(kpe), LANES)],
                        axis=-1).astype(BF16)
    uq = w_uq.reshape(n, Q_LORA, MLA_HEADS, QK_NOPE + QK_ROPE)
    wq = _pad_last(uq, QK_PAD).reshape(n, Q_LORA, MLA_HEADS * QK_PAD).astype(BF16)
    wqr = _pad_last(_rot_cols(uq[..., QK_NOPE:]), LANES)
    wqr = wqr.reshape(n, Q_LORA, MLA_HEADS * LANES).astype(BF16)
    ukv = w_ukv.reshape(n, KV_LORA, MLA_HEADS, QK_NOPE + V_HEAD)
    wk = _pad_last(ukv[..., :QK_NOPE], QK_PAD).reshape(n, KV_LORA, MLA_HEADS * QK_PAD).astype(BF16)
    wv = ukv[..., QK_NOPE:].reshape(n, KV_LORA, MLA_HEADS * V_HEAD)
    return w, wq, wqr, wk, jnp.swapaxes(wv, 1, 2).astype(BF16)


def _tile(n, pref):
    return pref if n % pref == 0 else n


def kernel(x, positions, ev_norm, ev_w_in, ev_conv_w, ev_mu, ev_w0, ev_w2, ev_a0, ev_a2, ev_g2, ev_k_k, ev_k_a, ev_r_k, ev_ln_w, ev_ln_b, ev_w_out, od_norm, od_w_in, od_pool_w, od_pool_scale, od_q_norm, od_w_uq, od_kv_norm, od_w_ukv, od_w_out, ffn_norm, ffn_w_gate, ffn_w_up, ffn_w_down, final_norm):
    b, s, d = x.shape
    m = b * s
    depth = ffn_norm.shape[0]
    tm = _tile(m, 512)
    tm_in = _tile(m, 1024)
    ts = _tile(s, 512)
    ts_prep = _tile(s, 256)

    inv = 1.0 / (ROPE_THETA ** (jnp.arange(0, QK_ROPE, 2, dtype=F32) / QK_ROPE))
    inv = jnp.concatenate([inv, inv, jnp.zeros((LANES - QK_ROPE,), F32)])
    ang = (positions.astype(F32)[..., None] * inv).reshape(m, LANES)

    ev_w, ev_mu_main, ev_mu_lora = _even_weights(ev_w_in, ev_mu)
    od_w, wq, wqr, wk, wv = _odd_weights(od_w_in, od_w_uq, od_w_ukv)
    ev_wo = ev_w_out.astype(BF16)
    od_wo = od_w_out.astype(BF16)
    wg, wu, wd = ffn_w_gate.astype(BF16), ffn_w_up.astype(BF16), ffn_w_down.astype(BF16)
    row = lambda t: t[None, :]

    h = x.reshape(m, d)
    for layer in range(depth):
        j = layer // 2
        if layer % 2 == 0:
            p = _norm_matmul(h, row(ev_norm[j]), ev_w, j, tm=tm_in, tn=512, out_dtype=BF16)
            p = p.reshape(b, s, EV_COLS)
            ya = _conv_mixer(p, ev_conv_w[j].T, ts=ts, tc=512)
            r, k, v, kk, a, lw, g, bonus = _rwkv_prep(
                p, ev_mu_main[j], ev_mu_lora[j], row(ev_w0[j]),
                _pad_rows(ev_w2[j], LANES).astype(BF16), row(ev_a0[j]),
                _pad_rows(ev_a2[j], LANES).astype(BF16), _pad_rows(ev_g2[j], 2 * LANES).astype(BF16),
                row(ev_k_k[j]), row(ev_k_a[j]), row(ev_r_k[j].reshape(-1)), ts=ts_prep)
            y = _wkv(r, k, v, kk, a, lw, ts=ts, npair=8)
            yb = _rwkv_post(y.reshape(m, -1), bonus.reshape(m, -1), g.reshape(m, -1),
                            row(ev_ln_w[j]), row(ev_ln_b[j]), tm=tm)
            h = _proj_residual(ya.reshape(m, -1), yb, ev_wo, j, h, tm=tm, tn=d)
        else:
            p = _norm_matmul(h, row(od_norm[j]), od_w, j, tm=tm_in, tn=256, out_dtype=BF16)
            yc = _pool_mixer(p.reshape(b, s, OD_COLS), od_pool_w[j].astype(BF16),
                             row(od_pool_scale[j]), ts=ts)
            q, kq, vv = _mla_prep(p, ang, row(od_q_norm[j]), row(od_kv_norm[j]),
                                  wq, wqr, wk, wv, j, tm=_tile(m, 256))
            hq = MLA_HEADS * QK_PAD
            yd = _attention(q.reshape(b, s, hq), kq.reshape(b, s, hq), vv, tq=_tile(s, 512),
                            tk=_tile(s, 512), nh=4)
            h = _proj_residual(yc.reshape(m, -1), yd.reshape(m, -1), od_wo, j, h, tm=tm, tn=d)
        h = _ffn(h, row(ffn_norm[layer]), wg, wu, wd, layer, tm=tm_in, tf=256)
    return _final_norm(h, final_norm[None, :], tm=tm).reshape(b, s, d)
```

```python
import functools

import jax
import jax.numpy as jnp
from jax import lax
from jax.experimental import pallas as pl
from jax.experimental.pallas import tpu as pltpu

F32 = jnp.float32
BF16 = jnp.bfloat16

D_MODEL = 2048
NORM_EPS = 1e-6
CONV_DIM = 1024
RWKV_DIM = 1024
RWKV_HEAD = 64
W_LORA = 64
A_LORA = 64
G_LORA = 160
GN_EPS = 64e-5
POOL_WINDOWS = (2, 4, 8, 16)
POOL_GROUP = 128
POOL_DIM = 512
MLA_HEADS = 12
Q_LORA = 512
KV_LORA = 512
QK_NOPE = 128
QK_ROPE = 64
V_HEAD = 128
ROPE_THETA = 10000.0
D_FF = 5632
LOG2_E = 1.4426950408889634
EXP_NEG_HALF = 0.6065306597126334

LANES = 128
HIST = 16
VMEM_LIMIT = 56 * 1024 * 1024
QK_PAD = 256
ATTN_SUM_ROWS = 16
VT_ROWS = V_HEAD + ATTN_SUM_ROWS
WKV_CHUNK = 64
EV_COLS = 6656
EV_LORA = 512
OD_COLS = 1792


def _cparams(sem):
    return pltpu.CompilerParams(dimension_semantics=sem, vmem_limit_bytes=VMEM_LIMIT)


def _rms(x, g):
    ms = jnp.mean(x * x, axis=-1, keepdims=True)
    return x * lax.rsqrt(ms + NORM_EPS) * g


def _sigmoid(x):
    return 1.0 / (1.0 + jnp.exp(-x))


def _dot(a, b):
    return jnp.dot(a, b, preferred_element_type=F32)


def _dot_nt(a, b):
    return lax.dot_general(a, b, (((1,), (1,)), ((), ())), preferred_element_type=F32)


def _dot_tn(a, b):
    return lax.dot_general(a, b, (((0,), (0,)), ((), ())), preferred_element_type=F32)


def _norm_mm_kernel(x_ref, g_ref, w_ref, o_ref, xn_ref):
    @pl.when(pl.program_id(1) == 0)
    def _():
        xn_ref[...] = _rms(x_ref[...], g_ref[...]).astype(BF16)

    o_ref[...] = _dot(xn_ref[...], w_ref[...]).astype(o_ref.dtype)


def _norm_matmul(x, g, w, layer, *, tm, tn, out_dtype):
    m, k = x.shape
    n = w.shape[2]
    return pl.pallas_call(
        _norm_mm_kernel,
        grid=(m // tm, n // tn),
        in_specs=[
            pl.BlockSpec((tm, k), lambda i, j: (i, 0)),
            pl.BlockSpec((1, k), lambda i, j: (0, 0)),
            pl.BlockSpec((None, k, tn), lambda i, j: (layer, 0, j)),
        ],
        out_specs=pl.BlockSpec((tm, tn), lambda i, j: (i, j)),
        out_shape=jax.ShapeDtypeStruct((m, n), out_dtype),
        scratch_shapes=[pltpu.VMEM((tm, k), BF16)],
        compiler_params=_cparams(("parallel", "arbitrary")),
        name="norm_matmul",
    )(x, g, w)


def _proj_res_kernel(a_ref, b_ref, w_ref, h_ref, o_ref):
    ka = a_ref.shape[1]
    o_ref[...] = h_ref[...] + _dot(a_ref[...], w_ref[0:ka, :]) + _dot(b_ref[...], w_ref[ka:, :])


def _proj_residual(a, b, w, layer, h, *, tm, tn):
    m, n = h.shape
    ka, kb = a.shape[1], b.shape[1]
    return pl.pallas_call(
        _proj_res_kernel,
        grid=(m // tm, n // tn),
        in_specs=[
            pl.BlockSpec((tm, ka), lambda i, j: (i, 0)),
            pl.BlockSpec((tm, kb), lambda i, j: (i, 0)),
            pl.BlockSpec((None, ka + kb, tn), lambda i, j: (layer, 0, j)),
            pl.BlockSpec((tm, tn), lambda i, j: (i, j)),
        ],
        out_specs=pl.BlockSpec((tm, tn), lambda i, j: (i, j)),
        out_shape=jax.ShapeDtypeStruct((m, n), F32),
        compiler_params=_cparams(("parallel", "arbitrary")),
        name="proj_residual",
    )(a, b, w, h)


def _ffn_kernel(x_ref, g_ref, wg_ref, wu_ref, wd_ref, o_ref, xn_ref):
    @pl.when(pl.program_id(1) == 0)
    def _():
        x = x_ref[...]
        xn_ref[...] = _rms(x, g_ref[...]).astype(BF16)
        o_ref[...] = x

    xn = xn_ref[...]
    a = _dot(xn, wg_ref[...])
    b = _dot(xn, wu_ref[...])
    act = (a * _sigmoid(a) * b).astype(BF16)
    o_ref[...] += _dot(act, wd_ref[...])


def _ffn(x, g, wg, wu, wd, layer, *, tm, tf):
    m, d = x.shape
    f = wg.shape[2]
    return pl.pallas_call(
        _ffn_kernel,
        grid=(m // tm, f // tf),
        in_specs=[
            pl.BlockSpec((tm, d), lambda i, j: (i, 0)),
            pl.BlockSpec((1, d), lambda i, j: (0, 0)),
            pl.BlockSpec((None, d, tf), lambda i, j: (layer, 0, j)),
            pl.BlockSpec((None, d, tf), lambda i, j: (layer, 0, j)),
            pl.BlockSpec((None, tf, d), lambda i, j: (layer, j, 0)),
        ],
        out_specs=pl.BlockSpec((tm, d), lambda i, j: (i, 0)),
        out_shape=jax.ShapeDtypeStruct((m, d), F32),
        scratch_shapes=[pltpu.VMEM((tm, d), BF16)],
        compiler_params=_cparams(("parallel", "arbitrary")),
        name="ffn",
    )(x, g, wg, wu, wd)


def _final_norm_kernel(x_ref, g_ref, o_ref):
    o_ref[...] = _rms(x_ref[...], g_ref[...])


def _final_norm(x, g, *, tm):
    m, d = x.shape
    return pl.pallas_call(
        _final_norm_kernel,
        grid=(m // tm,),
        in_specs=[pl.BlockSpec((tm, d), lambda i: (i, 0)), pl.BlockSpec((1, d), lambda i: (0, 0))],
        out_specs=pl.BlockSpec((tm, d), lambda i: (i, 0)),
        out_shape=jax.ShapeDtypeStruct((m, d), F32),
        compiler_params=_cparams(("parallel",)),
        name="final_norm",
    )(x, g)


def _fill_history(buf_ref, prev, cur, hist, is_first):
    buf_ref[pl.ds(0, hist), :] = jnp.where(is_first, 0.0, prev)
    buf_ref[pl.ds(hist, cur.shape[0]), :] = cur


def _conv_kernel(bg_ref, cg_ref, hh_ref, cgp_ref, hhp_ref, w_ref, o_ref, buf_ref):
    ts = o_ref.shape[1]
    u = cg_ref[0].astype(F32) * hh_ref[0].astype(F32)
    up = cgp_ref[0].astype(F32) * hhp_ref[0].astype(F32)
    _fill_history(buf_ref, up, u, HIST, pl.program_id(1) == 0)
    u1 = buf_ref[pl.ds(HIST - 1, ts), :]
    u2 = buf_ref[pl.ds(HIST - 2, ts), :]
    w = w_ref[...]
    y = w[2:3] * u + w[1:2] * u1 + w[0:1] * u2
    o_ref[0] = (bg_ref[0].astype(F32) * y).astype(o_ref.dtype)


def _conv_mixer(p, conv_w_t, *, ts, tc):
    b, s, _ = p.shape
    nc = CONV_DIM // tc
    rb = ts // HIST

    def cur(off):
        return pl.BlockSpec((1, ts, tc), lambda bi, i, c: (bi, i, off * nc + c))

    def prev(off):
        return pl.BlockSpec((1, HIST, tc),
                            lambda bi, i, c: (bi, jnp.maximum(i * rb - 1, 0), off * nc + c))

    return pl.pallas_call(
        _conv_kernel,
        grid=(b, s // ts, nc),
        in_specs=[cur(0), cur(1), cur(2), prev(1), prev(2),
                  pl.BlockSpec((3, tc), lambda bi, i, c: (0, c))],
        out_specs=pl.BlockSpec((1, ts, tc), lambda bi, i, c: (bi, i, c)),
        out_shape=jax.ShapeDtypeStruct((b, s, CONV_DIM), BF16),
        scratch_shapes=[pltpu.VMEM((ts + HIST, tc), F32)],
        compiler_params=_cparams(("parallel", "parallel", "parallel")),
        name="conv_mixer",
    )(p, p, p, p, p, conv_w_t)


def _head_sum(x):
    r = lax.broadcasted_iota(jnp.int32, (LANES, LANES), 0) // RWKV_HEAD
    c = lax.broadcasted_iota(jnp.int32, (LANES, LANES), 1) // RWKV_HEAD
    ones = jnp.where(r == c, 1.0, 0.0).astype(BF16)
    hi = x.astype(BF16)
    lo = (x - hi.astype(F32)).astype(BF16)
    parts = []
    for g in range(x.shape[1] // LANES):
        sl = slice(g * LANES, (g + 1) * LANES)
        parts.append(_dot(hi[:, sl], ones) + _dot(lo[:, sl], ones))
    return jnp.concatenate(parts, axis=1)


def _token_shift(xb, prev_rows, is_first):
    ts = xb.shape[0]
    r = lax.broadcasted_iota(jnp.int32, (ts, ts), 0)
    c = lax.broadcasted_iota(jnp.int32, (ts, ts), 1)
    xs = _dot(jnp.where(r == c + 1, 1.0, 0.0).astype(BF16), xb)
    hist = prev_rows.shape[0]
    last = jnp.where(is_first, 0.0, prev_rows[hist - 1:hist, :].astype(F32))
    top = jnp.where(lax.broadcasted_iota(jnp.int32, (HIST, 1), 0) == 0, last, xs[0:HIST])
    return jnp.concatenate([top, xs[HIST:]], axis=0)


def _rwkv_prep_kernel(x_ref, xl_ref, xp_ref, xlp_ref, mu_ref, mul_ref, w0_ref, w2_ref, a0_ref,
                      a2_ref, g2_ref, kk_ref, ka_ref, rk_ref,
                      r_out, k_out, v_out, kkn_out, a_out, lw_out, g_out, bonus_out):
    first = pl.program_id(1) == 0
    d = RWKV_DIM

    x = x_ref[0].astype(F32)
    xm = x + (_token_shift(x_ref[0], xp_ref[0], first) - x) * mu_ref[...]
    xl = xl_ref[0].astype(F32)
    xlm = xl + (_token_shift(xl_ref[0], xlp_ref[0], first) - xl) * mul_ref[...]

    r = xm[:, 0:d]
    k = xm[:, d:2 * d]
    v = xm[:, 2 * d:3 * d]
    xw = xlm[:, 0:LANES]
    xa = xlm[:, LANES:2 * LANES]
    xg = xlm[:, 2 * LANES:4 * LANES]

    y = w0_ref[...] + _dot(jnp.tanh(xw).astype(BF16), w2_ref[...])
    lw = -EXP_NEG_HALF * _sigmoid(y)
    a = _sigmoid(a0_ref[...] + _dot(xa.astype(BF16), a2_ref[...]))
    g = _dot(_sigmoid(xg).astype(BF16), g2_ref[...])

    kk = k * kk_ref[...]
    kk = kk * lax.rsqrt(jnp.maximum(_head_sum(kk * kk), 1e-24))
    k = k * (1.0 + (a - 1.0) * ka_ref[...])
    bonus = _head_sum(r * k * rk_ref[...]) * v

    r_out[0] = r
    k_out[0] = k
    v_out[0] = v
    kkn_out[0] = kk
    a_out[0] = a
    lw_out[0] = lw
    g_out[0] = g
    bonus_out[0] = bonus


def _rwkv_prep(p, mu, mul, w0, w2, a0, a2, g2, k_k, k_a, r_k, *, ts):
    b, s, _ = p.shape
    d = RWKV_DIM
    rb = ts // HIST
    row = lambda shape: pl.BlockSpec(shape, lambda bi, i: (0, 0))
    out = pl.BlockSpec((1, ts, d), lambda bi, i: (bi, i, 0))
    return pl.pallas_call(
        _rwkv_prep_kernel,
        grid=(b, s // ts),
        in_specs=[
            pl.BlockSpec((1, ts, 3 * d), lambda bi, i: (bi, i, 1)),
            pl.BlockSpec((1, ts, EV_LORA), lambda bi, i: (bi, i, 6 * d // EV_LORA)),
            pl.BlockSpec((1, HIST, 3 * d), lambda bi, i: (bi, jnp.maximum(i * rb - 1, 0), 1)),
            pl.BlockSpec((1, HIST, EV_LORA),
                         lambda bi, i: (bi, jnp.maximum(i * rb - 1, 0), 6 * d // EV_LORA)),
            row((1, 3 * d)), row((1, EV_LORA)), row((1, d)), row((LANES, d)), row((1, d)),
            row((LANES, d)), row((2 * LANES, d)), row((1, d)), row((1, d)), row((1, d)),
        ],
        out_specs=[out] * 8,
        out_shape=[jax.ShapeDtypeStruct((b, s, d), F32)] * 8,
        compiler_params=_cparams(("parallel", "parallel")),
        name="rwkv_prep",
    )(p, p, p, p, mu, mul, w0, w2, a0, a2, g2, k_k, k_a, r_k)


WKV_LEVELS = 6
_MASK_STRICT = WKV_LEVELS
_MASK_INCL = WKV_LEVELS + 1
_MASK_EYE = WKV_LEVELS + 2


def _wkv_kernel(r_ref, k_ref, v_ref, kk_ref, a_ref, lw_ref, y_ref, t_ref, mask_ref):
    n = WKV_CHUNK
    nb, ts = y_ref.shape[0], y_ref.shape[1]
    npair = y_ref.shape[2] // LANES

    @pl.when(pl.program_id(2) == 0)
    def _():
        t_ref[...] = jnp.zeros_like(t_ref)
        row = lax.broadcasted_iota(jnp.int32, (2 * n, 2 * n), 0)
        col = lax.broadcasted_iota(jnp.int32, (2 * n, 2 * n), 1)
        same = (row // n) == (col // n)
        x = row ^ col
        level = jnp.zeros_like(x)
        for bit in range(1, WKV_LEVELS):
            level = level + jnp.where(x >= (1 << bit), 1, 0)
        level = jnp.where(same & (row > col), level, -1)
        for lev in range(WKV_LEVELS):
            mask_ref[lev] = jnp.where(level == lev, 1.0, 0.0)
        mask_ref[_MASK_STRICT] = jnp.where(level >= 0, 1.0, 0.0)
        mask_ref[_MASK_INCL] = jnp.where(same & (row >= col), 1.0, 0.0)
        mask_ref[_MASK_EYE] = jnp.where(row == col, 1.0, 0.0)

    head0 = lax.broadcasted_iota(jnp.int32, (n, LANES), 1) < RWKV_HEAD
    tri = jnp.where(lax.broadcasted_iota(jnp.int32, (n, n), 0)
                    >= lax.broadcasted_iota(jnp.int32, (n, n), 1), 1.0, 0.0).astype(BF16)

    def stack(t):
        z = jnp.zeros_like(t)
        return jnp.concatenate([jnp.where(head0, t, z), jnp.where(head0, z, t)], axis=0).astype(BF16)

    def operands(sl, bi, q, cw):
        lanes = slice(q * LANES, (q + 1) * LANES)
        r = r_ref[bi, sl, lanes]
        k = k_ref[bi, sl, lanes]
        kk = kk_ref[bi, sl, lanes]
        lw = lw_ref[bi, sl, lanes]
        cw_end = cw[n - 1:n, :]
        p_inv = jnp.exp(-cw)
        p_to_end = jnp.exp(cw_end - cw)
        kka = kk * a_ref[bi, sl, lanes]
        xar = jnp.concatenate([stack(-kk * jnp.exp(cw - lw)), stack(r * jnp.exp(cw))], axis=0)
        ybk = jnp.concatenate([stack(kka * p_inv), stack(k * p_inv)], axis=0)
        return dict(xar=xar, ybk=ybk, bt=stack(kka * p_to_end), kt=stack(k * p_to_end),
                    vs=stack(v_ref[bi, sl, lanes]), decay=jnp.exp(cw_end))

    def cumulative(sl, bi):
        lw = lw_ref[bi, sl, :]
        hi = lw.astype(BF16)
        r1 = lw - hi.astype(F32)
        mid = r1.astype(BF16)
        lo = (r1 - mid.astype(F32)).astype(BF16)
        return _dot(tri, hi) + _dot(tri, mid) + _dot(tri, lo)

    def chunk(c, carry):
        sl = pl.ds(pl.multiple_of(c * n, n), n)
        cw = [cumulative(sl, bi) for bi in range(nb)]
        strict = mask_ref[_MASK_STRICT] > 0.5
        incl = mask_ref[_MASK_INCL] > 0.5

        def scores(group):
            ops = [operands(sl, bi, q, cw[bi][:, q * LANES:(q + 1) * LANES]) for bi, q in group]
            sc = [_dot_nt(o["xar"], o["ybk"]) for o in ops]
            for o, s in zip(ops, sc):
                o["a_ab"] = jnp.where(strict, s[0:2 * n, 0:2 * n], 0.0)
                o["a_k"] = jnp.concatenate([jnp.where(strict, s[0:2 * n, 2 * n:4 * n], 0.0),
                                            jnp.where(incl, s[2 * n:4 * n, 2 * n:4 * n], 0.0)],
                                           axis=0).astype(BF16)
                o["a_rb"] = jnp.where(incl, s[2 * n:4 * n, 0:2 * n], 0.0).astype(BF16)
            return ops

        def inverse(ops):
            m = [mask_ref[_MASK_EYE] + o["a_ab"] * mask_ref[0] for o in ops]
            for lev in range(1, WKV_LEVELS):
                mb = [x.astype(BF16) for x in m]
                em = [_dot((o["a_ab"] * mask_ref[lev]).astype(BF16), x).astype(BF16)
                      for o, x in zip(ops, mb)]
                m = [x + _dot(xb, y) for x, xb, y in zip(m, mb, em)]
            return m

        def finish(group, ops, m):
            t = [t_ref[bi * npair + q] for bi, q in group]
            xtv = [_dot(jnp.concatenate([o["xar"], o["a_k"]], axis=1),
                        jnp.concatenate([x.astype(BF16), o["vs"]], axis=0)) for o, x in zip(ops, t)]
            u = [_dot(x.astype(BF16), y[0:2 * n].astype(BF16)).astype(BF16) for x, y in zip(m, xtv)]
            for i, (bi, q) in enumerate(group):
                o = ops[i]
                ys = xtv[i][2 * n:4 * n] + _dot(o["a_rb"], u[i])
                y_ref[bi, sl, q * LANES:(q + 1) * LANES] = ys[0:n] + ys[n:2 * n]
                t_ref[bi * npair + q] = \
                    t[i] * jnp.transpose(jnp.broadcast_to(o["decay"], (LANES, LANES))) \
                    + _dot_tn(jnp.concatenate([o["bt"], o["kt"]], axis=0),
                              jnp.concatenate([u[i], o["vs"]], axis=0))

        group = [(bi, q) for bi in range(nb) for q in range(npair)]
        ops = scores(group)
        finish(group, ops, inverse(ops))
        return carry

    lax.fori_loop(0, ts // n, chunk, 0)


def _wkv(r, k, v, kk, a, lw, *, ts, npair, nb):
    b, s, d = r.shape
    wd = npair * LANES
    spec = pl.BlockSpec((nb, ts, wd), lambda bi, p, i: (bi, i, p))
    return pl.pallas_call(
        _wkv_kernel,
        grid=(b // nb, d // wd, s // ts),
        in_specs=[spec] * 6,
        out_specs=spec,
        out_shape=jax.ShapeDtypeStruct((b, s, d), F32),
        scratch_shapes=[pltpu.VMEM((nb * npair, LANES, LANES), F32),
                        pltpu.VMEM((WKV_LEVELS + 3, LANES, LANES), F32)],
        compiler_params=_cparams(("parallel", "parallel", "arbitrary")),
        name="wkv7",
    )(r, k, v, kk, a, lw)


def _rwkv_post_kernel(y_ref, bonus_ref, g_ref, lnw_ref, lnb_ref, o_ref):
    y = y_ref[...]
    inv_n = 1.0 / RWKV_HEAD
    mean = _head_sum(y) * inv_n
    yc = y - mean
    var = _head_sum(yc * yc) * inv_n
    yn = yc * lax.rsqrt(var + GN_EPS) * lnw_ref[...] + lnb_ref[...]
    o_ref[...] = ((yn + bonus_ref[...]) * g_ref[...]).astype(o_ref.dtype)


def _rwkv_post(y, bonus, g, ln_w, ln_b, *, tm):
    m, d = y.shape
    blk = pl.BlockSpec((tm, d), lambda i: (i, 0))
    row = pl.BlockSpec((1, d), lambda i: (0, 0))
    return pl.pallas_call(
        _rwkv_post_kernel,
        grid=(m // tm,),
        in_specs=[blk, blk, blk, row, row],
        out_specs=blk,
        out_shape=jax.ShapeDtypeStruct((m, d), BF16),
        compiler_params=_cparams(("parallel",)),
        name="rwkv_post",
    )(y, bonus, g, ln_w, ln_b)


def _pool_kernel(u_ref, up_ref, w_ref, sc_ref, o_ref, buf_ref):
    ts = o_ref.shape[1]
    hist = max(POOL_WINDOWS)
    i = pl.program_id(1)
    u = u_ref[0].astype(F32)
    _fill_history(buf_ref, up_ref[0].astype(F32), u, hist, i == 0)
    pos = i * ts + lax.broadcasted_iota(jnp.int32, (ts, 1), 0)
    outs = []
    for gi, win in enumerate(POOL_WINDOWS):
        cols = pl.ds(gi * POOL_GROUP, POOL_GROUP)
        acc = buf_ref[pl.ds(hist, ts), cols]
        for j in range(1, win):
            acc = acc + buf_ref[pl.ds(hist - j, ts), cols]
        count = jnp.minimum(pos + 1, win).astype(F32)
        dg = acc / count - u[:, gi * POOL_GROUP:(gi + 1) * POOL_GROUP]
        outs.append(_dot(dg.astype(BF16), w_ref[gi]))
    y = jnp.concatenate(outs, axis=1) * sc_ref[...]
    o_ref[0] = y.astype(o_ref.dtype)


def _pool_mixer(p, pool_w, pool_scale, *, ts):
    b, s, _ = p.shape
    hist = max(POOL_WINDOWS)
    rb = ts // hist
    return pl.pallas_call(
        _pool_kernel,
        grid=(b, s // ts),
        in_specs=[
            pl.BlockSpec((1, ts, POOL_DIM), lambda bi, i: (bi, i, 0)),
            pl.BlockSpec((1, hist, POOL_DIM), lambda bi, i: (bi, jnp.maximum(i * rb - 1, 0), 0)),
            pl.BlockSpec((len(POOL_WINDOWS), POOL_GROUP, POOL_GROUP), lambda bi, i: (0, 0, 0)),
            pl.BlockSpec((1, POOL_DIM), lambda bi, i: (0, 0)),
        ],
        out_specs=pl.BlockSpec((1, ts, POOL_DIM), lambda bi, i: (bi, i, 0)),
        out_shape=jax.ShapeDtypeStruct((b, s, POOL_DIM), BF16),
        scratch_shapes=[pltpu.VMEM((ts + hist, POOL_DIM), F32)],
        compiler_params=_cparams(("parallel", "parallel")),
        name="pool_mixer",
    )(p, p, pool_w, pool_scale)


def _mla_prep_kernel(ql_ref, kvl_ref, kpe_ref, kper_ref, ang_ref, qn_ref, kvn_ref,
                     wq_ref, wqr_ref, wk_ref, wv_ref, q_out, k_out, v_out):
    cos = jnp.cos(ang_ref[...])
    sin = jnp.sin(ang_ref[...])
    qn = _rms(ql_ref[...].astype(F32), qn_ref[...]).astype(BF16)
    kvn = _rms(kvl_ref[...].astype(F32), kvn_ref[...]).astype(BF16)
    scale = (QK_NOPE + QK_ROPE) ** -0.5 * LOG2_E
    qa = _dot(qn, wq_ref[...])
    qr = _dot(qn, wqr_ref[...])
    ka = _dot(kvn, wk_ref[...])
    kpe = kpe_ref[...].astype(F32) * cos + kper_ref[...].astype(F32) * sin
    for h in range(MLA_HEADS):
        lo = h * QK_PAD
        q_out[:, lo:lo + LANES] = (qa[:, lo:lo + LANES] * scale).astype(BF16)
        q_pe = qa[:, lo + LANES:lo + 2 * LANES] * cos + qr[:, h * LANES:(h + 1) * LANES] * sin
        q_out[:, lo + LANES:lo + 2 * LANES] = (q_pe * scale).astype(BF16)
        k_out[:, lo:lo + LANES] = ka[:, h * QK_NOPE:(h + 1) * QK_NOPE].astype(BF16)
        k_out[:, lo + LANES:lo + 2 * LANES] = kpe.astype(BF16)
    vt = _dot_nt(wv_ref[...], kvn).astype(BF16)
    ones = jnp.ones((ATTN_SUM_ROWS, vt.shape[1]), BF16)
    for h in range(MLA_HEADS):
        v_out[h * VT_ROWS:h * VT_ROWS + V_HEAD, :] = vt[h * V_HEAD:(h + 1) * V_HEAD, :]
        v_out[h * VT_ROWS + V_HEAD:(h + 1) * VT_ROWS, :] = ones


def _mla_prep(p, ang, q_norm, kv_norm, wq, wqr, wk, wv, layer, *, tm):
    m = p.shape[0]
    hq = MLA_HEADS * QK_PAD
    hv = MLA_HEADS * VT_ROWS
    row = lambda a: pl.BlockSpec(a.shape, lambda i: (0, 0))
    full = lambda a: pl.BlockSpec((None,) + a.shape[1:], lambda i: (layer, 0, 0))
    return pl.pallas_call(
        _mla_prep_kernel,
        grid=(m // tm,),
        in_specs=[
            pl.BlockSpec((tm, Q_LORA), lambda i: (i, 1)),
            pl.BlockSpec((tm, KV_LORA), lambda i: (i, 2)),
            pl.BlockSpec((tm, LANES), lambda i: (i, 12)),
            pl.BlockSpec((tm, LANES), lambda i: (i, 13)),
            pl.BlockSpec((tm, LANES), lambda i: (i, 0)),
            row(q_norm), row(kv_norm), full(wq), full(wqr), full(wk), full(wv),
        ],
        out_specs=[pl.BlockSpec((tm, hq), lambda i: (i, 0)),
                   pl.BlockSpec((tm, hq), lambda i: (i, 0)),
                   pl.BlockSpec((hv, tm), lambda i: (0, i))],
        out_shape=[jax.ShapeDtypeStruct((m, hq), BF16),
                   jax.ShapeDtypeStruct((m, hq), BF16),
                   jax.ShapeDtypeStruct((hv, m), BF16)],
        compiler_params=_cparams(("parallel",)),
        name="mla_prep",
    )(p, p, p, p, ang, q_norm, kv_norm, wq, wqr, wk, wv)


def _attn_kernel(q_ref, k_ref, vt_ref, o_ref, *, tq, tk, nh):
    qi = pl.program_id(2)
    heads = range(nh)
    qs = [q_ref[0, :, h * QK_PAD:(h + 1) * QK_PAD] for h in heads]
    diag = tq // tk

    def block(j, carry, masked):
        m, acc = carry
        sl = pl.ds(pl.multiple_of(j * tk, tk), tk)
        s = [_dot_nt(k_ref[0, sl, h * QK_PAD:(h + 1) * QK_PAD], qs[h]) for h in heads]
        if masked:
            key = j * tk + lax.broadcasted_iota(jnp.int32, (tk, tq), 0)
            qry = qi * tq + lax.broadcasted_iota(jnp.int32, (tk, tq), 1)
            s = [jnp.where(key <= qry, x, -1e30) for x in s]
        m_new = [jnp.maximum(a, jnp.max(x, axis=0, keepdims=True)) for a, x in zip(m, s)]
        alpha = [jnp.exp2(a - b) for a, b in zip(m, m_new)]
        p = [jnp.exp2(x - a).astype(BF16) for x, a in zip(s, m_new)]
        pv = [_dot(vt_ref[h * VT_ROWS:(h + 1) * VT_ROWS, sl], p[h]) for h in heads]
        acc = [a * b + x for a, b, x in zip(alpha, acc, pv)]
        return m_new, acc

    init = ([jnp.full((1, tq), -1e30, F32) for _ in heads],
            [jnp.zeros((V_HEAD + ATTN_SUM_ROWS, tq), F32) for _ in heads])
    carry = lax.fori_loop(0, qi * diag, lambda j, c: block(j, c, False), init)
    for d in range(diag):
        carry = block(qi * diag + d, carry, True)
    _, acc = carry
    for h in heads:
        out = acc[h][0:V_HEAD] / acc[h][V_HEAD:V_HEAD + 1]
        o_ref[0, :, h * V_HEAD:(h + 1) * V_HEAD] = jnp.transpose(out).astype(o_ref.dtype)


def _attention(q, k, vt, *, tq, tk, nh):
    b, s, _ = q.shape
    return pl.pallas_call(
        functools.partial(_attn_kernel, tq=tq, tk=tk, nh=nh),
        grid=(b, MLA_HEADS // nh, s // tq),
        in_specs=[
            pl.BlockSpec((1, tq, nh * QK_PAD), lambda bi, h, i: (bi, i, h)),
            pl.BlockSpec((1, s, nh * QK_PAD), lambda bi, h, i: (bi, 0, h)),
            pl.BlockSpec((nh * VT_ROWS, s), lambda bi, h, i: (h, bi)),
        ],
        out_specs=pl.BlockSpec((1, tq, nh * V_HEAD), lambda bi, h, i: (bi, i, h)),
        out_shape=jax.ShapeDtypeStruct((b, s, MLA_HEADS * V_HEAD), BF16),
        compiler_params=_cparams(("parallel", "parallel", "arbitrary")),
        name="mla_attention",
    )(q, k, vt)


def _pad_last(w, n):
    return jnp.pad(w, [(0, 0)] * (w.ndim - 1) + [(0, n - w.shape[-1])])


def _pad_rows(w, n):
    return jnp.pad(w, ((0, n - w.shape[0]), (0, 0)))


def _rot_cols(w):
    half = w.shape[-1] // 2
    return jnp.concatenate([-w[..., half:], w[..., :half]], axis=-1)


def _even_weights(w_in, mu):
    c = 3 * CONV_DIM + 3 * RWKV_DIM
    split = lambda t, base: (t[..., base:base + W_LORA],
                             t[..., base + W_LORA:base + W_LORA + A_LORA],
                             t[..., base + W_LORA + A_LORA:])
    xw, xa, xg = split(w_in, c)
    w = jnp.concatenate([w_in[..., :c], _pad_last(xw, LANES), _pad_last(xa, LANES),
                         _pad_last(xg, 2 * LANES)], axis=-1).astype(BF16)
    r = 3 * RWKV_DIM
    mw, ma, mg = split(mu, r)
    mu_l = jnp.concatenate([_pad_last(mw, LANES), _pad_last(ma, LANES), _pad_last(mg, 2 * LANES)],
                           axis=-1)
    return w, mu[:, None, :r], mu_l[:, None, :]


def _odd_weights(w_in, w_uq, w_ukv):
    n = w_in.shape[0]
    o3 = POOL_DIM + Q_LORA + KV_LORA
    kpe = w_in[..., o3:]
    w = jnp.concatenate([w_in[..., :o3], _pad_last(kpe, LANES), _pad_last(_rot_cols(kpe), LANES)],
                        axis=-1).astype(BF16)
    uq = w_uq.reshape(n, Q_LORA, MLA_HEADS, QK_NOPE + QK_ROPE)
    wq = _pad_last(uq, QK_PAD).reshape(n, Q_LORA, MLA_HEADS * QK_PAD).astype(BF16)
    wqr = _pad_last(_rot_cols(uq[..., QK_NOPE:]), LANES)
    wqr = wqr.reshape(n, Q_LORA, MLA_HEADS * LANES).astype(BF16)
    ukv = w_ukv.reshape(n, KV_LORA, MLA_HEADS, QK_NOPE + V_HEAD)
    wk = ukv[..., :QK_NOPE].reshape(n, KV_LORA, MLA_HEADS * QK_NOPE).astype(BF16)
    wv = ukv[..., QK_NOPE:].reshape(n, KV_LORA, MLA_HEADS * V_HEAD)
    return w, wq, wqr, wk, jnp.swapaxes(wv, 1, 2).astype(BF16)


def _tile(n, pref):
    return pref if n % pref == 0 else n


def kernel(x, positions, ev_norm, ev_w_in, ev_conv_w, ev_mu, ev_w0, ev_w2, ev_a0, ev_a2, ev_g2, ev_k_k, ev_k_a, ev_r_k, ev_ln_w, ev_ln_b, ev_w_out, od_norm, od_w_in, od_pool_w, od_pool_scale, od_q_norm, od_w_uq, od_kv_norm, od_w_ukv, od_w_out, ffn_norm, ffn_w_gate, ffn_w_up, ffn_w_down, final_norm):
    b, s, d = x.shape
    m = b * s
    depth = ffn_norm.shape[0]
    tm = _tile(m, 512)
    tm_in = _tile(m, 1024)
    ts = _tile(s, 512)
    ts_prep = _tile(s, 256)

    inv = 1.0 / (ROPE_THETA ** (jnp.arange(0, QK_ROPE, 2, dtype=F32) / QK_ROPE))
    inv = jnp.concatenate([inv, inv, jnp.zeros((LANES - QK_ROPE,), F32)])
    ang = (positions.astype(F32)[..., None] * inv).reshape(m, LANES)

    ev_w, ev_mu_main, ev_mu_lora = _even_weights(ev_w_in, ev_mu)
    od_w, wq, wqr, wk, wv = _odd_weights(od_w_in, od_w_uq, od_w_ukv)
    ev_wo = ev_w_out.astype(BF16)
    od_wo = od_w_out.astype(BF16)
    wg, wu, wd = ffn_w_gate.astype(BF16), ffn_w_up.astype(BF16), ffn_w_down.astype(BF16)
    row = lambda t: t[None, :]

    h = x.reshape(m, d)
    for layer in range(depth):
        j = layer // 2
        if layer % 2 == 0:
            p = _norm_matmul(h, row(ev_norm[j]), ev_w, j, tm=tm_in, tn=512, out_dtype=BF16)
            p = p.reshape(b, s, EV_COLS)
            ya = _conv_mixer(p, ev_conv_w[j].T, ts=ts, tc=512)
            r, k, v, kk, a, lw, g, bonus = _rwkv_prep(
                p, ev_mu_main[j], ev_mu_lora[j], row(ev_w0[j]),
                _pad_rows(ev_w2[j], LANES).astype(BF16), row(ev_a0[j]),
                _pad_rows(ev_a2[j], LANES).astype(BF16), _pad_rows(ev_g2[j], 2 * LANES).astype(BF16),
                row(ev_k_k[j]), row(ev_k_a[j]), row(ev_r_k[j].reshape(-1)), ts=ts_prep)
            y = _wkv(r, k, v, kk, a, lw, ts=_tile(s, 256), npair=8, nb=2 if b % 2 == 0 else 1)
            yb = _rwkv_post(y.reshape(m, -1), bonus.reshape(m, -1), g.reshape(m, -1),
                            row(ev_ln_w[j]), row(ev_ln_b[j]), tm=tm)
            h = _proj_residual(ya.reshape(m, -1), yb, ev_wo, j, h, tm=tm, tn=d)
        else:
            p = _norm_matmul(h, row(od_norm[j]), od_w, j, tm=tm_in, tn=256, out_dtype=BF16)
            yc = _pool_mixer(p.reshape(b, s, OD_COLS), od_pool_w[j].astype(BF16),
                             row(od_pool_scale[j]), ts=ts)
            q, kq, vv = _mla_prep(p, ang, row(od_q_norm[j]), row(od_kv_norm[j]),
                                  wq, wqr, wk, wv, j, tm=_tile(m, 256))
            hq = MLA_HEADS * QK_PAD
            yd = _attention(q.reshape(b, s, hq), kq.reshape(b, s, hq), vv, tq=_tile(s, 512),
                            tk=_tile(s, 512), nh=4)
            h = _proj_residual(yc.reshape(m, -1), yd.reshape(m, -1), od_wo, j, h, tm=tm, tn=d)
        h = _ffn(h, row(ffn_norm[layer]), wg, wu, wd, layer, tm=tm_in, tf=256)
    return _final_norm(h, final_norm[None, :], tm=tm).reshape(b, s, d)
```

```python
import functools

import jax
import jax.numpy as jnp
from jax import lax
from jax.experimental import pallas as pl
from jax.experimental.pallas import tpu as pltpu

F32 = jnp.float32
BF16 = jnp.bfloat16

D_MODEL = 2048
NORM_EPS = 1e-6
CONV_DIM = 1024
RWKV_DIM = 1024
RWKV_HEAD = 64
W_LORA = 64
A_LORA = 64
G_LORA = 160
GN_EPS = 64e-5
POOL_WINDOWS = (2, 4, 8, 16)
POOL_GROUP = 128
POOL_DIM = 512
MLA_HEADS = 12
Q_LORA = 512
KV_LORA = 512
QK_NOPE = 128
QK_ROPE = 64
V_HEAD = 128
ROPE_THETA = 10000.0
D_FF = 5632
LOG2_E = 1.4426950408889634
EXP_NEG_HALF = 0.6065306597126334

LANES = 128
HIST = 16
VMEM_LIMIT = 60 * 1024 * 1024
QK_PAD = 256
ATTN_SUM_ROWS = 16
VT_ROWS = V_HEAD + ATTN_SUM_ROWS
WKV_CHUNK = 64
EV_COLS = 6656
EV_LORA = 512
OD_COLS = 1792


def _cparams(sem):
    return pltpu.CompilerParams(dimension_semantics=sem, vmem_limit_bytes=VMEM_LIMIT)


def _rms(x, g):
    ms = jnp.mean(x * x, axis=-1, keepdims=True)
    return x * lax.rsqrt(ms + NORM_EPS) * g


def _sigmoid(x):
    return 1.0 / (1.0 + jnp.exp(-x))


def _dot(a, b):
    return jnp.dot(a, b, preferred_element_type=F32)


def _dot_nt(a, b):
    return lax.dot_general(a, b, (((1,), (1,)), ((), ())), preferred_element_type=F32)


def _dot_tn(a, b):
    return lax.dot_general(a, b, (((0,), (0,)), ((), ())), preferred_element_type=F32)


def _norm_mm_kernel(x_ref, g_ref, w_ref, o_ref, xn_ref):
    @pl.when(pl.program_id(1) == 0)
    def _():
        xn_ref[...] = _rms(x_ref[...], g_ref[...]).astype(BF16)

    o_ref[...] = _dot(xn_ref[...], w_ref[...]).astype(o_ref.dtype)


def _norm_matmul(x, g, w, layer, *, tm, tn, out_dtype):
    m, k = x.shape
    n = w.shape[2]
    return pl.pallas_call(
        _norm_mm_kernel,
        grid=(m // tm, n // tn),
        in_specs=[
            pl.BlockSpec((tm, k), lambda i, j: (i, 0)),
            pl.BlockSpec((1, k), lambda i, j: (0, 0)),
            pl.BlockSpec((None, k, tn), lambda i, j: (layer, 0, j)),
        ],
        out_specs=pl.BlockSpec((tm, tn), lambda i, j: (i, j)),
        out_shape=jax.ShapeDtypeStruct((m, n), out_dtype),
        scratch_shapes=[pltpu.VMEM((tm, k), BF16)],
        compiler_params=_cparams(("parallel", "arbitrary")),
        name="norm_matmul",
    )(x, g, w)


def _proj_res_kernel(a_ref, b_ref, w_ref, h_ref, o_ref):
    ka = a_ref.shape[1]
    o_ref[...] = h_ref[...] + _dot(a_ref[...], w_ref[0:ka, :]) + _dot(b_ref[...], w_ref[ka:, :])


def _proj_residual(a, b, w, layer, h, *, tm, tn):
    m, n = h.shape
    ka, kb = a.shape[1], b.shape[1]
    return pl.pallas_call(
        _proj_res_kernel,
        grid=(m // tm, n // tn),
        in_specs=[
            pl.BlockSpec((tm, ka), lambda i, j: (i, 0)),
            pl.BlockSpec((tm, kb), lambda i, j: (i, 0)),
            pl.BlockSpec((None, ka + kb, tn), lambda i, j: (layer, 0, j)),
            pl.BlockSpec((tm, tn), lambda i, j: (i, j)),
        ],
        out_specs=pl.BlockSpec((tm, tn), lambda i, j: (i, j)),
        out_shape=jax.ShapeDtypeStruct((m, n), F32),
        compiler_params=_cparams(("parallel", "arbitrary")),
        name="proj_residual",
    )(a, b, w, h)


def _ffn_kernel(x_ref, g_ref, gout_ref, wg_ref, wu_ref, wd_ref, o_ref, xn_ref, *, norm_out):
    @pl.when(pl.program_id(1) == 0)
    def _():
        x = x_ref[...]
        xn_ref[...] = _rms(x, g_ref[...]).astype(BF16)
        o_ref[...] = x

    xn = xn_ref[...]
    a = _dot(xn, wg_ref[...])
    b = _dot(xn, wu_ref[...])
    act = (a * _sigmoid(a) * b).astype(BF16)
    o_ref[...] += _dot(act, wd_ref[...])

    if norm_out:
        @pl.when(pl.program_id(1) == pl.num_programs(1) - 1)
        def _():
            o_ref[...] = _rms(o_ref[...], gout_ref[...])


def _ffn(x, g, g_out, wg, wu, wd, layer, *, tm, tf, norm_out):
    m, d = x.shape
    f = wg.shape[2]
    return pl.pallas_call(
        functools.partial(_ffn_kernel, norm_out=norm_out),
        grid=(m // tm, f // tf),
        in_specs=[
            pl.BlockSpec((tm, d), lambda i, j: (i, 0)),
            pl.BlockSpec((1, d), lambda i, j: (0, 0)),
            pl.BlockSpec((1, d), lambda i, j: (0, 0)),
            pl.BlockSpec((None, d, tf), lambda i, j: (layer, 0, j)),
            pl.BlockSpec((None, d, tf), lambda i, j: (layer, 0, j)),
            pl.BlockSpec((None, tf, d), lambda i, j: (layer, j, 0)),
        ],
        out_specs=pl.BlockSpec((tm, d), lambda i, j: (i, 0)),
        out_shape=jax.ShapeDtypeStruct((m, d), F32),
        scratch_shapes=[pltpu.VMEM((tm, d), BF16)],
        compiler_params=_cparams(("parallel", "arbitrary")),
        name="ffn",
    )(x, g, g_out, wg, wu, wd)


def _fill_history(buf_ref, prev, cur, hist, is_first):
    buf_ref[pl.ds(0, hist), :] = jnp.where(is_first, 0.0, prev)
    buf_ref[pl.ds(hist, cur.shape[0]), :] = cur


def _conv_kernel(bg_ref, cg_ref, hh_ref, cgp_ref, hhp_ref, w_ref, o_ref, buf_ref):
    ts = o_ref.shape[1]
    u = cg_ref[0].astype(F32) * hh_ref[0].astype(F32)
    up = cgp_ref[0].astype(F32) * hhp_ref[0].astype(F32)
    _fill_history(buf_ref, up, u, HIST, pl.program_id(1) == 0)
    u1 = buf_ref[pl.ds(HIST - 1, ts), :]
    u2 = buf_ref[pl.ds(HIST - 2, ts), :]
    w = w_ref[...]
    y = w[2:3] * u + w[1:2] * u1 + w[0:1] * u2
    o_ref[0] = (bg_ref[0].astype(F32) * y).astype(o_ref.dtype)


def _conv_mixer(p, conv_w_t, *, ts, tc):
    b, s, _ = p.shape
    nc = CONV_DIM // tc
    rb = ts // HIST

    def cur(off):
        return pl.BlockSpec((1, ts, tc), lambda bi, i, c: (bi, i, off * nc + c))

    def prev(off):
        return pl.BlockSpec((1, HIST, tc),
                            lambda bi, i, c: (bi, jnp.maximum(i * rb - 1, 0), off * nc + c))

    return pl.pallas_call(
        _conv_kernel,
        grid=(b, s // ts, nc),
        in_specs=[cur(0), cur(1), cur(2), prev(1), prev(2),
                  pl.BlockSpec((3, tc), lambda bi, i, c: (0, c))],
        out_specs=pl.BlockSpec((1, ts, tc), lambda bi, i, c: (bi, i, c)),
        out_shape=jax.ShapeDtypeStruct((b, s, CONV_DIM), BF16),
        scratch_shapes=[pltpu.VMEM((ts + HIST, tc), F32)],
        compiler_params=_cparams(("parallel", "parallel", "parallel")),
        name="conv_mixer",
    )(p, p, p, p, p, conv_w_t)


def _head_sum(x):
    r = lax.broadcasted_iota(jnp.int32, (LANES, LANES), 0) // RWKV_HEAD
    c = lax.broadcasted_iota(jnp.int32, (LANES, LANES), 1) // RWKV_HEAD
    ones = jnp.where(r == c, 1.0, 0.0).astype(BF16)
    hi = x.astype(BF16)
    lo = (x - hi.astype(F32)).astype(BF16)
    parts = []
    for g in range(x.shape[1] // LANES):
        sl = slice(g * LANES, (g + 1) * LANES)
        parts.append(_dot(hi[:, sl], ones) + _dot(lo[:, sl], ones))
    return jnp.concatenate(parts, axis=1)


def _token_shift(xb, prev_rows, is_first):
    ts = xb.shape[0]
    r = lax.broadcasted_iota(jnp.int32, (ts, ts), 0)
    c = lax.broadcasted_iota(jnp.int32, (ts, ts), 1)
    xs = _dot(jnp.where(r == c + 1, 1.0, 0.0).astype(BF16), xb)
    hist = prev_rows.shape[0]
    last = jnp.where(is_first, 0.0, prev_rows[hist - 1:hist, :].astype(F32))
    top = jnp.where(lax.broadcasted_iota(jnp.int32, (HIST, 1), 0) == 0, last, xs[0:HIST])
    return jnp.concatenate([top, xs[HIST:]], axis=0)


def _rwkv_prep_kernel(x_ref, xl_ref, xp_ref, xlp_ref, mu_ref, mul_ref, w0_ref, w2_ref, a0_ref,
                      a2_ref, g2_ref, kk_ref, ka_ref, rk_ref,
                      r_out, k_out, v_out, kkn_out, a_out, lw_out, g_out, bonus_out):
    first = pl.program_id(1) == 0
    d = RWKV_DIM

    x = x_ref[0].astype(F32)
    xm = x + (_token_shift(x_ref[0], xp_ref[0], first) - x) * mu_ref[...]
    xl = xl_ref[0].astype(F32)
    xlm = xl + (_token_shift(xl_ref[0], xlp_ref[0], first) - xl) * mul_ref[...]

    r = xm[:, 0:d]
    k = xm[:, d:2 * d]
    v = xm[:, 2 * d:3 * d]
    xw = xlm[:, 0:LANES]
    xa = xlm[:, LANES:2 * LANES]
    xg = xlm[:, 2 * LANES:4 * LANES]

    y = w0_ref[...] + _dot(jnp.tanh(xw).astype(BF16), w2_ref[...])
    lw = -EXP_NEG_HALF * _sigmoid(y)
    a = _sigmoid(a0_ref[...] + _dot(xa.astype(BF16), a2_ref[...]))
    g = _dot(_sigmoid(xg).astype(BF16), g2_ref[...])

    kk = k * kk_ref[...]
    kk = kk * lax.rsqrt(jnp.maximum(_head_sum(kk * kk), 1e-24))
    k = k * (1.0 + (a - 1.0) * ka_ref[...])
    bonus = _head_sum(r * k * rk_ref[...]) * v

    r_out[0] = r.astype(r_out.dtype)
    k_out[0] = k.astype(k_out.dtype)
    v_out[0] = v.astype(v_out.dtype)
    kkn_out[0] = kk.astype(kkn_out.dtype)
    a_out[0] = a.astype(a_out.dtype)
    lw_out[0] = lw
    g_out[0] = g.astype(g_out.dtype)
    bonus_out[0] = bonus.astype(bonus_out.dtype)


def _rwkv_prep(p, mu, mul, w0, w2, a0, a2, g2, k_k, k_a, r_k, *, ts):
    b, s, _ = p.shape
    d = RWKV_DIM
    rb = ts // HIST
    row = lambda shape: pl.BlockSpec(shape, lambda bi, i: (0, 0))
    out = pl.BlockSpec((1, ts, d), lambda bi, i: (bi, i, 0))
    return pl.pallas_call(
        _rwkv_prep_kernel,
        grid=(b, s // ts),
        in_specs=[
            pl.BlockSpec((1, ts, 3 * d), lambda bi, i: (bi, i, 1)),
            pl.BlockSpec((1, ts, EV_LORA), lambda bi, i: (bi, i, 6 * d // EV_LORA)),
            pl.BlockSpec((1, HIST, 3 * d), lambda bi, i: (bi, jnp.maximum(i * rb - 1, 0), 1)),
            pl.BlockSpec((1, HIST, EV_LORA),
                         lambda bi, i: (bi, jnp.maximum(i * rb - 1, 0), 6 * d // EV_LORA)),
            row((1, 3 * d)), row((1, EV_LORA)), row((1, d)), row((LANES, d)), row((1, d)),
            row((LANES, d)), row((2 * LANES, d)), row((1, d)), row((1, d)), row((1, d)),
        ],
        out_specs=[out] * 8,
        out_shape=[jax.ShapeDtypeStruct((b, s, d), BF16)] * 5
        + [jax.ShapeDtypeStruct((b, s, d), F32)] + [jax.ShapeDtypeStruct((b, s, d), BF16)] * 2,
        compiler_params=_cparams(("parallel", "parallel")),
        name="rwkv_prep",
    )(p, p, p, p, mu, mul, w0, w2, a0, a2, g2, k_k, k_a, r_k)


WKV_LEVELS = 6
_MASK_STRICT = WKV_LEVELS
_MASK_INCL = WKV_LEVELS + 1
_MASK_EYE = WKV_LEVELS + 2


def _wkv_kernel(r_ref, k_ref, v_ref, kk_ref, a_ref, lw_ref, y_ref, t_ref, mask_ref):
    n = WKV_CHUNK
    nb, ts = y_ref.shape[0], y_ref.shape[1]
    npair = y_ref.shape[2] // LANES

    @pl.when(pl.program_id(2) == 0)
    def _():
        t_ref[...] = jnp.zeros_like(t_ref)
        row = lax.broadcasted_iota(jnp.int32, (2 * n, 2 * n), 0)
        col = lax.broadcasted_iota(jnp.int32, (2 * n, 2 * n), 1)
        same = (row // n) == (col // n)
        x = row ^ col
        level = jnp.zeros_like(x)
        for bit in range(1, WKV_LEVELS):
            level = level + jnp.where(x >= (1 << bit), 1, 0)
        level = jnp.where(same & (row > col), level, -1)
        for lev in range(WKV_LEVELS):
            mask_ref[lev] = jnp.where(level == lev, 1.0, 0.0)
        mask_ref[_MASK_STRICT] = jnp.where(level >= 0, 1.0, 0.0)
        mask_ref[_MASK_INCL] = jnp.where(same & (row >= col), 1.0, 0.0)
        mask_ref[_MASK_EYE] = jnp.where(row == col, 1.0, 0.0)

    head0 = lax.broadcasted_iota(jnp.int32, (n, LANES), 1) < RWKV_HEAD
    tri = jnp.where(lax.broadcasted_iota(jnp.int32, (n, n), 0)
                    >= lax.broadcasted_iota(jnp.int32, (n, n), 1), 1.0, 0.0).astype(BF16)

    def stack(t):
        z = jnp.zeros_like(t)
        return jnp.concatenate([jnp.where(head0, t, z), jnp.where(head0, z, t)], axis=0).astype(BF16)

    def operands(sl, bi, q, cw):
        lanes = slice(q * LANES, (q + 1) * LANES)
        r = r_ref[bi, sl, lanes].astype(F32)
        k = k_ref[bi, sl, lanes].astype(F32)
        kk = kk_ref[bi, sl, lanes].astype(F32)
        lw = lw_ref[bi, sl, lanes]
        cw_end = cw[n - 1:n, :]
        p_inv = jnp.exp(-cw)
        p_to_end = jnp.exp(cw_end - cw)
        kka = kk * a_ref[bi, sl, lanes].astype(F32)
        xar = jnp.concatenate([stack(-kk * jnp.exp(cw - lw)), stack(r * jnp.exp(cw))], axis=0)
        ybk = jnp.concatenate([stack(kka * p_inv), stack(k * p_inv)], axis=0)
        return dict(xar=xar, ybk=ybk, bt=stack(kka * p_to_end), kt=stack(k * p_to_end),
                    vs=stack(v_ref[bi, sl, lanes].astype(F32)), decay=jnp.exp(cw_end))

    def cumulative(sl, bi):
        lw = lw_ref[bi, sl, :]
        hi = lw.astype(BF16)
        r1 = lw - hi.astype(F32)
        mid = r1.astype(BF16)
        lo = (r1 - mid.astype(F32)).astype(BF16)
        return _dot(tri, hi) + _dot(tri, mid) + _dot(tri, lo)

    def chunk(c, carry):
        sl = pl.ds(pl.multiple_of(c * n, n), n)
        cw = [cumulative(sl, bi) for bi in range(nb)]
        strict = mask_ref[_MASK_STRICT] > 0.5
        incl = mask_ref[_MASK_INCL] > 0.5

        def scores(group):
            ops = [operands(sl, bi, q, cw[bi][:, q * LANES:(q + 1) * LANES]) for bi, q in group]
            sc = [_dot_nt(o["xar"], o["ybk"]) for o in ops]
            for o, s in zip(ops, sc):
                o["a_ab"] = jnp.where(strict, s[0:2 * n, 0:2 * n], 0.0)
                o["a_k"] = jnp.concatenate([jnp.where(strict, s[0:2 * n, 2 * n:4 * n], 0.0),
                                            jnp.where(incl, s[2 * n:4 * n, 2 * n:4 * n], 0.0)],
                                           axis=0).astype(BF16)
                o["a_rb"] = jnp.where(incl, s[2 * n:4 * n, 0:2 * n], 0.0).astype(BF16)
            return ops

        def inverse(ops):
            m = [mask_ref[_MASK_EYE] + o["a_ab"] * mask_ref[0] for o in ops]
            for lev in range(1, WKV_LEVELS):
                mb = [x.astype(BF16) for x in m]
                em = [_dot((o["a_ab"] * mask_ref[lev]).astype(BF16), x).astype(BF16)
                      for o, x in zip(ops, mb)]
                m = [x + _dot(xb, y) for x, xb, y in zip(m, mb, em)]
            return m

        def finish(group, ops, m):
            t = [t_ref[bi * npair + q] for bi, q in group]
            xtv = [_dot(jnp.concatenate([o["xar"], o["a_k"]], axis=1),
                        jnp.concatenate([x.astype(BF16), o["vs"]], axis=0)) for o, x in zip(ops, t)]
            u = [_dot(x.astype(BF16), y[0:2 * n].astype(BF16)).astype(BF16) for x, y in zip(m, xtv)]
            for i, (bi, q) in enumerate(group):
                o = ops[i]
                ys = xtv[i][2 * n:4 * n] + _dot(o["a_rb"], u[i])
                y_ref[bi, sl, q * LANES:(q + 1) * LANES] = ys[0:n] + ys[n:2 * n]
                t_ref[bi * npair + q] = \
                    t[i] * jnp.transpose(jnp.broadcast_to(o["decay"], (LANES, LANES))) \
                    + _dot_tn(jnp.concatenate([o["bt"], o["kt"]], axis=0),
                              jnp.concatenate([u[i], o["vs"]], axis=0))

        group = [(bi, q) for bi in range(nb) for q in range(npair)]
        ops = scores(group)
        finish(group, ops, inverse(ops))
        return carry

    lax.fori_loop(0, ts // n, chunk, 0)


def _wkv(r, k, v, kk, a, lw, *, ts, npair, nb):
    b, s, d = r.shape
    wd = npair * LANES
    spec = pl.BlockSpec((nb, ts, wd), lambda bi, p, i: (bi, i, p))
    return pl.pallas_call(
        _wkv_kernel,
        grid=(b // nb, d // wd, s // ts),
        in_specs=[spec] * 6,
        out_specs=spec,
        out_shape=jax.ShapeDtypeStruct((b, s, d), F32),
        scratch_shapes=[pltpu.VMEM((nb * npair, LANES, LANES), F32),
                        pltpu.VMEM((WKV_LEVELS + 3, LANES, LANES), F32)],
        compiler_params=_cparams(("parallel", "parallel", "arbitrary")),
        name="wkv7",
    )(r, k, v, kk, a, lw)


def _rwkv_post_kernel(y_ref, bonus_ref, g_ref, lnw_ref, lnb_ref, o_ref):
    y = y_ref[...]
    inv_n = 1.0 / RWKV_HEAD
    mean = _head_sum(y) * inv_n
    yc = y - mean
    var = _head_sum(yc * yc) * inv_n
    yn = yc * lax.rsqrt(var + GN_EPS) * lnw_ref[...] + lnb_ref[...]
    o_ref[...] = ((yn + bonus_ref[...].astype(F32)) * g_ref[...].astype(F32)).astype(o_ref.dtype)


def _rwkv_post(y, bonus, g, ln_w, ln_b, *, tm):
    m, d = y.shape
    blk = pl.BlockSpec((tm, d), lambda i: (i, 0))
    row = pl.BlockSpec((1, d), lambda i: (0, 0))
    return pl.pallas_call(
        _rwkv_post_kernel,
        grid=(m // tm,),
        in_specs=[blk, blk, blk, row, row],
        out_specs=blk,
        out_shape=jax.ShapeDtypeStruct((m, d), BF16),
        compiler_params=_cparams(("parallel",)),
        name="rwkv_post",
    )(y, bonus, g, ln_w, ln_b)


def _pool_kernel(u_ref, up_ref, w_ref, sc_ref, o_ref, buf_ref):
    ts = o_ref.shape[1]
    hist = max(POOL_WINDOWS)
    i = pl.program_id(1)
    u = u_ref[0].astype(F32)
    _fill_history(buf_ref, up_ref[0].astype(F32), u, hist, i == 0)
    pos = i * ts + lax.broadcasted_iota(jnp.int32, (ts, 1), 0)
    outs = []
    for gi, win in enumerate(POOL_WINDOWS):
        cols = pl.ds(gi * POOL_GROUP, POOL_GROUP)
        acc = buf_ref[pl.ds(hist, ts), cols]
        for j in range(1, win):
            acc = acc + buf_ref[pl.ds(hist - j, ts), cols]
        count = jnp.minimum(pos + 1, win).astype(F32)
        dg = acc / count - u[:, gi * POOL_GROUP:(gi + 1) * POOL_GROUP]
        outs.append(_dot(dg.astype(BF16), w_ref[gi]))
    y = jnp.concatenate(outs, axis=1) * sc_ref[...]
    o_ref[0] = y.astype(o_ref.dtype)


def _pool_mixer(p, pool_w, pool_scale, *, ts):
    b, s, _ = p.shape
    hist = max(POOL_WINDOWS)
    rb = ts // hist
    return pl.pallas_call(
        _pool_kernel,
        grid=(b, s // ts),
        in_specs=[
            pl.BlockSpec((1, ts, POOL_DIM), lambda bi, i: (bi, i, 0)),
            pl.BlockSpec((1, hist, POOL_DIM), lambda bi, i: (bi, jnp.maximum(i * rb - 1, 0), 0)),
            pl.BlockSpec((len(POOL_WINDOWS), POOL_GROUP, POOL_GROUP), lambda bi, i: (0, 0, 0)),
            pl.BlockSpec((1, POOL_DIM), lambda bi, i: (0, 0)),
        ],
        out_specs=pl.BlockSpec((1, ts, POOL_DIM), lambda bi, i: (bi, i, 0)),
        out_shape=jax.ShapeDtypeStruct((b, s, POOL_DIM), BF16),
        scratch_shapes=[pltpu.VMEM((ts + hist, POOL_DIM), F32)],
        compiler_params=_cparams(("parallel", "parallel")),
        name="pool_mixer",
    )(p, p, pool_w, pool_scale)


def _mla_prep_kernel(ql_ref, kvl_ref, kpe_ref, kper_ref, ang_ref, qn_ref, kvn_ref,
                     wq_ref, wqr_ref, wk_ref, wv_ref, q_out, k_out, v_out):
    cos = jnp.cos(ang_ref[...])
    sin = jnp.sin(ang_ref[...])
    qn = _rms(ql_ref[...].astype(F32), qn_ref[...]).astype(BF16)
    kvn = _rms(kvl_ref[...].astype(F32), kvn_ref[...]).astype(BF16)
    scale = (QK_NOPE + QK_ROPE) ** -0.5 * LOG2_E
    qa = _dot(qn, wq_ref[...])
    qr = _dot(qn, wqr_ref[...])
    ka = _dot(kvn, wk_ref[...])
    kpe = kpe_ref[...].astype(F32) * cos + kper_ref[...].astype(F32) * sin
    for h in range(MLA_HEADS):
        lo = h * QK_PAD
        q_out[:, lo:lo + LANES] = (qa[:, lo:lo + LANES] * scale).astype(BF16)
        q_pe = qa[:, lo + LANES:lo + 2 * LANES] * cos + qr[:, h * LANES:(h + 1) * LANES] * sin
        q_out[:, lo + LANES:lo + 2 * LANES] = (q_pe * scale).astype(BF16)
        k_out[:, lo:lo + LANES] = ka[:, h * QK_NOPE:(h + 1) * QK_NOPE].astype(BF16)
        k_out[:, lo + LANES:lo + 2 * LANES] = kpe.astype(BF16)
    vt = _dot_nt(wv_ref[...], kvn).astype(BF16)
    ones = jnp.ones((ATTN_SUM_ROWS, vt.shape[1]), BF16)
    for h in range(MLA_HEADS):
        v_out[h * VT_ROWS:h * VT_ROWS + V_HEAD, :] = vt[h * V_HEAD:(h + 1) * V_HEAD, :]
        v_out[h * VT_ROWS + V_HEAD:(h + 1) * VT_ROWS, :] = ones


def _mla_prep(p, ang, q_norm, kv_norm, wq, wqr, wk, wv, layer, *, tm):
    m = p.shape[0]
    hq = MLA_HEADS * QK_PAD
    hv = MLA_HEADS * VT_ROWS
    row = lambda a: pl.BlockSpec(a.shape, lambda i: (0, 0))
    full = lambda a: pl.BlockSpec((None,) + a.shape[1:], lambda i: (layer, 0, 0))
    return pl.pallas_call(
        _mla_prep_kernel,
        grid=(m // tm,),
        in_specs=[
            pl.BlockSpec((tm, Q_LORA), lambda i: (i, 1)),
            pl.BlockSpec((tm, KV_LORA), lambda i: (i, 2)),
            pl.BlockSpec((tm, LANES), lambda i: (i, 12)),
            pl.BlockSpec((tm, LANES), lambda i: (i, 13)),
            pl.BlockSpec((tm, LANES), lambda i: (i, 0)),
            row(q_norm), row(kv_norm), full(wq), full(wqr), full(wk), full(wv),
        ],
        out_specs=[pl.BlockSpec((tm, hq), lambda i: (i, 0)),
                   pl.BlockSpec((tm, hq), lambda i: (i, 0)),
                   pl.BlockSpec((hv, tm), lambda i: (0, i))],
        out_shape=[jax.ShapeDtypeStruct((m, hq), BF16),
                   jax.ShapeDtypeStruct((m, hq), BF16),
                   jax.ShapeDtypeStruct((hv, m), BF16)],
        compiler_params=_cparams(("parallel",)),
        name="mla_prep",
    )(p, p, p, p, ang, q_norm, kv_norm, wq, wqr, wk, wv)


def _attn_kernel(q_ref, k_ref, vt_ref, o_ref, *, tq, tk, nh):
    qi = pl.program_id(2)
    heads = range(nh)
    qs = [q_ref[0, :, h * QK_PAD:(h + 1) * QK_PAD] for h in heads]
    diag = tq // tk

    def block(j, carry, masked):
        m, acc = carry
        sl = pl.ds(pl.multiple_of(j * tk, tk), tk)
        s = [_dot_nt(k_ref[0, sl, h * QK_PAD:(h + 1) * QK_PAD], qs[h]) for h in heads]
        if masked:
            key = j * tk + lax.broadcasted_iota(jnp.int32, (tk, tq), 0)
            qry = qi * tq + lax.broadcasted_iota(jnp.int32, (tk, tq), 1)
            s = [jnp.where(key <= qry, x, -1e30) for x in s]
        m_new = [jnp.maximum(a, jnp.max(x, axis=0, keepdims=True)) for a, x in zip(m, s)]
        alpha = [jnp.exp2(a - b) for a, b in zip(m, m_new)]
        p = [jnp.exp2(x - a).astype(BF16) for x, a in zip(s, m_new)]
        pv = [_dot(vt_ref[h * VT_ROWS:(h + 1) * VT_ROWS, sl], p[h]) for h in heads]
        acc = [a * b + x for a, b, x in zip(alpha, acc, pv)]
        return m_new, acc

    init = ([jnp.full((1, tq), -1e30, F32) for _ in heads],
            [jnp.zeros((V_HEAD + ATTN_SUM_ROWS, tq), F32) for _ in heads])
    carry = lax.fori_loop(0, qi * diag, lambda j, c: block(j, c, False), init)
    for d in range(diag):
        carry = block(qi * diag + d, carry, True)
    _, acc = carry
    for h in heads:
        out = acc[h][0:V_HEAD] / acc[h][V_HEAD:V_HEAD + 1]
        o_ref[0, :, h * V_HEAD:(h + 1) * V_HEAD] = jnp.transpose(out).astype(o_ref.dtype)


def _attention(q, k, vt, *, tq, tk, nh):
    b, s, _ = q.shape
    return pl.pallas_call(
        functools.partial(_attn_kernel, tq=tq, tk=tk, nh=nh),
        grid=(b, MLA_HEADS // nh, s // tq),
        in_specs=[
            pl.BlockSpec((1, tq, nh * QK_PAD), lambda bi, h, i: (bi, i, h)),
            pl.BlockSpec((1, s, nh * QK_PAD), lambda bi, h, i: (bi, 0, h)),
            pl.BlockSpec((nh * VT_ROWS, s), lambda bi, h, i: (h, bi)),
        ],
        out_specs=pl.BlockSpec((1, tq, nh * V_HEAD), lambda bi, h, i: (bi, i, h)),
        out_shape=jax.ShapeDtypeStruct((b, s, MLA_HEADS * V_HEAD), BF16),
        compiler_params=_cparams(("parallel", "parallel", "arbitrary")),
        name="mla_attention",
    )(q, k, vt)


def _pad_last(w, n):
    return jnp.pad(w, [(0, 0)] * (w.ndim - 1) + [(0, n - w.shape[-1])])


def _pad_rows(w, n):
    return jnp.pad(w, ((0, n - w.shape[0]), (0, 0)))


def _rot_cols(w):
    half = w.shape[-1] // 2
    return jnp.concatenate([-w[..., half:], w[..., :half]], axis=-1)


def _even_weights(w_in, mu):
    c = 3 * CONV_DIM + 3 * RWKV_DIM
    split = lambda t, base: (t[..., base:base + W_LORA],
                             t[..., base + W_LORA:base + W_LORA + A_LORA],
                             t[..., base + W_LORA + A_LORA:])
    xw, xa, xg = split(w_in, c)
    w = jnp.concatenate([w_in[..., :c], _pad_last(xw, LANES), _pad_last(xa, LANES),
                         _pad_last(xg, 2 * LANES)], axis=-1).astype(BF16)
    r = 3 * RWKV_DIM
    mw, ma, mg = split(mu, r)
    mu_l = jnp.concatenate([_pad_last(mw, LANES), _pad_last(ma, LANES), _pad_last(mg, 2 * LANES)],
                           axis=-1)
    return w, mu[:, None, :r], mu_l[:, None, :]


def _odd_weights(w_in, w_uq, w_ukv):
    n = w_in.shape[0]
    o3 = POOL_DIM + Q_LORA + KV_LORA
    kpe = w_in[..., o3:]
    w = jnp.concatenate([w_in[..., :o3], _pad_last(kpe, LANES), _pad_last(_rot_cols(kpe), LANES)],
                        axis=-1).astype(BF16)
    uq = w_uq.reshape(n, Q_LORA, MLA_HEADS, QK_NOPE + QK_ROPE)
    wq = _pad_last(uq, QK_PAD).reshape(n, Q_LORA, MLA_HEADS * QK_PAD).astype(BF16)
    wqr = _pad_last(_rot_cols(uq[..., QK_NOPE:]), LANES)
    wqr = wqr.reshape(n, Q_LORA, MLA_HEADS * LANES).astype(BF16)
    ukv = w_ukv.reshape(n, KV_LORA, MLA_HEADS, QK_NOPE + V_HEAD)
    wk = ukv[..., :QK_NOPE].reshape(n, KV_LORA, MLA_HEADS * QK_NOPE).astype(BF16)
    wv = ukv[..., QK_NOPE:].reshape(n, KV_LORA, MLA_HEADS * V_HEAD)
    return w, wq, wqr, wk, jnp.swapaxes(wv, 1, 2).astype(BF16)


def _tile(n, pref):
    return pref if n % pref == 0 else n


def kernel(x, positions, ev_norm, ev_w_in, ev_conv_w, ev_mu, ev_w0, ev_w2, ev_a0, ev_a2, ev_g2, ev_k_k, ev_k_a, ev_r_k, ev_ln_w, ev_ln_b, ev_w_out, od_norm, od_w_in, od_pool_w, od_pool_scale, od_q_norm, od_w_uq, od_kv_norm, od_w_ukv, od_w_out, ffn_norm, ffn_w_gate, ffn_w_up, ffn_w_down, final_norm):
    b, s, d = x.shape
    m = b * s
    depth = ffn_norm.shape[0]
    tm = _tile(m, 512)
    tm_in = _tile(m, 1024)
    ts = _tile(s, 512)
    ts_prep = _tile(s, 256)

    inv = 1.0 / (ROPE_THETA ** (jnp.arange(0, QK_ROPE, 2, dtype=F32) / QK_ROPE))
    inv = jnp.concatenate([inv, inv, jnp.zeros((LANES - QK_ROPE,), F32)])
    ang = (positions.astype(F32)[..., None] * inv).reshape(m, LANES)

    ev_w, ev_mu_main, ev_mu_lora = _even_weights(ev_w_in, ev_mu)
    od_w, wq, wqr, wk, wv = _odd_weights(od_w_in, od_w_uq, od_w_ukv)
    ev_wo = ev_w_out.astype(BF16)
    od_wo = od_w_out.astype(BF16)
    wg, wu, wd = ffn_w_gate.astype(BF16), ffn_w_up.astype(BF16), ffn_w_down.astype(BF16)
    row = lambda t: t[None, :]

    h = x.reshape(m, d)
    for layer in range(depth):
        j = layer // 2
        if layer % 2 == 0:
            p = _norm_matmul(h, row(ev_norm[j]), ev_w, j, tm=tm_in, tn=EV_COLS // 4, out_dtype=BF16)
            p = p.reshape(b, s, EV_COLS)
            ya = _conv_mixer(p, ev_conv_w[j].T, ts=ts, tc=512)
            r, k, v, kk, a, lw, g, bonus = _rwkv_prep(
                p, ev_mu_main[j], ev_mu_lora[j], row(ev_w0[j]),
                _pad_rows(ev_w2[j], LANES).astype(BF16), row(ev_a0[j]),
                _pad_rows(ev_a2[j], LANES).astype(BF16), _pad_rows(ev_g2[j], 2 * LANES).astype(BF16),
                row(ev_k_k[j]), row(ev_k_a[j]), row(ev_r_k[j].reshape(-1)), ts=ts_prep)
            y = _wkv(r, k, v, kk, a, lw, ts=_tile(s, 256), npair=8, nb=2 if b % 2 == 0 else 1)
            yb = _rwkv_post(y.reshape(m, -1), bonus.reshape(m, -1), g.reshape(m, -1),
                            row(ev_ln_w[j]), row(ev_ln_b[j]), tm=tm)
            h = _proj_residual(ya.reshape(m, -1), yb, ev_wo, j, h, tm=tm, tn=d)
        else:
            p = _norm_matmul(h, row(od_norm[j]), od_w, j, tm=tm_in, tn=OD_COLS // 2, out_dtype=BF16)
            yc = _pool_mixer(p.reshape(b, s, OD_COLS), od_pool_w[j].astype(BF16),
                             row(od_pool_scale[j]), ts=ts)
            q, kq, vv = _mla_prep(p, ang, row(od_q_norm[j]), row(od_kv_norm[j]),
                                  wq, wqr, wk, wv, j, tm=_tile(m, 256))
            hq = MLA_HEADS * QK_PAD
            yd = _attention(q.reshape(b, s, hq), kq.reshape(b, s, hq), vv, tq=_tile(s, 512),
                            tk=_tile(s, 512), nh=4)
            h = _proj_residual(yc.reshape(m, -1), yd.reshape(m, -1), od_wo, j, h, tm=tm, tn=d)
        h = _ffn(h, row(ffn_norm[layer]), row(final_norm), wg, wu, wd, layer, tm=tm_in, tf=512,
                 norm_out=layer == depth - 1)
    return h.reshape(b, s, d)
```

```python
import functools

import jax
import jax.numpy as jnp
from jax import lax
from jax.experimental import pallas as pl
from jax.experimental.pallas import tpu as pltpu

F32 = jnp.float32
BF16 = jnp.bfloat16

D_MODEL = 2048
NORM_EPS = 1e-6
CONV_DIM = 1024
RWKV_DIM = 1024
RWKV_HEAD = 64
W_LORA = 64
A_LORA = 64
G_LORA = 160
GN_EPS = 64e-5
POOL_WINDOWS = (2, 4, 8, 16)
POOL_GROUP = 128
POOL_DIM = 512
MLA_HEADS = 12
Q_LORA = 512
KV_LORA = 512
QK_NOPE = 128
QK_ROPE = 64
V_HEAD = 128
ROPE_THETA = 10000.0
D_FF = 5632
LOG2_E = 1.4426950408889634
EXP_NEG_HALF = 0.6065306597126334

LANES = 128
HIST = 16
VMEM_LIMIT = 60 * 1024 * 1024
QK_PAD = 256
ATTN_SUM_ROWS = 16
VT_ROWS = V_HEAD + ATTN_SUM_ROWS
WKV_CHUNK = 64
EV_COLS = 6656
EV_LORA = 512
OD_COLS = 1792


def _cparams(sem):
    return pltpu.CompilerParams(dimension_semantics=sem, vmem_limit_bytes=VMEM_LIMIT)


def _rms(x, g):
    ms = jnp.mean(x * x, axis=-1, keepdims=True)
    return x * lax.rsqrt(ms + NORM_EPS) * g


def _sigmoid(x):
    return 1.0 / (1.0 + jnp.exp(-x))


def _dot(a, b):
    return jnp.dot(a, b, preferred_element_type=F32)


def _dot_nt(a, b):
    return lax.dot_general(a, b, (((1,), (1,)), ((), ())), preferred_element_type=F32)


def _dot_tn(a, b):
    return lax.dot_general(a, b, (((0,), (0,)), ((), ())), preferred_element_type=F32)


def _norm_mm_kernel(x_ref, g_ref, w_ref, o_ref, xn_ref):
    @pl.when(pl.program_id(1) == 0)
    def _():
        xn_ref[...] = _rms(x_ref[...], g_ref[...]).astype(BF16)

    o_ref[...] = _dot(xn_ref[...], w_ref[...]).astype(o_ref.dtype)


def _norm_matmul(x, g, w, layer, *, tm, tn, out_dtype):
    m, k = x.shape
    n = w.shape[2]
    return pl.pallas_call(
        _norm_mm_kernel,
        grid=(m // tm, n // tn),
        in_specs=[
            pl.BlockSpec((tm, k), lambda i, j: (i, 0)),
            pl.BlockSpec((1, k), lambda i, j: (0, 0)),
            pl.BlockSpec((None, k, tn), lambda i, j: (layer, 0, j)),
        ],
        out_specs=pl.BlockSpec((tm, tn), lambda i, j: (i, j)),
        out_shape=jax.ShapeDtypeStruct((m, n), out_dtype),
        scratch_shapes=[pltpu.VMEM((tm, k), BF16)],
        compiler_params=_cparams(("parallel", "arbitrary")),
        name="norm_matmul",
    )(x, g, w)


def _proj_res_kernel(a_ref, b_ref, w_ref, h_ref, o_ref):
    ka = a_ref.shape[1]
    o_ref[...] = h_ref[...] + _dot(a_ref[...], w_ref[0:ka, :]) + _dot(b_ref[...], w_ref[ka:, :])


def _proj_residual(a, b, w, layer, h, *, tm, tn):
    m, n = h.shape
    ka, kb = a.shape[1], b.shape[1]
    return pl.pallas_call(
        _proj_res_kernel,
        grid=(m // tm, n // tn),
        in_specs=[
            pl.BlockSpec((tm, ka), lambda i, j: (i, 0)),
            pl.BlockSpec((tm, kb), lambda i, j: (i, 0)),
            pl.BlockSpec((None, ka + kb, tn), lambda i, j: (layer, 0, j)),
            pl.BlockSpec((tm, tn), lambda i, j: (i, j)),
        ],
        out_specs=pl.BlockSpec((tm, tn), lambda i, j: (i, j)),
        out_shape=jax.ShapeDtypeStruct((m, n), F32),
        compiler_params=_cparams(("parallel", "arbitrary")),
        name="proj_residual",
    )(a, b, w, h)


def _ffn_kernel(x_ref, g_ref, gout_ref, wg_ref, wu_ref, wd_ref, o_ref, xn_ref, *, norm_out):
    @pl.when(pl.program_id(1) == 0)
    def _():
        x = x_ref[...]
        xn_ref[...] = _rms(x, g_ref[...]).astype(BF16)
        o_ref[...] = x

    xn = xn_ref[...]
    a = _dot(xn, wg_ref[...])
    b = _dot(xn, wu_ref[...])
    act = (a * _sigmoid(a) * b).astype(BF16)
    o_ref[...] += _dot(act, wd_ref[...])

    if norm_out:
        @pl.when(pl.program_id(1) == pl.num_programs(1) - 1)
        def _():
            o_ref[...] = _rms(o_ref[...], gout_ref[...])


def _ffn(x, g, g_out, wg, wu, wd, layer, *, tm, tf, norm_out):
    m, d = x.shape
    f = wg.shape[2]
    return pl.pallas_call(
        functools.partial(_ffn_kernel, norm_out=norm_out),
        grid=(m // tm, f // tf),
        in_specs=[
            pl.BlockSpec((tm, d), lambda i, j: (i, 0)),
            pl.BlockSpec((1, d), lambda i, j: (0, 0)),
            pl.BlockSpec((1, d), lambda i, j: (0, 0)),
            pl.BlockSpec((None, d, tf), lambda i, j: (layer, 0, j)),
            pl.BlockSpec((None, d, tf), lambda i, j: (layer, 0, j)),
            pl.BlockSpec((None, tf, d), lambda i, j: (layer, j, 0)),
        ],
        out_specs=pl.BlockSpec((tm, d), lambda i, j: (i, 0)),
        out_shape=jax.ShapeDtypeStruct((m, d), F32),
        scratch_shapes=[pltpu.VMEM((tm, d), BF16)],
        compiler_params=_cparams(("parallel", "arbitrary")),
        name="ffn",
    )(x, g, g_out, wg, wu, wd)


def _fill_history(buf_ref, prev, cur, hist, is_first):
    buf_ref[pl.ds(0, hist), :] = jnp.where(is_first, 0.0, prev)
    buf_ref[pl.ds(hist, cur.shape[0]), :] = cur


def _conv_kernel(bg_ref, cg_ref, hh_ref, cgp_ref, hhp_ref, w_ref, o_ref, buf_ref):
    ts = o_ref.shape[1]
    u = cg_ref[0].astype(F32) * hh_ref[0].astype(F32)
    up = cgp_ref[0].astype(F32) * hhp_ref[0].astype(F32)
    _fill_history(buf_ref, up, u, HIST, pl.program_id(1) == 0)
    u1 = buf_ref[pl.ds(HIST - 1, ts), :]
    u2 = buf_ref[pl.ds(HIST - 2, ts), :]
    w = w_ref[...]
    y = w[2:3] * u + w[1:2] * u1 + w[0:1] * u2
    o_ref[0] = (bg_ref[0].astype(F32) * y).astype(o_ref.dtype)


def _conv_mixer(p, conv_w_t, *, ts, tc):
    b, s, _ = p.shape
    nc = CONV_DIM // tc
    rb = ts // HIST

    def cur(off):
        return pl.BlockSpec((1, ts, tc), lambda bi, i, c: (bi, i, off * nc + c))

    def prev(off):
        return pl.BlockSpec((1, HIST, tc),
                            lambda bi, i, c: (bi, jnp.maximum(i * rb - 1, 0), off * nc + c))

    return pl.pallas_call(
        _conv_kernel,
        grid=(b, s // ts, nc),
        in_specs=[cur(0), cur(1), cur(2), prev(1), prev(2),
                  pl.BlockSpec((3, tc), lambda bi, i, c: (0, c))],
        out_specs=pl.BlockSpec((1, ts, tc), lambda bi, i, c: (bi, i, c)),
        out_shape=jax.ShapeDtypeStruct((b, s, CONV_DIM), BF16),
        scratch_shapes=[pltpu.VMEM((ts + HIST, tc), F32)],
        compiler_params=_cparams(("parallel", "parallel", "parallel")),
        name="conv_mixer",
    )(p, p, p, p, p, conv_w_t)


def _head_sum(x, split=True):
    r = lax.broadcasted_iota(jnp.int32, (LANES, LANES), 0) // RWKV_HEAD
    c = lax.broadcasted_iota(jnp.int32, (LANES, LANES), 1) // RWKV_HEAD
    ones = jnp.where(r == c, 1.0, 0.0).astype(BF16)
    hi = x.astype(BF16)
    lo = (x - hi.astype(F32)).astype(BF16) if split else None
    parts = []
    for g in range(x.shape[1] // LANES):
        sl = slice(g * LANES, (g + 1) * LANES)
        part = _dot(hi[:, sl], ones)
        parts.append(part + _dot(lo[:, sl], ones) if split else part)
    return jnp.concatenate(parts, axis=1)


def _token_shift(xb, prev_rows, is_first):
    ts = xb.shape[0]
    r = lax.broadcasted_iota(jnp.int32, (ts, ts), 0)
    c = lax.broadcasted_iota(jnp.int32, (ts, ts), 1)
    xs = _dot(jnp.where(r == c + 1, 1.0, 0.0).astype(BF16), xb)
    hist = prev_rows.shape[0]
    last = jnp.where(is_first, 0.0, prev_rows[hist - 1:hist, :].astype(F32))
    top = jnp.where(lax.broadcasted_iota(jnp.int32, (HIST, 1), 0) == 0, last, xs[0:HIST])
    return jnp.concatenate([top, xs[HIST:]], axis=0)


def _rwkv_prep_kernel(x_ref, xl_ref, xp_ref, xlp_ref, mu_ref, mul_ref, w0_ref, w2_ref, a0_ref,
                      a2_ref, g2_ref, kk_ref, ka_ref, rk_ref,
                      r_out, k_out, v_out, kkn_out, a_out, lw_out, g_out, bonus_out):
    first = pl.program_id(1) == 0
    d = RWKV_DIM

    x = x_ref[0].astype(F32)
    xm = x + (_token_shift(x_ref[0], xp_ref[0], first) - x) * mu_ref[...]
    xl = xl_ref[0].astype(F32)
    xlm = xl + (_token_shift(xl_ref[0], xlp_ref[0], first) - xl) * mul_ref[...]

    r = xm[:, 0:d]
    k = xm[:, d:2 * d]
    v = xm[:, 2 * d:3 * d]
    xw = xlm[:, 0:LANES]
    xa = xlm[:, LANES:2 * LANES]
    xg = xlm[:, 2 * LANES:4 * LANES]

    y = w0_ref[...] + _dot(jnp.tanh(xw).astype(BF16), w2_ref[...])
    lw = -EXP_NEG_HALF * _sigmoid(y)
    a = _sigmoid(a0_ref[...] + _dot(xa.astype(BF16), a2_ref[...]))
    g = _dot(_sigmoid(xg).astype(BF16), g2_ref[...])

    kk = k * kk_ref[...]
    kk = kk * lax.rsqrt(jnp.maximum(_head_sum(kk * kk, split=False), 1e-24))
    k = k * (1.0 + (a - 1.0) * ka_ref[...])
    bonus = _head_sum(r * k * rk_ref[...], split=False) * v

    r_out[0] = r.astype(r_out.dtype)
    k_out[0] = k.astype(k_out.dtype)
    v_out[0] = v.astype(v_out.dtype)
    kkn_out[0] = kk.astype(kkn_out.dtype)
    a_out[0] = a.astype(a_out.dtype)
    lw_out[0] = lw
    g_out[0] = g.astype(g_out.dtype)
    bonus_out[0] = bonus.astype(bonus_out.dtype)


def _rwkv_prep(p, mu, mul, w0, w2, a0, a2, g2, k_k, k_a, r_k, *, ts):
    b, s, _ = p.shape
    d = RWKV_DIM
    rb = ts // HIST
    row = lambda shape: pl.BlockSpec(shape, lambda bi, i: (0, 0))
    out = pl.BlockSpec((1, ts, d), lambda bi, i: (bi, i, 0))
    return pl.pallas_call(
        _rwkv_prep_kernel,
        grid=(b, s // ts),
        in_specs=[
            pl.BlockSpec((1, ts, 3 * d), lambda bi, i: (bi, i, 1)),
            pl.BlockSpec((1, ts, EV_LORA), lambda bi, i: (bi, i, 6 * d // EV_LORA)),
            pl.BlockSpec((1, HIST, 3 * d), lambda bi, i: (bi, jnp.maximum(i * rb - 1, 0), 1)),
            pl.BlockSpec((1, HIST, EV_LORA),
                         lambda bi, i: (bi, jnp.maximum(i * rb - 1, 0), 6 * d // EV_LORA)),
            row((1, 3 * d)), row((1, EV_LORA)), row((1, d)), row((LANES, d)), row((1, d)),
            row((LANES, d)), row((2 * LANES, d)), row((1, d)), row((1, d)), row((1, d)),
        ],
        out_specs=[out] * 8,
        out_shape=[jax.ShapeDtypeStruct((b, s, d), BF16)] * 5
        + [jax.ShapeDtypeStruct((b, s, d), F32)] + [jax.ShapeDtypeStruct((b, s, d), BF16)] * 2,
        compiler_params=_cparams(("parallel", "parallel")),
        name="rwkv_prep",
    )(p, p, p, p, mu, mul, w0, w2, a0, a2, g2, k_k, k_a, r_k)


WKV_LEVELS = 6
_MASK_STRICT = WKV_LEVELS
_MASK_INCL = WKV_LEVELS + 1
_MASK_EYE = WKV_LEVELS + 2


def _wkv_kernel(r_ref, k_ref, v_ref, kk_ref, a_ref, lw_ref, y_ref, t_ref, mask_ref):
    n = WKV_CHUNK
    nb, ts = y_ref.shape[0], y_ref.shape[1]
    npair = y_ref.shape[2] // LANES

    @pl.when(pl.program_id(2) == 0)
    def _():
        t_ref[...] = jnp.zeros_like(t_ref)
        row = lax.broadcasted_iota(jnp.int32, (2 * n, 2 * n), 0)
        col = lax.broadcasted_iota(jnp.int32, (2 * n, 2 * n), 1)
        same = (row // n) == (col // n)
        x = row ^ col
        level = jnp.zeros_like(x)
        for bit in range(1, WKV_LEVELS):
            level = level + jnp.where(x >= (1 << bit), 1, 0)
        level = jnp.where(same & (row > col), level, -1)
        for lev in range(WKV_LEVELS):
            mask_ref[lev] = jnp.where(level == lev, 1.0, 0.0)
        mask_ref[_MASK_STRICT] = jnp.where(level >= 0, 1.0, 0.0)
        mask_ref[_MASK_INCL] = jnp.where(same & (row >= col), 1.0, 0.0)
        mask_ref[_MASK_EYE] = jnp.where(row == col, 1.0, 0.0)

    head0 = lax.broadcasted_iota(jnp.int32, (n, LANES), 1) < RWKV_HEAD
    tri = jnp.where(lax.broadcasted_iota(jnp.int32, (n, n), 0)
                    >= lax.broadcasted_iota(jnp.int32, (n, n), 1), 1.0, 0.0).astype(BF16)

    def stack(t):
        z = jnp.zeros_like(t)
        return jnp.concatenate([jnp.where(head0, t, z), jnp.where(head0, z, t)], axis=0).astype(BF16)

    def operands(sl, bi, q, cw):
        lanes = slice(q * LANES, (q + 1) * LANES)
        r = r_ref[bi, sl, lanes].astype(F32)
        k = k_ref[bi, sl, lanes].astype(F32)
        kk = kk_ref[bi, sl, lanes].astype(F32)
        lw = lw_ref[bi, sl, lanes]
        cw_end = cw[n - 1:n, :]
        p_inv = jnp.exp(-cw)
        p_to_end = jnp.exp(cw_end - cw)
        kka = kk * a_ref[bi, sl, lanes].astype(F32)
        xar = jnp.concatenate([stack(-kk * jnp.exp(cw - lw)), stack(r * jnp.exp(cw))], axis=0)
        ybk = jnp.concatenate([stack(kka * p_inv), stack(k * p_inv)], axis=0)
        return dict(xar=xar, ybk=ybk, bt=stack(kka * p_to_end), kt=stack(k * p_to_end),
                    vs=stack(v_ref[bi, sl, lanes].astype(F32)), decay=jnp.exp(cw_end))

    def cumulative(sl, bi):
        lw = lw_ref[bi, sl, :]
        hi = lw.astype(BF16)
        r1 = lw - hi.astype(F32)
        mid = r1.astype(BF16)
        lo = (r1 - mid.astype(F32)).astype(BF16)
        return _dot(tri, hi) + _dot(tri, mid) + _dot(tri, lo)

    def chunk(c, carry):
        sl = pl.ds(pl.multiple_of(c * n, n), n)
        cw = [cumulative(sl, bi) for bi in range(nb)]
        strict = mask_ref[_MASK_STRICT] > 0.5
        incl = mask_ref[_MASK_INCL] > 0.5

        def scores(group):
            ops = [operands(sl, bi, q, cw[bi][:, q * LANES:(q + 1) * LANES]) for bi, q in group]
            sc = [_dot_nt(o["xar"], o["ybk"]) for o in ops]
            for o, s in zip(ops, sc):
                o["a_ab"] = jnp.where(strict, s[0:2 * n, 0:2 * n], 0.0)
                o["a_k"] = jnp.concatenate([jnp.where(strict, s[0:2 * n, 2 * n:4 * n], 0.0),
                                            jnp.where(incl, s[2 * n:4 * n, 2 * n:4 * n], 0.0)],
                                           axis=0).astype(BF16)
                o["a_rb"] = jnp.where(incl, s[2 * n:4 * n, 0:2 * n], 0.0).astype(BF16)
            return ops

        def inverse(ops):
            m = [mask_ref[_MASK_EYE] + o["a_ab"] * mask_ref[0] for o in ops]
            for lev in range(1, WKV_LEVELS):
                mb = [x.astype(BF16) for x in m]
                em = [_dot((o["a_ab"] * mask_ref[lev]).astype(BF16), x).astype(BF16)
                      for o, x in zip(ops, mb)]
                m = [x + _dot(xb, y) for x, xb, y in zip(m, mb, em)]
            return m

        def finish(group, ops, m):
            t = [t_ref[bi * npair + q] for bi, q in group]
            xtv = [_dot(jnp.concatenate([o["xar"], o["a_k"]], axis=1),
                        jnp.concatenate([x.astype(BF16), o["vs"]], axis=0)) for o, x in zip(ops, t)]
            u = [_dot(x.astype(BF16), y[0:2 * n].astype(BF16)).astype(BF16) for x, y in zip(m, xtv)]
            for i, (bi, q) in enumerate(group):
                o = ops[i]
                ys = xtv[i][2 * n:4 * n] + _dot(o["a_rb"], u[i])
                y_ref[bi, sl, q * LANES:(q + 1) * LANES] = ys[0:n] + ys[n:2 * n]
                t_ref[bi * npair + q] = \
                    t[i] * jnp.transpose(jnp.broadcast_to(o["decay"], (LANES, LANES))) \
                    + _dot_tn(jnp.concatenate([o["bt"], o["kt"]], axis=0),
                              jnp.concatenate([u[i], o["vs"]], axis=0))

        group = [(bi, q) for bi in range(nb) for q in range(npair)]
        ops = scores(group)
        finish(group, ops, inverse(ops))
        return carry

    lax.fori_loop(0, ts // n, chunk, 0)


def _wkv(r, k, v, kk, a, lw, *, ts, npair, nb):
    b, s, d = r.shape
    wd = npair * LANES
    spec = pl.BlockSpec((nb, ts, wd), lambda bi, p, i: (bi, i, p))
    return pl.pallas_call(
        _wkv_kernel,
        grid=(b // nb, d // wd, s // ts),
        in_specs=[spec] * 6,
        out_specs=spec,
        out_shape=jax.ShapeDtypeStruct((b, s, d), F32),
        scratch_shapes=[pltpu.VMEM((nb * npair, LANES, LANES), F32),
                        pltpu.VMEM((WKV_LEVELS + 3, LANES, LANES), F32)],
        compiler_params=_cparams(("parallel", "parallel", "arbitrary")),
        name="wkv7",
    )(r, k, v, kk, a, lw)


def _even_out_kernel(ya_ref, y_ref, bonus_ref, g_ref, lnw_ref, lnb_ref, w_ref, h_ref, o_ref):
    ka = ya_ref.shape[1]
    y = y_ref[...]
    inv_n = 1.0 / RWKV_HEAD
    s1 = _head_sum(y)
    s2 = _head_sum(y * y)
    acc = h_ref[...] + _dot(ya_ref[...], w_ref[0:ka, :])
    mean = s1 * inv_n
    var = jnp.maximum(s2 * inv_n - mean * mean, 0.0)
    yn = (y - mean) * lax.rsqrt(var + GN_EPS) * lnw_ref[...] + lnb_ref[...]
    yb = ((yn + bonus_ref[...].astype(F32)) * g_ref[...].astype(F32)).astype(BF16)
    o_ref[...] = acc + _dot(yb, w_ref[ka:, :])


def _even_out(ya, y, bonus, g, ln_w, ln_b, w, layer, h, *, tm):
    m, n = h.shape
    ka, kb = ya.shape[1], y.shape[1]
    blk = lambda width: pl.BlockSpec((tm, width), lambda i: (i, 0))
    row = pl.BlockSpec((1, kb), lambda i: (0, 0))
    return pl.pallas_call(
        _even_out_kernel,
        grid=(m // tm,),
        in_specs=[blk(ka), blk(kb), blk(kb), blk(kb), row, row,
                  pl.BlockSpec((None, ka + kb, n), lambda i: (layer, 0, 0)), blk(n)],
        out_specs=blk(n),
        out_shape=jax.ShapeDtypeStruct((m, n), F32),
        compiler_params=_cparams(("parallel",)),
        name="even_out",
    )(ya, y, bonus, g, ln_w, ln_b, w, h)


def _pool_kernel(u_ref, up_ref, w_ref, sc_ref, o_ref, buf_ref):
    ts = o_ref.shape[1]
    hist = max(POOL_WINDOWS)
    i = pl.program_id(1)
    u = u_ref[0].astype(F32)
    _fill_history(buf_ref, up_ref[0].astype(F32), u, hist, i == 0)
    pos = i * ts + lax.broadcasted_iota(jnp.int32, (ts, 1), 0)
    outs = []
    for gi, win in enumerate(POOL_WINDOWS):
        cols = pl.ds(gi * POOL_GROUP, POOL_GROUP)
        acc = buf_ref[pl.ds(hist, ts), cols]
        for j in range(1, win):
            acc = acc + buf_ref[pl.ds(hist - j, ts), cols]
        count = jnp.minimum(pos + 1, win).astype(F32)
        dg = acc / count - u[:, gi * POOL_GROUP:(gi + 1) * POOL_GROUP]
        outs.append(_dot(dg.astype(BF16), w_ref[gi]))
    y = jnp.concatenate(outs, axis=1) * sc_ref[...]
    o_ref[0] = y.astype(o_ref.dtype)


def _pool_mixer(p, pool_w, pool_scale, *, ts):
    b, s, _ = p.shape
    hist = max(POOL_WINDOWS)
    rb = ts // hist
    return pl.pallas_call(
        _pool_kernel,
        grid=(b, s // ts),
        in_specs=[
            pl.BlockSpec((1, ts, POOL_DIM), lambda bi, i: (bi, i, 0)),
            pl.BlockSpec((1, hist, POOL_DIM), lambda bi, i: (bi, jnp.maximum(i * rb - 1, 0), 0)),
            pl.BlockSpec((len(POOL_WINDOWS), POOL_GROUP, POOL_GROUP), lambda bi, i: (0, 0, 0)),
            pl.BlockSpec((1, POOL_DIM), lambda bi, i: (0, 0)),
        ],
        out_specs=pl.BlockSpec((1, ts, POOL_DIM), lambda bi, i: (bi, i, 0)),
        out_shape=jax.ShapeDtypeStruct((b, s, POOL_DIM), BF16),
        scratch_shapes=[pltpu.VMEM((ts + hist, POOL_DIM), F32)],
        compiler_params=_cparams(("parallel", "parallel")),
        name="pool_mixer",
    )(p, p, pool_w, pool_scale)


def _mla_prep_kernel(ql_ref, kvl_ref, kpe_ref, kper_ref, ang_ref, qn_ref, kvn_ref,
                     wq_ref, wqr_ref, wk_ref, wv_ref, q_out, k_out, v_out):
    cos = jnp.cos(ang_ref[...])
    sin = jnp.sin(ang_ref[...])
    qn = _rms(ql_ref[...].astype(F32), qn_ref[...]).astype(BF16)
    kvn = _rms(kvl_ref[...].astype(F32), kvn_ref[...]).astype(BF16)
    scale = (QK_NOPE + QK_ROPE) ** -0.5 * LOG2_E
    qa = _dot(qn, wq_ref[...])
    qr = _dot(qn, wqr_ref[...])
    ka = _dot(kvn, wk_ref[...])
    kpe = kpe_ref[...].astype(F32) * cos + kper_ref[...].astype(F32) * sin
    for h in range(MLA_HEADS):
        lo = h * QK_PAD
        q_out[:, lo:lo + LANES] = (qa[:, lo:lo + LANES] * scale).astype(BF16)
        q_pe = qa[:, lo + LANES:lo + 2 * LANES] * cos + qr[:, h * LANES:(h + 1) * LANES] * sin
        q_out[:, lo + LANES:lo + 2 * LANES] = (q_pe * scale).astype(BF16)
        k_out[:, lo:lo + LANES] = ka[:, h * QK_NOPE:(h + 1) * QK_NOPE].astype(BF16)
        k_out[:, lo + LANES:lo + 2 * LANES] = kpe.astype(BF16)
    vt = _dot_nt(wv_ref[...], kvn).astype(BF16)
    ones = jnp.ones((ATTN_SUM_ROWS, vt.shape[1]), BF16)
    for h in range(MLA_HEADS):
        v_out[h * VT_ROWS:h * VT_ROWS + V_HEAD, :] = vt[h * V_HEAD:(h + 1) * V_HEAD, :]
        v_out[h * VT_ROWS + V_HEAD:(h + 1) * VT_ROWS, :] = ones


def _mla_prep(p, ang, q_norm, kv_norm, wq, wqr, wk, wv, layer, *, tm):
    m = p.shape[0]
    hq = MLA_HEADS * QK_PAD
    hv = MLA_HEADS * VT_ROWS
    row = lambda a: pl.BlockSpec(a.shape, lambda i: (0, 0))
    full = lambda a: pl.BlockSpec((None,) + a.shape[1:], lambda i: (layer, 0, 0))
    return pl.pallas_call(
        _mla_prep_kernel,
        grid=(m // tm,),
        in_specs=[
            pl.BlockSpec((tm, Q_LORA), lambda i: (i, 1)),
            pl.BlockSpec((tm, KV_LORA), lambda i: (i, 2)),
            pl.BlockSpec((tm, LANES), lambda i: (i, 12)),
            pl.BlockSpec((tm, LANES), lambda i: (i, 13)),
            pl.BlockSpec((tm, LANES), lambda i: (i, 0)),
            row(q_norm), row(kv_norm), full(wq), full(wqr), full(wk), full(wv),
        ],
        out_specs=[pl.BlockSpec((tm, hq), lambda i: (i, 0)),
                   pl.BlockSpec((tm, hq), lambda i: (i, 0)),
                   pl.BlockSpec((hv, tm), lambda i: (0, i))],
        out_shape=[jax.ShapeDtypeStruct((m, hq), BF16),
                   jax.ShapeDtypeStruct((m, hq), BF16),
                   jax.ShapeDtypeStruct((hv, m), BF16)],
        compiler_params=_cparams(("parallel",)),
        name="mla_prep",
    )(p, p, p, p, ang, q_norm, kv_norm, wq, wqr, wk, wv)


def _attn_kernel(q_ref, k_ref, vt_ref, o_ref, *, tq, tk, nh):
    qi = pl.program_id(2)
    heads = range(nh)
    qs = [q_ref[0, :, h * QK_PAD:(h + 1) * QK_PAD] for h in heads]
    diag = tq // tk

    def block(j, carry, masked):
        m, acc = carry
        sl = pl.ds(pl.multiple_of(j * tk, tk), tk)
        s = [_dot_nt(k_ref[0, sl, h * QK_PAD:(h + 1) * QK_PAD], qs[h]) for h in heads]
        if masked:
            key = j * tk + lax.broadcasted_iota(jnp.int32, (tk, tq), 0)
            qry = qi * tq + lax.broadcasted_iota(jnp.int32, (tk, tq), 1)
            s = [jnp.where(key <= qry, x, -1e30) for x in s]
        m_new = [jnp.maximum(a, jnp.max(x, axis=0, keepdims=True)) for a, x in zip(m, s)]
        alpha = [jnp.exp2(a - b) for a, b in zip(m, m_new)]
        p = [jnp.exp2(x - a).astype(BF16) for x, a in zip(s, m_new)]
        pv = [_dot(vt_ref[h * VT_ROWS:(h + 1) * VT_ROWS, sl], p[h]) for h in heads]
        acc = [a * b + x for a, b, x in zip(alpha, acc, pv)]
        return m_new, acc

    init = ([jnp.full((1, tq), -1e30, F32) for _ in heads],
            [jnp.zeros((V_HEAD + ATTN_SUM_ROWS, tq), F32) for _ in heads])
    carry = lax.fori_loop(0, qi * diag, lambda j, c: block(j, c, False), init)
    for d in range(diag):
        carry = block(qi * diag + d, carry, True)
    _, acc = carry
    for h in heads:
        out = acc[h][0:V_HEAD] / acc[h][V_HEAD:V_HEAD + 1]
        o_ref[0, :, h * V_HEAD:(h + 1) * V_HEAD] = jnp.transpose(out).astype(o_ref.dtype)


def _attention(q, k, vt, *, tq, tk, nh):
    b, s, _ = q.shape
    return pl.pallas_call(
        functools.partial(_attn_kernel, tq=tq, tk=tk, nh=nh),
        grid=(b, MLA_HEADS // nh, s // tq),
        in_specs=[
            pl.BlockSpec((1, tq, nh * QK_PAD), lambda bi, h, i: (bi, i, h)),
            pl.BlockSpec((1, s, nh * QK_PAD), lambda bi, h, i: (bi, 0, h)),
            pl.BlockSpec((nh * VT_ROWS, s), lambda bi, h, i: (h, bi)),
        ],
        out_specs=pl.BlockSpec((1, tq, nh * V_HEAD), lambda bi, h, i: (bi, i, h)),
        out_shape=jax.ShapeDtypeStruct((b, s, MLA_HEADS * V_HEAD), BF16),
        compiler_params=_cparams(("parallel", "parallel", "arbitrary")),
        name="mla_attention",
    )(q, k, vt)


def _pad_last(w, n):
    return jnp.pad(w, [(0, 0)] * (w.ndim - 1) + [(0, n - w.shape[-1])])


def _pad_rows(w, n):
    return jnp.pad(w, ((0, n - w.shape[0]), (0, 0)))


def _rot_cols(w):
    half = w.shape[-1] // 2
    return jnp.concatenate([-w[..., half:], w[..., :half]], axis=-1)


def _even_weights(w_in, mu):
    c = 3 * CONV_DIM + 3 * RWKV_DIM
    split = lambda t, base: (t[..., base:base + W_LORA],
                             t[..., base + W_LORA:base + W_LORA + A_LORA],
                             t[..., base + W_LORA + A_LORA:])
    xw, xa, xg = split(w_in, c)
    w = jnp.concatenate([w_in[..., :c], _pad_last(xw, LANES), _pad_last(xa, LANES),
                         _pad_last(xg, 2 * LANES)], axis=-1).astype(BF16)
    r = 3 * RWKV_DIM
    mw, ma, mg = split(mu, r)
    mu_l = jnp.concatenate([_pad_last(mw, LANES), _pad_last(ma, LANES), _pad_last(mg, 2 * LANES)],
                           axis=-1)
    return w, mu[:, None, :r], mu_l[:, None, :]


def _odd_weights(w_in, w_uq, w_ukv):
    n = w_in.shape[0]
    o3 = POOL_DIM + Q_LORA + KV_LORA
    kpe = w_in[..., o3:]
    w = jnp.concatenate([w_in[..., :o3], _pad_last(kpe, LANES), _pad_last(_rot_cols(kpe), LANES)],
                        axis=-1).astype(BF16)
    uq = w_uq.reshape(n, Q_LORA, MLA_HEADS, QK_NOPE + QK_ROPE)
    wq = _pad_last(uq, QK_PAD).reshape(n, Q_LORA, MLA_HEADS * QK_PAD).astype(BF16)
    wqr = _pad_last(_rot_cols(uq[..., QK_NOPE:]), LANES)
    wqr = wqr.reshape(n, Q_LORA, MLA_HEADS * LANES).astype(BF16)
    ukv = w_ukv.reshape(n, KV_LORA, MLA_HEADS, QK_NOPE + V_HEAD)
    wk = ukv[..., :QK_NOPE].reshape(n, KV_LORA, MLA_HEADS * QK_NOPE).astype(BF16)
    wv = ukv[..., QK_NOPE:].reshape(n, KV_LORA, MLA_HEADS * V_HEAD)
    return w, wq, wqr, wk, jnp.swapaxes(wv, 1, 2).astype(BF16)


def _tile(n, pref):
    return pref if n % pref == 0 else n


def kernel(x, positions, ev_norm, ev_w_in, ev_conv_w, ev_mu, ev_w0, ev_w2, ev_a0, ev_a2, ev_g2, ev_k_k, ev_k_a, ev_r_k, ev_ln_w, ev_ln_b, ev_w_out, od_norm, od_w_in, od_pool_w, od_pool_scale, od_q_norm, od_w_uq, od_kv_norm, od_w_ukv, od_w_out, ffn_norm, ffn_w_gate, ffn_w_up, ffn_w_down, final_norm):
    b, s, d = x.shape
    m = b * s
    depth = ffn_norm.shape[0]
    tm = _tile(m, 512)
    tm_in = _tile(m, 1024)
    ts = _tile(s, 512)
    ts_prep = _tile(s, 256)

    inv = 1.0 / (ROPE_THETA ** (jnp.arange(0, QK_ROPE, 2, dtype=F32) / QK_ROPE))
    inv = jnp.concatenate([inv, inv, jnp.zeros((LANES - QK_ROPE,), F32)])
    ang = (positions.astype(F32)[..., None] * inv).reshape(m, LANES)

    ev_w, ev_mu_main, ev_mu_lora = _even_weights(ev_w_in, ev_mu)
    od_w, wq, wqr, wk, wv = _odd_weights(od_w_in, od_w_uq, od_w_ukv)
    ev_wo = ev_w_out.astype(BF16)
    od_wo = od_w_out.astype(BF16)
    wg, wu, wd = ffn_w_gate.astype(BF16), ffn_w_up.astype(BF16), ffn_w_down.astype(BF16)
    row = lambda t: t[None, :]

    h = x.reshape(m, d)
    for layer in range(depth):
        j = layer // 2
        if layer % 2 == 0:
            p = _norm_matmul(h, row(ev_norm[j]), ev_w, j, tm=tm_in, tn=EV_COLS // 4, out_dtype=BF16)
            p = p.reshape(b, s, EV_COLS)
            ya = _conv_mixer(p, ev_conv_w[j].T, ts=ts, tc=512)
            r, k, v, kk, a, lw, g, bonus = _rwkv_prep(
                p, ev_mu_main[j], ev_mu_lora[j], row(ev_w0[j]),
                _pad_rows(ev_w2[j], LANES).astype(BF16), row(ev_a0[j]),
                _pad_rows(ev_a2[j], LANES).astype(BF16), _pad_rows(ev_g2[j], 2 * LANES).astype(BF16),
                row(ev_k_k[j]), row(ev_k_a[j]), row(ev_r_k[j].reshape(-1)), ts=ts_prep)
            y = _wkv(r, k, v, kk, a, lw, ts=_tile(s, 256), npair=8, nb=2 if b % 2 == 0 else 1)
            h = _even_out(ya.reshape(m, -1), y.reshape(m, -1), bonus.reshape(m, -1),
                          g.reshape(m, -1), row(ev_ln_w[j]), row(ev_ln_b[j]), ev_wo, j, h, tm=tm)
        else:
            p = _norm_matmul(h, row(od_norm[j]), od_w, j, tm=tm_in, tn=OD_COLS // 2, out_dtype=BF16)
            yc = _pool_mixer(p.reshape(b, s, OD_COLS), od_pool_w[j].astype(BF16),
                             row(od_pool_scale[j]), ts=ts)
            q, kq, vv = _mla_prep(p, ang, row(od_q_norm[j]), row(od_kv_norm[j]),
                                  wq, wqr, wk, wv, j, tm=tm)
            hq = MLA_HEADS * QK_PAD
            yd = _attention(q.reshape(b, s, hq), kq.reshape(b, s, hq), vv, tq=_tile(s, 512),
                            tk=_tile(s, 512), nh=4)
            h = _proj_residual(yc.reshape(m, -1), yd.reshape(m, -1), od_wo, j, h, tm=tm, tn=d)
        h = _ffn(h, row(ffn_norm[layer]), row(final_norm), wg, wu, wd, layer, tm=tm_in, tf=512,
                 norm_out=layer == depth - 1)
    return h.reshape(b, s, d)
```

```python
import functools

import jax
import jax.numpy as jnp
from jax import lax
from jax.experimental import pallas as pl
from jax.experimental.pallas import tpu as pltpu

F32 = jnp.float32
BF16 = jnp.bfloat16

D_MODEL = 2048
NORM_EPS = 1e-6
CONV_DIM = 1024
RWKV_DIM = 1024
RWKV_HEAD = 64
W_LORA = 64
A_LORA = 64
G_LORA = 160
GN_EPS = 64e-5
POOL_WINDOWS = (2, 4, 8, 16)
POOL_GROUP = 128
POOL_DIM = 512
MLA_HEADS = 12
Q_LORA = 512
KV_LORA = 512
QK_NOPE = 128
QK_ROPE = 64
V_HEAD = 128
ROPE_THETA = 10000.0
D_FF = 5632
LOG2_E = 1.4426950408889634
EXP_NEG_HALF = 0.6065306597126334

LANES = 128
HIST = 16
VMEM_LIMIT = 60 * 1024 * 1024
QK_PAD = 256
ATTN_SUM_ROWS = 16
VT_ROWS = V_HEAD + ATTN_SUM_ROWS
WKV_CHUNK = 64
EV_COLS = 6656
EV_LORA = 512
OD_COLS = 1792


def _cparams(sem):
    return pltpu.CompilerParams(dimension_semantics=sem, vmem_limit_bytes=VMEM_LIMIT)


def _rms(x, g):
    ms = jnp.mean(x * x, axis=-1, keepdims=True)
    return x * lax.rsqrt(ms + NORM_EPS) * g


def _sigmoid(x):
    return 1.0 / (1.0 + jnp.exp(-x))


def _dot(a, b):
    return jnp.dot(a, b, preferred_element_type=F32)


def _dot_nt(a, b):
    return lax.dot_general(a, b, (((1,), (1,)), ((), ())), preferred_element_type=F32)


def _dot_tn(a, b):
    return lax.dot_general(a, b, (((0,), (0,)), ((), ())), preferred_element_type=F32)


def _norm_mm_kernel(x_ref, g_ref, w_ref, o_ref, xn_ref):
    @pl.when(pl.program_id(1) == 0)
    def _():
        xn_ref[...] = _rms(x_ref[...], g_ref[...]).astype(BF16)

    o_ref[...] = _dot(xn_ref[...], w_ref[...]).astype(o_ref.dtype)


def _norm_matmul(x, g, w, layer, *, tm, tn, out_dtype):
    m, k = x.shape
    n = w.shape[2]
    return pl.pallas_call(
        _norm_mm_kernel,
        grid=(m // tm, n // tn),
        in_specs=[
            pl.BlockSpec((tm, k), lambda i, j: (i, 0)),
            pl.BlockSpec((1, k), lambda i, j: (0, 0)),
            pl.BlockSpec((None, k, tn), lambda i, j: (layer, 0, j)),
        ],
        out_specs=pl.BlockSpec((tm, tn), lambda i, j: (i, j)),
        out_shape=jax.ShapeDtypeStruct((m, n), out_dtype),
        scratch_shapes=[pltpu.VMEM((tm, k), BF16)],
        compiler_params=_cparams(("parallel", "arbitrary")),
        name="norm_matmul",
    )(x, g, w)


def _proj_res_kernel(a_ref, b_ref, w_ref, h_ref, o_ref):
    ka = a_ref.shape[1]
    o_ref[...] = h_ref[...] + _dot(a_ref[...], w_ref[0:ka, :]) + _dot(b_ref[...], w_ref[ka:, :])


def _proj_residual(a, b, w, layer, h, *, tm, tn):
    m, n = h.shape
    ka, kb = a.shape[1], b.shape[1]
    return pl.pallas_call(
        _proj_res_kernel,
        grid=(m // tm, n // tn),
        in_specs=[
            pl.BlockSpec((tm, ka), lambda i, j: (i, 0)),
            pl.BlockSpec((tm, kb), lambda i, j: (i, 0)),
            pl.BlockSpec((None, ka + kb, tn), lambda i, j: (layer, 0, j)),
            pl.BlockSpec((tm, tn), lambda i, j: (i, j)),
        ],
        out_specs=pl.BlockSpec((tm, tn), lambda i, j: (i, j)),
        out_shape=jax.ShapeDtypeStruct((m, n), F32),
        compiler_params=_cparams(("parallel", "arbitrary")),
        name="proj_residual",
    )(a, b, w, h)


def _ffn_kernel(x_ref, g_ref, gout_ref, wg_ref, wu_ref, wd_ref, o_ref, xn_ref, *, norm_out):
    @pl.when(pl.program_id(1) == 0)
    def _():
        x = x_ref[...]
        xn_ref[...] = _rms(x, g_ref[...]).astype(BF16)
        o_ref[...] = x

    xn = xn_ref[...]
    a = _dot(xn, wg_ref[...])
    b = _dot(xn, wu_ref[...])
    act = (a * _sigmoid(a) * b).astype(BF16)
    o_ref[...] += _dot(act, wd_ref[...])

    if norm_out:
        @pl.when(pl.program_id(1) == pl.num_programs(1) - 1)
        def _():
            o_ref[...] = _rms(o_ref[...], gout_ref[...])


def _ffn(x, g, g_out, wg, wu, wd, layer, *, tm, tf, norm_out):
    m, d = x.shape
    f = wg.shape[2]
    return pl.pallas_call(
        functools.partial(_ffn_kernel, norm_out=norm_out),
        grid=(m // tm, f // tf),
        in_specs=[
            pl.BlockSpec((tm, d), lambda i, j: (i, 0)),
            pl.BlockSpec((1, d), lambda i, j: (0, 0)),
            pl.BlockSpec((1, d), lambda i, j: (0, 0)),
            pl.BlockSpec((None, d, tf), lambda i, j: (layer, 0, j)),
            pl.BlockSpec((None, d, tf), lambda i, j: (layer, 0, j)),
            pl.BlockSpec((None, tf, d), lambda i, j: (layer, j, 0)),
        ],
        out_specs=pl.BlockSpec((tm, d), lambda i, j: (i, 0)),
        out_shape=jax.ShapeDtypeStruct((m, d), F32),
        scratch_shapes=[pltpu.VMEM((tm, d), BF16)],
        compiler_params=_cparams(("parallel", "arbitrary")),
        name="ffn",
    )(x, g, g_out, wg, wu, wd)


def _fill_history(buf_ref, prev, cur, hist, is_first):
    buf_ref[pl.ds(0, hist), :] = jnp.where(is_first, 0.0, prev)
    buf_ref[pl.ds(hist, cur.shape[0]), :] = cur


def _conv_kernel(bg_ref, cg_ref, hh_ref, cgp_ref, hhp_ref, w_ref, o_ref, buf_ref):
    ts = o_ref.shape[1]
    u = cg_ref[0].astype(F32) * hh_ref[0].astype(F32)
    up = cgp_ref[0].astype(F32) * hhp_ref[0].astype(F32)
    _fill_history(buf_ref, up, u, HIST, pl.program_id(1) == 0)
    u1 = buf_ref[pl.ds(HIST - 1, ts), :]
    u2 = buf_ref[pl.ds(HIST - 2, ts), :]
    w = w_ref[...]
    y = w[2:3] * u + w[1:2] * u1 + w[0:1] * u2
    o_ref[0] = (bg_ref[0].astype(F32) * y).astype(o_ref.dtype)


def _conv_mixer(p, conv_w_t, *, ts, tc):
    b, s, _ = p.shape
    nc = CONV_DIM // tc
    rb = ts // HIST

    def cur(off):
        return pl.BlockSpec((1, ts, tc), lambda bi, i, c: (bi, i, off * nc + c))

    def prev(off):
        return pl.BlockSpec((1, HIST, tc),
                            lambda bi, i, c: (bi, jnp.maximum(i * rb - 1, 0), off * nc + c))

    return pl.pallas_call(
        _conv_kernel,
        grid=(b, s // ts, nc),
        in_specs=[cur(0), cur(1), cur(2), prev(1), prev(2),
                  pl.BlockSpec((3, tc), lambda bi, i, c: (0, c))],
        out_specs=pl.BlockSpec((1, ts, tc), lambda bi, i, c: (bi, i, c)),
        out_shape=jax.ShapeDtypeStruct((b, s, CONV_DIM), BF16),
        scratch_shapes=[pltpu.VMEM((ts + HIST, tc), F32)],
        compiler_params=_cparams(("parallel", "parallel", "parallel")),
        name="conv_mixer",
    )(p, p, p, p, p, conv_w_t)


def _head_sum(x, split=True):
    r = lax.broadcasted_iota(jnp.int32, (LANES, LANES), 0) // RWKV_HEAD
    c = lax.broadcasted_iota(jnp.int32, (LANES, LANES), 1) // RWKV_HEAD
    ones = jnp.where(r == c, 1.0, 0.0).astype(BF16)
    hi = x.astype(BF16)
    lo = (x - hi.astype(F32)).astype(BF16) if split else None
    parts = []
    for g in range(x.shape[1] // LANES):
        sl = slice(g * LANES, (g + 1) * LANES)
        part = _dot(hi[:, sl], ones)
        parts.append(part + _dot(lo[:, sl], ones) if split else part)
    return jnp.concatenate(parts, axis=1)


def _token_shift(xb, prev_rows, is_first):
    ts = xb.shape[0]
    r = lax.broadcasted_iota(jnp.int32, (ts, ts), 0)
    c = lax.broadcasted_iota(jnp.int32, (ts, ts), 1)
    xs = _dot(jnp.where(r == c + 1, 1.0, 0.0).astype(BF16), xb)
    hist = prev_rows.shape[0]
    last = jnp.where(is_first, 0.0, prev_rows[hist - 1:hist, :].astype(F32))
    top = jnp.where(lax.broadcasted_iota(jnp.int32, (HIST, 1), 0) == 0, last, xs[0:HIST])
    return jnp.concatenate([top, xs[HIST:]], axis=0)


def _rwkv_prep_kernel(x_ref, xl_ref, xp_ref, xlp_ref, mu_ref, mul_ref, w0_ref, w2_ref, a0_ref,
                      a2_ref, g2_ref, kk_ref, ka_ref, rk_ref,
                      r_out, k_out, v_out, kkn_out, a_out, lw_out, g_out, bonus_out):
    first = pl.program_id(1) == 0
    d = RWKV_DIM

    x = x_ref[0].astype(F32)
    xm = x + (_token_shift(x_ref[0], xp_ref[0], first) - x) * mu_ref[...]
    xl = xl_ref[0].astype(F32)
    xlm = xl + (_token_shift(xl_ref[0], xlp_ref[0], first) - xl) * mul_ref[...]

    r = xm[:, 0:d]
    k = xm[:, d:2 * d]
    v = xm[:, 2 * d:3 * d]
    xw = xlm[:, 0:LANES]
    xa = xlm[:, LANES:2 * LANES]
    xg = xlm[:, 2 * LANES:4 * LANES]

    y = w0_ref[...] + _dot(jnp.tanh(xw).astype(BF16), w2_ref[...])
    lw = -EXP_NEG_HALF * _sigmoid(y)
    a = _sigmoid(a0_ref[...] + _dot(xa.astype(BF16), a2_ref[...]))
    g = _dot(_sigmoid(xg).astype(BF16), g2_ref[...])

    kk = k * kk_ref[...]
    kk = kk * lax.rsqrt(jnp.maximum(_head_sum(kk * kk, split=False), 1e-24))
    k = k * (1.0 + (a - 1.0) * ka_ref[...])
    bonus = _head_sum(r * k * rk_ref[...], split=False) * v

    r_out[0] = r.astype(r_out.dtype)
    k_out[0] = k.astype(k_out.dtype)
    v_out[0] = v.astype(v_out.dtype)
    kkn_out[0] = kk.astype(kkn_out.dtype)
    a_out[0] = a.astype(a_out.dtype)
    lw_out[0] = lw
    g_out[0] = g.astype(g_out.dtype)
    bonus_out[0] = bonus.astype(bonus_out.dtype)


def _rwkv_prep(p, mu, mul, w0, w2, a0, a2, g2, k_k, k_a, r_k, *, ts):
    b, s, _ = p.shape
    d = RWKV_DIM
    rb = ts // HIST
    row = lambda shape: pl.BlockSpec(shape, lambda bi, i: (0, 0))
    out = pl.BlockSpec((1, ts, d), lambda bi, i: (bi, i, 0))
    return pl.pallas_call(
        _rwkv_prep_kernel,
        grid=(b, s // ts),
        in_specs=[
            pl.BlockSpec((1, ts, 3 * d), lambda bi, i: (bi, i, 1)),
            pl.BlockSpec((1, ts, EV_LORA), lambda bi, i: (bi, i, 6 * d // EV_LORA)),
            pl.BlockSpec((1, HIST, 3 * d), lambda bi, i: (bi, jnp.maximum(i * rb - 1, 0), 1)),
            pl.BlockSpec((1, HIST, EV_LORA),
                         lambda bi, i: (bi, jnp.maximum(i * rb - 1, 0), 6 * d // EV_LORA)),
            row((1, 3 * d)), row((1, EV_LORA)), row((1, d)), row((LANES, d)), row((1, d)),
            row((LANES, d)), row((2 * LANES, d)), row((1, d)), row((1, d)), row((1, d)),
        ],
        out_specs=[out] * 8,
        out_shape=[jax.ShapeDtypeStruct((b, s, d), BF16)] * 5
        + [jax.ShapeDtypeStruct((b, s, d), F32)] + [jax.ShapeDtypeStruct((b, s, d), BF16)] * 2,
        compiler_params=_cparams(("parallel", "parallel")),
        name="rwkv_prep",
    )(p, p, p, p, mu, mul, w0, w2, a0, a2, g2, k_k, k_a, r_k)


WKV_LEVELS = 6
_MASK_STRICT = WKV_LEVELS
_MASK_INCL = WKV_LEVELS + 1
_MASK_EYE = WKV_LEVELS + 2


def _wkv_kernel(r_ref, k_ref, v_ref, kk_ref, a_ref, lw_ref, y_ref, t_ref, mask_ref):
    n = WKV_CHUNK
    nb, ts = y_ref.shape[0], y_ref.shape[1]
    npair = y_ref.shape[2] // LANES

    @pl.when(pl.program_id(2) == 0)
    def _():
        t_ref[...] = jnp.zeros_like(t_ref)
        row = lax.broadcasted_iota(jnp.int32, (2 * n, 2 * n), 0)
        col = lax.broadcasted_iota(jnp.int32, (2 * n, 2 * n), 1)
        same = (row // n) == (col // n)
        x = row ^ col
        level = jnp.zeros_like(x)
        for bit in range(1, WKV_LEVELS):
            level = level + jnp.where(x >= (1 << bit), 1, 0)
        level = jnp.where(same & (row > col), level, -1)
        for lev in range(WKV_LEVELS):
            mask_ref[lev] = jnp.where(level == lev, 1.0, 0.0)
        mask_ref[_MASK_STRICT] = jnp.where(level >= 0, 1.0, 0.0)
        mask_ref[_MASK_INCL] = jnp.where(same & (row >= col), 1.0, 0.0)
        mask_ref[_MASK_EYE] = jnp.where(row == col, 1.0, 0.0)

    head0 = lax.broadcasted_iota(jnp.int32, (n, LANES), 1) < RWKV_HEAD
    tri = jnp.where(lax.broadcasted_iota(jnp.int32, (n, n), 0)
                    >= lax.broadcasted_iota(jnp.int32, (n, n), 1), 1.0, 0.0).astype(BF16)

    def stack(t):
        z = jnp.zeros_like(t)
        return jnp.concatenate([jnp.where(head0, t, z), jnp.where(head0, z, t)], axis=0).astype(BF16)

    def operands(sl, bi, q, cw):
        lanes = slice(q * LANES, (q + 1) * LANES)
        r = r_ref[bi, sl, lanes].astype(F32)
        k = k_ref[bi, sl, lanes].astype(F32)
        kk = kk_ref[bi, sl, lanes].astype(F32)
        lw = lw_ref[bi, sl, lanes]
        cw_end = cw[n - 1:n, :]
        p_inv = jnp.exp(-cw)
        p_to_end = jnp.exp(cw_end - cw)
        kka = kk * a_ref[bi, sl, lanes].astype(F32)
        xar = jnp.concatenate([stack(-kk * jnp.exp(cw - lw)), stack(r * jnp.exp(cw))], axis=0)
        ybk = jnp.concatenate([stack(kka * p_inv), stack(k * p_inv)], axis=0)
        return dict(xar=xar, ybk=ybk, bt=stack(kka * p_to_end), kt=stack(k * p_to_end),
                    vs=stack(v_ref[bi, sl, lanes].astype(F32)), decay=jnp.exp(cw_end))

    def cumulative(sl, bi):
        lw = lw_ref[bi, sl, :]
        hi = lw.astype(BF16)
        r1 = lw - hi.astype(F32)
        mid = r1.astype(BF16)
        lo = (r1 - mid.astype(F32)).astype(BF16)
        return _dot(tri, hi) + _dot(tri, mid) + _dot(tri, lo)

    def chunk(c, carry):
        sl = pl.ds(pl.multiple_of(c * n, n), n)
        cw = [cumulative(sl, bi) for bi in range(nb)]
        strict = mask_ref[_MASK_STRICT] > 0.5
        incl = mask_ref[_MASK_INCL] > 0.5

        def scores(group):
            ops = [operands(sl, bi, q, cw[bi][:, q * LANES:(q + 1) * LANES]) for bi, q in group]
            sc = [_dot_nt(o["xar"], o["ybk"]) for o in ops]
            for o, s in zip(ops, sc):
                o["a_ab"] = jnp.where(strict, s[0:2 * n, 0:2 * n], 0.0)
                o["a_k"] = jnp.concatenate([jnp.where(strict, s[0:2 * n, 2 * n:4 * n], 0.0),
                                            jnp.where(incl, s[2 * n:4 * n, 2 * n:4 * n], 0.0)],
                                           axis=0).astype(BF16)
                o["a_rb"] = jnp.where(incl, s[2 * n:4 * n, 0:2 * n], 0.0).astype(BF16)
            return ops

        def inverse(ops):
            m = [mask_ref[_MASK_EYE] + o["a_ab"] * mask_ref[0] for o in ops]
            for lev in range(1, WKV_LEVELS):
                mb = [x.astype(BF16) for x in m]
                em = [_dot((o["a_ab"] * mask_ref[lev]).astype(BF16), x).astype(BF16)
                      for o, x in zip(ops, mb)]
                m = [x + _dot(xb, y) for x, xb, y in zip(m, mb, em)]
            return m

        def finish(group, ops, m):
            t = [t_ref[bi * npair + q] for bi, q in group]
            xtv = [_dot(jnp.concatenate([o["xar"], o["a_k"]], axis=1),
                        jnp.concatenate([x.astype(BF16), o["vs"]], axis=0)) for o, x in zip(ops, t)]
            u = [_dot(x.astype(BF16), y[0:2 * n].astype(BF16)).astype(BF16) for x, y in zip(m, xtv)]
            for i, (bi, q) in enumerate(group):
                o = ops[i]
                ys = xtv[i][2 * n:4 * n] + _dot(o["a_rb"], u[i])
                y_ref[bi, sl, q * LANES:(q + 1) * LANES] = ys[0:n] + ys[n:2 * n]
                t_ref[bi * npair + q] = \
                    t[i] * jnp.transpose(jnp.broadcast_to(o["decay"], (LANES, LANES))) \
                    + _dot_tn(jnp.concatenate([o["bt"], o["kt"]], axis=0),
                              jnp.concatenate([u[i], o["vs"]], axis=0))

        group = [(bi, q) for bi in range(nb) for q in range(npair)]
        ops = scores(group)
        finish(group, ops, inverse(ops))
        return carry

    lax.fori_loop(0, ts // n, chunk, 0)


def _wkv(r, k, v, kk, a, lw, *, ts, npair, nb):
    b, s, d = r.shape
    wd = npair * LANES
    spec = pl.BlockSpec((nb, ts, wd), lambda bi, p, i: (bi, i, p))
    return pl.pallas_call(
        _wkv_kernel,
        grid=(b // nb, d // wd, s // ts),
        in_specs=[spec] * 6,
        out_specs=spec,
        out_shape=jax.ShapeDtypeStruct((b, s, d), F32),
        scratch_shapes=[pltpu.VMEM((nb * npair, LANES, LANES), F32),
                        pltpu.VMEM((WKV_LEVELS + 3, LANES, LANES), F32)],
        compiler_params=_cparams(("parallel", "parallel", "arbitrary")),
        name="wkv7",
    )(r, k, v, kk, a, lw)


def _even_out_kernel(ya_ref, y_ref, bonus_ref, g_ref, lnw_ref, lnb_ref, w_ref, h_ref, o_ref):
    ka = ya_ref.shape[1]
    y = y_ref[...]
    inv_n = 1.0 / RWKV_HEAD
    s1 = _head_sum(y)
    s2 = _head_sum(y * y)
    acc = h_ref[...] + _dot(ya_ref[...], w_ref[0:ka, :])
    mean = s1 * inv_n
    var = jnp.maximum(s2 * inv_n - mean * mean, 0.0)
    yn = (y - mean) * lax.rsqrt(var + GN_EPS) * lnw_ref[...] + lnb_ref[...]
    yb = ((yn + bonus_ref[...].astype(F32)) * g_ref[...].astype(F32)).astype(BF16)
    o_ref[...] = acc + _dot(yb, w_ref[ka:, :])


def _even_out(ya, y, bonus, g, ln_w, ln_b, w, layer, h, *, tm):
    m, n = h.shape
    ka, kb = ya.shape[1], y.shape[1]
    blk = lambda width: pl.BlockSpec((tm, width), lambda i: (i, 0))
    row = pl.BlockSpec((1, kb), lambda i: (0, 0))
    return pl.pallas_call(
        _even_out_kernel,
        grid=(m // tm,),
        in_specs=[blk(ka), blk(kb), blk(kb), blk(kb), row, row,
                  pl.BlockSpec((None, ka + kb, n), lambda i: (layer, 0, 0)), blk(n)],
        out_specs=blk(n),
        out_shape=jax.ShapeDtypeStruct((m, n), F32),
        compiler_params=_cparams(("parallel",)),
        name="even_out",
    )(ya, y, bonus, g, ln_w, ln_b, w, h)


def _pool_kernel(u_ref, up_ref, w_ref, sc_ref, o_ref, buf_ref):
    ts = o_ref.shape[1]
    hist = max(POOL_WINDOWS)
    i = pl.program_id(1)
    u = u_ref[0].astype(F32)
    _fill_history(buf_ref, up_ref[0].astype(F32), u, hist, i == 0)
    pos = i * ts + lax.broadcasted_iota(jnp.int32, (ts, 1), 0)
    outs = []
    for gi, win in enumerate(POOL_WINDOWS):
        cols = pl.ds(gi * POOL_GROUP, POOL_GROUP)
        acc = buf_ref[pl.ds(hist, ts), cols]
        for j in range(1, win):
            acc = acc + buf_ref[pl.ds(hist - j, ts), cols]
        count = jnp.minimum(pos + 1, win).astype(F32)
        dg = acc / count - u[:, gi * POOL_GROUP:(gi + 1) * POOL_GROUP]
        outs.append(_dot(dg.astype(BF16), w_ref[gi]))
    y = jnp.concatenate(outs, axis=1) * sc_ref[...]
    o_ref[0] = y.astype(o_ref.dtype)


def _pool_mixer(p, pool_w, pool_scale, *, ts):
    b, s, _ = p.shape
    hist = max(POOL_WINDOWS)
    rb = ts // hist
    return pl.pallas_call(
        _pool_kernel,
        grid=(b, s // ts),
        in_specs=[
            pl.BlockSpec((1, ts, POOL_DIM), lambda bi, i: (bi, i, 0)),
            pl.BlockSpec((1, hist, POOL_DIM), lambda bi, i: (bi, jnp.maximum(i * rb - 1, 0), 0)),
            pl.BlockSpec((len(POOL_WINDOWS), POOL_GROUP, POOL_GROUP), lambda bi, i: (0, 0, 0)),
            pl.BlockSpec((1, POOL_DIM), lambda bi, i: (0, 0)),
        ],
        out_specs=pl.BlockSpec((1, ts, POOL_DIM), lambda bi, i: (bi, i, 0)),
        out_shape=jax.ShapeDtypeStruct((b, s, POOL_DIM), BF16),
        scratch_shapes=[pltpu.VMEM((ts + hist, POOL_DIM), F32)],
        compiler_params=_cparams(("parallel", "parallel")),
        name="pool_mixer",
    )(p, p, pool_w, pool_scale)


def _mla_prep_kernel(ql_ref, kvl_ref, kpe_ref, kper_ref, ang_ref, qn_ref, kvn_ref,
                     wq_ref, wqr_ref, wk_ref, wv_ref, q_out, k_out, v_out):
    cos = jnp.cos(ang_ref[...])
    sin = jnp.sin(ang_ref[...])
    qn = _rms(ql_ref[...].astype(F32), qn_ref[...]).astype(BF16)
    kvn = _rms(kvl_ref[...].astype(F32), kvn_ref[...]).astype(BF16)
    scale = (QK_NOPE + QK_ROPE) ** -0.5 * LOG2_E
    qa = _dot(qn, wq_ref[...])
    qr = _dot(qn, wqr_ref[...])
    ka = _dot(kvn, wk_ref[...])
    kpe = kpe_ref[...].astype(F32) * cos + kper_ref[...].astype(F32) * sin
    for h in range(MLA_HEADS):
        lo = h * QK_PAD
        q_out[:, lo:lo + LANES] = (qa[:, lo:lo + LANES] * scale).astype(BF16)
        q_pe = qa[:, lo + LANES:lo + 2 * LANES] * cos + qr[:, h * LANES:(h + 1) * LANES] * sin
        q_out[:, lo + LANES:lo + 2 * LANES] = (q_pe * scale).astype(BF16)
        k_out[:, lo:lo + LANES] = ka[:, h * QK_NOPE:(h + 1) * QK_NOPE].astype(BF16)
        k_out[:, lo + LANES:lo + 2 * LANES] = kpe.astype(BF16)
    vt = _dot_nt(wv_ref[...], kvn).astype(BF16)
    ones = jnp.ones((ATTN_SUM_ROWS, vt.shape[1]), BF16)
    for h in range(MLA_HEADS):
        v_out[h * VT_ROWS:h * VT_ROWS + V_HEAD, :] = vt[h * V_HEAD:(h + 1) * V_HEAD, :]
        v_out[h * VT_ROWS + V_HEAD:(h + 1) * VT_ROWS, :] = ones


def _mla_prep(p, ang, q_norm, kv_norm, wq, wqr, wk, wv, layer, *, tm):
    m = p.shape[0]
    hq = MLA_HEADS * QK_PAD
    hv = MLA_HEADS * VT_ROWS
    row = lambda a: pl.BlockSpec(a.shape, lambda i: (0, 0))
    full = lambda a: pl.BlockSpec((None,) + a.shape[1:], lambda i: (layer, 0, 0))
    return pl.pallas_call(
        _mla_prep_kernel,
        grid=(m // tm,),
        in_specs=[
            pl.BlockSpec((tm, Q_LORA), lambda i: (i, 1)),
            pl.BlockSpec((tm, KV_LORA), lambda i: (i, 2)),
            pl.BlockSpec((tm, LANES), lambda i: (i, 12)),
            pl.BlockSpec((tm, LANES), lambda i: (i, 13)),
            pl.BlockSpec((tm, LANES), lambda i: (i, 0)),
            row(q_norm), row(kv_norm), full(wq), full(wqr), full(wk), full(wv),
        ],
        out_specs=[pl.BlockSpec((tm, hq), lambda i: (i, 0)),
                   pl.BlockSpec((tm, hq), lambda i: (i, 0)),
                   pl.BlockSpec((hv, tm), lambda i: (0, i))],
        out_shape=[jax.ShapeDtypeStruct((m, hq), BF16),
                   jax.ShapeDtypeStruct((m, hq), BF16),
                   jax.ShapeDtypeStruct((hv, m), BF16)],
        compiler_params=_cparams(("parallel",)),
        name="mla_prep",
    )(p, p, p, p, ang, q_norm, kv_norm, wq, wqr, wk, wv)


def _attn_kernel(q_ref, k_ref, vt_ref, o_ref, *, tq, tk, nh):
    qi = pl.program_id(2)
    heads = range(nh)
    qs = [q_ref[0, :, h * QK_PAD:(h + 1) * QK_PAD] for h in heads]
    def keys(j):
        return pl.ds(pl.multiple_of(j * tk, tk), tk)

    def scores(j):
        return [_dot_nt(k_ref[0, keys(j), h * QK_PAD:(h + 1) * QK_PAD], qs[h]) for h in heads]

    def softmax_pv(j, s, m, acc, masked):
        if masked:
            key = j * tk + lax.broadcasted_iota(jnp.int32, (tk, tq), 0)
            qry = qi * tq + lax.broadcasted_iota(jnp.int32, (tk, tq), 1)
            s = [jnp.where(key <= qry, x, -1e30) for x in s]
        m_new = [jnp.maximum(a, jnp.max(x, axis=0, keepdims=True)) for a, x in zip(m, s)]
        alpha = [jnp.exp2(a - b) for a, b in zip(m, m_new)]
        p = [jnp.exp2(x - a).astype(BF16) for x, a in zip(s, m_new)]
        pv = [_dot(vt_ref[h * VT_ROWS:(h + 1) * VT_ROWS, keys(j)], p[h]) for h in heads]
        return m_new, [a * b + x for a, b, x in zip(alpha, acc, pv)]

    def body(j, carry):
        return softmax_pv(j, scores(j), *carry, False)

    init = ([jnp.full((1, tq), -1e30, F32) for _ in heads],
            [jnp.zeros((V_HEAD + ATTN_SUM_ROWS, tq), F32) for _ in heads])
    m, acc = lax.fori_loop(0, qi, body, init)
    _, acc = softmax_pv(qi, scores(qi), m, acc, True)
    for h in heads:
        out = acc[h][0:V_HEAD] / acc[h][V_HEAD:V_HEAD + 1]
        o_ref[0, :, h * V_HEAD:(h + 1) * V_HEAD] = jnp.transpose(out).astype(o_ref.dtype)


def _attention(q, k, vt, *, tq, tk, nh):
    b, s, _ = q.shape
    assert tq == tk, "one key block per query tile sits on the causal diagonal"
    return pl.pallas_call(
        functools.partial(_attn_kernel, tq=tq, tk=tk, nh=nh),
        grid=(b, MLA_HEADS // nh, s // tq),
        in_specs=[
            pl.BlockSpec((1, tq, nh * QK_PAD), lambda bi, h, i: (bi, i, h)),
            pl.BlockSpec((1, s, nh * QK_PAD), lambda bi, h, i: (bi, 0, h)),
            pl.BlockSpec((nh * VT_ROWS, s), lambda bi, h, i: (h, bi)),
        ],
        out_specs=pl.BlockSpec((1, tq, nh * V_HEAD), lambda bi, h, i: (bi, i, h)),
        out_shape=jax.ShapeDtypeStruct((b, s, MLA_HEADS * V_HEAD), BF16),
        compiler_params=_cparams(("parallel", "parallel", "arbitrary")),
        name="mla_attention",
    )(q, k, vt)


def _pad_last(w, n):
    return jnp.pad(w, [(0, 0)] * (w.ndim - 1) + [(0, n - w.shape[-1])])


def _pad_rows(w, n):
    return jnp.pad(w, ((0, n - w.shape[0]), (0, 0)))


def _rot_cols(w):
    half = w.shape[-1] // 2
    return jnp.concatenate([-w[..., half:], w[..., :half]], axis=-1)


def _even_weights(w_in, mu):
    c = 3 * CONV_DIM + 3 * RWKV_DIM
    split = lambda t, base: (t[..., base:base + W_LORA],
                             t[..., base + W_LORA:base + W_LORA + A_LORA],
                             t[..., base + W_LORA + A_LORA:])
    xw, xa, xg = split(w_in, c)
    w = jnp.concatenate([w_in[..., :c], _pad_last(xw, LANES), _pad_last(xa, LANES),
                         _pad_last(xg, 2 * LANES)], axis=-1).astype(BF16)
    r = 3 * RWKV_DIM
    mw, ma, mg = split(mu, r)
    mu_l = jnp.concatenate([_pad_last(mw, LANES), _pad_last(ma, LANES), _pad_last(mg, 2 * LANES)],
                           axis=-1)
    return w, mu[:, None, :r], mu_l[:, None, :]


def _odd_weights(w_in, w_uq, w_ukv):
    n = w_in.shape[0]
    o3 = POOL_DIM + Q_LORA + KV_LORA
    kpe = w_in[..., o3:]
    w = jnp.concatenate([w_in[..., :o3], _pad_last(kpe, LANES), _pad_last(_rot_cols(kpe), LANES)],
                        axis=-1).astype(BF16)
    uq = w_uq.reshape(n, Q_LORA, MLA_HEADS, QK_NOPE + QK_ROPE)
    wq = _pad_last(uq, QK_PAD).reshape(n, Q_LORA, MLA_HEADS * QK_PAD).astype(BF16)
    wqr = _pad_last(_rot_cols(uq[..., QK_NOPE:]), LANES)
    wqr = wqr.reshape(n, Q_LORA, MLA_HEADS * LANES).astype(BF16)
    ukv = w_ukv.reshape(n, KV_LORA, MLA_HEADS, QK_NOPE + V_HEAD)
    wk = ukv[..., :QK_NOPE].reshape(n, KV_LORA, MLA_HEADS * QK_NOPE).astype(BF16)
    wv = ukv[..., QK_NOPE:].reshape(n, KV_LORA, MLA_HEADS * V_HEAD)
    return w, wq, wqr, wk, jnp.swapaxes(wv, 1, 2).astype(BF16)


def _tile(n, pref):
    return pref if n % pref == 0 else n


def kernel(x, positions, ev_norm, ev_w_in, ev_conv_w, ev_mu, ev_w0, ev_w2, ev_a0, ev_a2, ev_g2, ev_k_k, ev_k_a, ev_r_k, ev_ln_w, ev_ln_b, ev_w_out, od_norm, od_w_in, od_pool_w, od_pool_scale, od_q_norm, od_w_uq, od_kv_norm, od_w_ukv, od_w_out, ffn_norm, ffn_w_gate, ffn_w_up, ffn_w_down, final_norm):
    b, s, d = x.shape
    m = b * s
    depth = ffn_norm.shape[0]
    tm = _tile(m, 512)
    tm_in = _tile(m, 1024)
    ts = _tile(s, 512)
    ts_prep = _tile(s, 256)

    inv = 1.0 / (ROPE_THETA ** (jnp.arange(0, QK_ROPE, 2, dtype=F32) / QK_ROPE))
    inv = jnp.concatenate([inv, inv, jnp.zeros((LANES - QK_ROPE,), F32)])
    ang = (positions.astype(F32)[..., None] * inv).reshape(m, LANES)

    ev_w, ev_mu_main, ev_mu_lora = _even_weights(ev_w_in, ev_mu)
    od_w, wq, wqr, wk, wv = _odd_weights(od_w_in, od_w_uq, od_w_ukv)
    ev_wo = ev_w_out.astype(BF16)
    od_wo = od_w_out.astype(BF16)
    wg, wu, wd = ffn_w_gate.astype(BF16), ffn_w_up.astype(BF16), ffn_w_down.astype(BF16)
    row = lambda t: t[None, :]

    h = x.reshape(m, d)
    for layer in range(depth):
        j = layer // 2
        if layer % 2 == 0:
            p = _norm_matmul(h, row(ev_norm[j]), ev_w, j, tm=tm_in, tn=EV_COLS // 4, out_dtype=BF16)
            p = p.reshape(b, s, EV_COLS)
            ya = _conv_mixer(p, ev_conv_w[j].T, ts=_tile(s, 1024), tc=CONV_DIM)
            r, k, v, kk, a, lw, g, bonus = _rwkv_prep(
                p, ev_mu_main[j], ev_mu_lora[j], row(ev_w0[j]),
                _pad_rows(ev_w2[j], LANES).astype(BF16), row(ev_a0[j]),
                _pad_rows(ev_a2[j], LANES).astype(BF16), _pad_rows(ev_g2[j], 2 * LANES).astype(BF16),
                row(ev_k_k[j]), row(ev_k_a[j]), row(ev_r_k[j].reshape(-1)), ts=ts_prep)
            y = _wkv(r, k, v, kk, a, lw, ts=_tile(s, 256), npair=8, nb=2 if b % 2 == 0 else 1)
            h = _even_out(ya.reshape(m, -1), y.reshape(m, -1), bonus.reshape(m, -1),
                          g.reshape(m, -1), row(ev_ln_w[j]), row(ev_ln_b[j]), ev_wo, j, h, tm=tm)
        else:
            p = _norm_matmul(h, row(od_norm[j]), od_w, j, tm=tm_in, tn=OD_COLS // 2, out_dtype=BF16)
            yc = _pool_mixer(p.reshape(b, s, OD_COLS), od_pool_w[j].astype(BF16),
                             row(od_pool_scale[j]), ts=_tile(s, 1024))
            q, kq, vv = _mla_prep(p, ang, row(od_q_norm[j]), row(od_kv_norm[j]),
                                  wq, wqr, wk, wv, j, tm=tm)
            hq = MLA_HEADS * QK_PAD
            yd = _attention(q.reshape(b, s, hq), kq.reshape(b, s, hq), vv, tq=_tile(s, 512),
                            tk=_tile(s, 512), nh=4)
            h = _proj_residual(yc.reshape(m, -1), yd.reshape(m, -1), od_wo, j, h, tm=tm, tn=d)
        h = _ffn(h, row(ffn_norm[layer]), row(final_norm), wg, wu, wd, layer, tm=tm_in, tf=512,
                 norm_out=layer == depth - 1)
    return h.reshape(b, s, d)
```

```python
import functools

import jax
import jax.numpy as jnp
from jax import lax
from jax.experimental import pallas as pl
from jax.experimental.pallas import tpu as pltpu

F32 = jnp.float32
BF16 = jnp.bfloat16

NORM_EPS = 1e-6
CONV_DIM = 1024
RWKV_DIM = 1024
RWKV_HEAD = 64
W_LORA = 64
A_LORA = 64
GN_EPS = 64e-5
POOL_WINDOWS = (2, 4, 8, 16)
POOL_GROUP = 128
POOL_DIM = 512
MLA_HEADS = 12
Q_LORA = 512
KV_LORA = 512
QK_NOPE = 128
QK_ROPE = 64
V_HEAD = 128
ROPE_THETA = 10000.0
LOG2_E = 1.4426950408889634
EXP_NEG_HALF = 0.6065306597126334

LANES = 128
HIST = 16
VMEM_LIMIT = 60 * 1024 * 1024
QK_PAD = 256
ATTN_SUM_ROWS = 16
VT_ROWS = V_HEAD + ATTN_SUM_ROWS
WKV_CHUNK = 64
EV_COLS = 6656
EV_LORA = 512
OD_COLS = 1792


def _cparams(sem):
    return pltpu.CompilerParams(dimension_semantics=sem, vmem_limit_bytes=VMEM_LIMIT)


def _rms(x, g):
    ms = jnp.mean(x * x, axis=-1, keepdims=True)
    return x * lax.rsqrt(ms + NORM_EPS) * g


def _sigmoid(x):
    return 1.0 / (1.0 + jnp.exp(-x))


def _dot(a, b):
    return jnp.dot(a, b, preferred_element_type=F32)


def _dot_nt(a, b):
    return lax.dot_general(a, b, (((1,), (1,)), ((), ())), preferred_element_type=F32)


def _dot_tn(a, b):
    return lax.dot_general(a, b, (((0,), (0,)), ((), ())), preferred_element_type=F32)


def _norm_mm_kernel(x_ref, g_ref, w_ref, o_ref, xn_ref):
    @pl.when(pl.program_id(1) == 0)
    def _():
        xn_ref[...] = _rms(x_ref[...], g_ref[...]).astype(BF16)

    o_ref[...] = _dot(xn_ref[...], w_ref[...]).astype(o_ref.dtype)


def _norm_matmul(x, g, w, layer, *, tm, tn, out_dtype):
    m, k = x.shape
    n = w.shape[2]
    return pl.pallas_call(
        _norm_mm_kernel,
        grid=(m // tm, n // tn),
        in_specs=[
            pl.BlockSpec((tm, k), lambda i, j: (i, 0)),
            pl.BlockSpec((1, k), lambda i, j: (0, 0)),
            pl.BlockSpec((None, k, tn), lambda i, j: (layer, 0, j)),
        ],
        out_specs=pl.BlockSpec((tm, tn), lambda i, j: (i, j)),
        out_shape=jax.ShapeDtypeStruct((m, n), out_dtype),
        scratch_shapes=[pltpu.VMEM((tm, k), BF16)],
        compiler_params=_cparams(("parallel", "arbitrary")),
        name="norm_matmul",
    )(x, g, w)


def _proj_res_kernel(a_ref, b_ref, w_ref, h_ref, o_ref):
    ka = a_ref.shape[1]
    o_ref[...] = h_ref[...] + _dot(a_ref[...], w_ref[0:ka, :]) + _dot(b_ref[...], w_ref[ka:, :])


def _proj_residual(a, b, w, layer, h, *, tm, tn):
    m, n = h.shape
    ka, kb = a.shape[1], b.shape[1]
    return pl.pallas_call(
        _proj_res_kernel,
        grid=(m // tm, n // tn),
        in_specs=[
            pl.BlockSpec((tm, ka), lambda i, j: (i, 0)),
            pl.BlockSpec((tm, kb), lambda i, j: (i, 0)),
            pl.BlockSpec((None, ka + kb, tn), lambda i, j: (layer, 0, j)),
            pl.BlockSpec((tm, tn), lambda i, j: (i, j)),
        ],
        out_specs=pl.BlockSpec((tm, tn), lambda i, j: (i, j)),
        out_shape=jax.ShapeDtypeStruct((m, n), F32),
        compiler_params=_cparams(("parallel", "arbitrary")),
        name="proj_residual",
    )(a, b, w, h)


def _ffn_kernel(x_ref, g_ref, gout_ref, wg_ref, wu_ref, wd_ref, o_ref, xn_ref, *, norm_out):
    @pl.when(pl.program_id(1) == 0)
    def _():
        x = x_ref[...]
        xn_ref[...] = _rms(x, g_ref[...]).astype(BF16)
        o_ref[...] = x

    xn = xn_ref[...]
    a = _dot(xn, wg_ref[...])
    b = _dot(xn, wu_ref[...])
    act = (a * _sigmoid(a) * b).astype(BF16)
    o_ref[...] += _dot(act, wd_ref[...])

    if norm_out:
        @pl.when(pl.program_id(1) == pl.num_programs(1) - 1)
        def _():
            o_ref[...] = _rms(o_ref[...], gout_ref[...])


def _ffn(x, g, g_out, wg, wu, wd, layer, *, tm, tf, norm_out):
    m, d = x.shape
    f = wg.shape[2]
    return pl.pallas_call(
        functools.partial(_ffn_kernel, norm_out=norm_out),
        grid=(m // tm, f // tf),
        in_specs=[
            pl.BlockSpec((tm, d), lambda i, j: (i, 0)),
            pl.BlockSpec((1, d), lambda i, j: (0, 0)),
            pl.BlockSpec((1, d), lambda i, j: (0, 0)),
            pl.BlockSpec((None, d, tf), lambda i, j: (layer, 0, j)),
            pl.BlockSpec((None, d, tf), lambda i, j: (layer, 0, j)),
            pl.BlockSpec((None, tf, d), lambda i, j: (layer, j, 0)),
        ],
        out_specs=pl.BlockSpec((tm, d), lambda i, j: (i, 0)),
        out_shape=jax.ShapeDtypeStruct((m, d), F32),
        scratch_shapes=[pltpu.VMEM((tm, d), BF16)],
        compiler_params=_cparams(("parallel", "arbitrary")),
        name="ffn",
    )(x, g, g_out, wg, wu, wd)


def _fill_history(buf_ref, prev, cur, hist, is_first):
    buf_ref[pl.ds(0, hist), :] = jnp.where(is_first, 0.0, prev)
    buf_ref[pl.ds(hist, cur.shape[0]), :] = cur


def _conv_kernel(bg_ref, cg_ref, hh_ref, cgp_ref, hhp_ref, w_ref, o_ref, buf_ref):
    ts = o_ref.shape[1]
    u = cg_ref[0].astype(F32) * hh_ref[0].astype(F32)
    up = cgp_ref[0].astype(F32) * hhp_ref[0].astype(F32)
    _fill_history(buf_ref, up, u, HIST, pl.program_id(1) == 0)
    u1 = buf_ref[pl.ds(HIST - 1, ts), :]
    u2 = buf_ref[pl.ds(HIST - 2, ts), :]
    w = w_ref[...]
    y = w[2:3] * u + w[1:2] * u1 + w[0:1] * u2
    o_ref[0] = (bg_ref[0].astype(F32) * y).astype(o_ref.dtype)


def _conv_mixer(p, conv_w_t, *, ts, tc):
    b, s, _ = p.shape
    nc = CONV_DIM // tc
    rb = ts // HIST

    def cur(off):
        return pl.BlockSpec((1, ts, tc), lambda bi, i, c: (bi, i, off * nc + c))

    def prev(off):
        return pl.BlockSpec((1, HIST, tc),
                            lambda bi, i, c: (bi, jnp.maximum(i * rb - 1, 0), off * nc + c))

    return pl.pallas_call(
        _conv_kernel,
        grid=(b, s // ts, nc),
        in_specs=[cur(0), cur(1), cur(2), prev(1), prev(2),
                  pl.BlockSpec((3, tc), lambda bi, i, c: (0, c))],
        out_specs=pl.BlockSpec((1, ts, tc), lambda bi, i, c: (bi, i, c)),
        out_shape=jax.ShapeDtypeStruct((b, s, CONV_DIM), BF16),
        scratch_shapes=[pltpu.VMEM((ts + HIST, tc), F32)],
        compiler_params=_cparams(("parallel", "parallel", "parallel")),
        name="conv_mixer",
    )(p, p, p, p, p, conv_w_t)


def _head_sum(x, split=True):
    r = lax.broadcasted_iota(jnp.int32, (LANES, LANES), 0) // RWKV_HEAD
    c = lax.broadcasted_iota(jnp.int32, (LANES, LANES), 1) // RWKV_HEAD
    ones = jnp.where(r == c, 1.0, 0.0).astype(BF16)
    hi = x.astype(BF16)
    lo = (x - hi.astype(F32)).astype(BF16) if split else None
    parts = []
    for g in range(x.shape[1] // LANES):
        sl = slice(g * LANES, (g + 1) * LANES)
        part = _dot(hi[:, sl], ones)
        parts.append(part + _dot(lo[:, sl], ones) if split else part)
    return jnp.concatenate(parts, axis=1)


def _token_shift(xb, prev_rows, is_first):
    ts = xb.shape[0]
    r = lax.broadcasted_iota(jnp.int32, (ts, ts), 0)
    c = lax.broadcasted_iota(jnp.int32, (ts, ts), 1)
    xs = _dot(jnp.where(r == c + 1, 1.0, 0.0).astype(BF16), xb)
    hist = prev_rows.shape[0]
    last = jnp.where(is_first, 0.0, prev_rows[hist - 1:hist, :].astype(F32))
    top = jnp.where(lax.broadcasted_iota(jnp.int32, (HIST, 1), 0) == 0, last, xs[0:HIST])
    return jnp.concatenate([top, xs[HIST:]], axis=0)


def _rwkv_prep_kernel(x_ref, xl_ref, xp_ref, xlp_ref, mu_ref, mul_ref, w0_ref, w2_ref, a0_ref,
                      a2_ref, g2_ref, kk_ref, ka_ref, rk_ref,
                      r_out, k_out, v_out, kkn_out, a_out, lw_out, g_out, bonus_out):
    first = pl.program_id(1) == 0
    d = RWKV_DIM

    x = x_ref[0].astype(F32)
    xm = x + (_token_shift(x_ref[0], xp_ref[0], first) - x) * mu_ref[...]
    xl = xl_ref[0].astype(F32)
    xlm = xl + (_token_shift(xl_ref[0], xlp_ref[0], first) - xl) * mul_ref[...]

    r = xm[:, 0:d]
    k = xm[:, d:2 * d]
    v = xm[:, 2 * d:3 * d]
    xw = xlm[:, 0:LANES]
    xa = xlm[:, LANES:2 * LANES]
    xg = xlm[:, 2 * LANES:4 * LANES]

    y = w0_ref[...] + _dot(jnp.tanh(xw).astype(BF16), w2_ref[...])
    lw = -EXP_NEG_HALF * _sigmoid(y)
    a = _sigmoid(a0_ref[...] + _dot(xa.astype(BF16), a2_ref[...]))
    g = _dot(_sigmoid(xg).astype(BF16), g2_ref[...])

    kk = k * kk_ref[...]
    kk = kk * lax.rsqrt(jnp.maximum(_head_sum(kk * kk, split=False), 1e-24))
    k = k * (1.0 + (a - 1.0) * ka_ref[...])
    bonus = _head_sum(r * k * rk_ref[...], split=False) * v

    r_out[0] = r.astype(r_out.dtype)
    k_out[0] = k.astype(k_out.dtype)
    v_out[0] = v.astype(v_out.dtype)
    kkn_out[0] = kk.astype(kkn_out.dtype)
    a_out[0] = a.astype(a_out.dtype)
    lw_out[0] = lw
    g_out[0] = g.astype(g_out.dtype)
    bonus_out[0] = bonus.astype(bonus_out.dtype)


def _rwkv_prep(p, mu, mul, w0, w2, a0, a2, g2, k_k, k_a, r_k, *, ts):
    b, s, _ = p.shape
    d = RWKV_DIM
    rb = ts // HIST
    row = lambda shape: pl.BlockSpec(shape, lambda bi, i: (0, 0))
    out = pl.BlockSpec((1, ts, d), lambda bi, i: (bi, i, 0))
    return pl.pallas_call(
        _rwkv_prep_kernel,
        grid=(b, s // ts),
        in_specs=[
            pl.BlockSpec((1, ts, 3 * d), lambda bi, i: (bi, i, 1)),
            pl.BlockSpec((1, ts, EV_LORA), lambda bi, i: (bi, i, 6 * d // EV_LORA)),
            pl.BlockSpec((1, HIST, 3 * d), lambda bi, i: (bi, jnp.maximum(i * rb - 1, 0), 1)),
            pl.BlockSpec((1, HIST, EV_LORA),
                         lambda bi, i: (bi, jnp.maximum(i * rb - 1, 0), 6 * d // EV_LORA)),
            row((1, 3 * d)), row((1, EV_LORA)), row((1, d)), row((LANES, d)), row((1, d)),
            row((LANES, d)), row((2 * LANES, d)), row((1, d)), row((1, d)), row((1, d)),
        ],
        out_specs=[out] * 8,
        out_shape=[jax.ShapeDtypeStruct((b, s, d), BF16)] * 5
        + [jax.ShapeDtypeStruct((b, s, d), F32)] + [jax.ShapeDtypeStruct((b, s, d), BF16)] * 2,
        compiler_params=_cparams(("parallel", "parallel")),
        name="rwkv_prep",
    )(p, p, p, p, mu, mul, w0, w2, a0, a2, g2, k_k, k_a, r_k)


WKV_LEVELS = 6
_MASK_STRICT = WKV_LEVELS
_MASK_INCL = WKV_LEVELS + 1
_MASK_EYE = WKV_LEVELS + 2


def _wkv_kernel(r_ref, k_ref, v_ref, kk_ref, a_ref, lw_ref, y_ref, t_ref, mask_ref):
    n = WKV_CHUNK
    nb, ts = y_ref.shape[0], y_ref.shape[1]
    npair = y_ref.shape[2] // LANES

    @pl.when(pl.program_id(2) == 0)
    def _():
        t_ref[...] = jnp.zeros_like(t_ref)
        row = lax.broadcasted_iota(jnp.int32, (2 * n, 2 * n), 0)
        col = lax.broadcasted_iota(jnp.int32, (2 * n, 2 * n), 1)
        same = (row // n) == (col // n)
        x = row ^ col
        level = jnp.zeros_like(x)
        for bit in range(1, WKV_LEVELS):
            level = level + jnp.where(x >= (1 << bit), 1, 0)
        level = jnp.where(same & (row > col), level, -1)
        for lev in range(WKV_LEVELS):
            mask_ref[lev] = jnp.where(level == lev, 1.0, 0.0)
        mask_ref[_MASK_STRICT] = jnp.where(level >= 0, 1.0, 0.0)
        mask_ref[_MASK_INCL] = jnp.where(same & (row >= col), 1.0, 0.0)
        mask_ref[_MASK_EYE] = jnp.where(row == col, 1.0, 0.0)

    head0 = lax.broadcasted_iota(jnp.int32, (n, LANES), 1) < RWKV_HEAD
    tri = jnp.where(lax.broadcasted_iota(jnp.int32, (n, n), 0)
                    >= lax.broadcasted_iota(jnp.int32, (n, n), 1), 1.0, 0.0).astype(BF16)

    def stack(t):
        z = jnp.zeros_like(t)
        return jnp.concatenate([jnp.where(head0, t, z), jnp.where(head0, z, t)], axis=0).astype(BF16)

    def operands(sl, bi, q, cw):
        lanes = slice(q * LANES, (q + 1) * LANES)
        r = r_ref[bi, sl, lanes].astype(F32)
        k = k_ref[bi, sl, lanes].astype(F32)
        kk = kk_ref[bi, sl, lanes].astype(F32)
        lw = lw_ref[bi, sl, lanes]
        cw_end = cw[n - 1:n, :]
        p_inv = jnp.exp(-cw)
        p_to_end = jnp.exp(cw_end - cw)
        kka = kk * a_ref[bi, sl, lanes].astype(F32)
        xar = jnp.concatenate([stack(-kk * jnp.exp(cw - lw)), stack(r * jnp.exp(cw))], axis=0)
        ybk = jnp.concatenate([stack(kka * p_inv), stack(k * p_inv)], axis=0)
        return dict(xar=xar, ybk=ybk, bt=stack(kka * p_to_end), kt=stack(k * p_to_end),
                    vs=stack(v_ref[bi, sl, lanes].astype(F32)), decay=jnp.exp(cw_end))

    def cumulative(sl, bi):
        lw = lw_ref[bi, sl, :]
        hi = lw.astype(BF16)
        r1 = lw - hi.astype(F32)
        mid = r1.astype(BF16)
        lo = (r1 - mid.astype(F32)).astype(BF16)
        return _dot(tri, hi) + _dot(tri, mid) + _dot(tri, lo)

    def chunk(c, carry):
        sl = pl.ds(pl.multiple_of(c * n, n), n)
        cw = [cumulative(sl, bi) for bi in range(nb)]
        strict = mask_ref[_MASK_STRICT] > 0.5
        incl = mask_ref[_MASK_INCL] > 0.5

        def scores(group):
            ops = [operands(sl, bi, q, cw[bi][:, q * LANES:(q + 1) * LANES]) for bi, q in group]
            sc = [_dot_nt(o["xar"], o["ybk"]) for o in ops]
            for o, s in zip(ops, sc):
                o["a_ab"] = jnp.where(strict, s[0:2 * n, 0:2 * n], 0.0)
                o["a_k"] = jnp.concatenate([jnp.where(strict, s[0:2 * n, 2 * n:4 * n], 0.0),
                                            jnp.where(incl, s[2 * n:4 * n, 2 * n:4 * n], 0.0)],
                                           axis=0).astype(BF16)
                o["a_rb"] = jnp.where(incl, s[2 * n:4 * n, 0:2 * n], 0.0).astype(BF16)
            return ops

        def inverse(ops):
            m = [mask_ref[_MASK_EYE] + o["a_ab"] * mask_ref[0] for o in ops]
            for lev in range(1, WKV_LEVELS):
                mb = [x.astype(BF16) for x in m]
                em = [_dot((o["a_ab"] * mask_ref[lev]).astype(BF16), x).astype(BF16)
                      for o, x in zip(ops, mb)]
                m = [x + _dot(xb, y) for x, xb, y in zip(m, mb, em)]
            return m

        def finish(group, ops, m):
            t = [t_ref[bi * npair + q] for bi, q in group]
            xtv = [_dot(jnp.concatenate([o["xar"], o["a_k"]], axis=1),
                        jnp.concatenate([x.astype(BF16), o["vs"]], axis=0)) for o, x in zip(ops, t)]
            u = [_dot(x.astype(BF16), y[0:2 * n].astype(BF16)).astype(BF16) for x, y in zip(m, xtv)]
            for i, (bi, q) in enumerate(group):
                o = ops[i]
                ys = xtv[i][2 * n:4 * n] + _dot(o["a_rb"], u[i])
                y_ref[bi, sl, q * LANES:(q + 1) * LANES] = ys[0:n] + ys[n:2 * n]
                t_ref[bi * npair + q] = \
                    t[i] * jnp.transpose(jnp.broadcast_to(o["decay"], (LANES, LANES))) \
                    + _dot_tn(jnp.concatenate([o["bt"], o["kt"]], axis=0),
                              jnp.concatenate([u[i], o["vs"]], axis=0))

        group = [(bi, q) for bi in range(nb) for q in range(npair)]
        ops = scores(group)
        finish(group, ops, inverse(ops))
        return carry

    lax.fori_loop(0, ts // n, chunk, 0)


def _wkv(r, k, v, kk, a, lw, *, ts, npair, nb):
    b, s, d = r.shape
    wd = npair * LANES
    spec = pl.BlockSpec((nb, ts, wd), lambda bi, p, i: (bi, i, p))
    return pl.pallas_call(
        _wkv_kernel,
        grid=(b // nb, d // wd, s // ts),
        in_specs=[spec] * 6,
        out_specs=spec,
        out_shape=jax.ShapeDtypeStruct((b, s, d), F32),
        scratch_shapes=[pltpu.VMEM((nb * npair, LANES, LANES), F32),
                        pltpu.VMEM((WKV_LEVELS + 3, LANES, LANES), F32)],
        compiler_params=_cparams(("parallel", "parallel", "arbitrary")),
        name="wkv7",
    )(r, k, v, kk, a, lw)


def _even_out_kernel(ya_ref, y_ref, bonus_ref, g_ref, lnw_ref, lnb_ref, w_ref, h_ref, o_ref):
    ka = ya_ref.shape[1]
    y = y_ref[...]
    inv_n = 1.0 / RWKV_HEAD
    s1 = _head_sum(y)
    s2 = _head_sum(y * y)
    acc = h_ref[...] + _dot(ya_ref[...], w_ref[0:ka, :])
    mean = s1 * inv_n
    var = jnp.maximum(s2 * inv_n - mean * mean, 0.0)
    yn = (y - mean) * lax.rsqrt(var + GN_EPS) * lnw_ref[...] + lnb_ref[...]
    yb = ((yn + bonus_ref[...].astype(F32)) * g_ref[...].astype(F32)).astype(BF16)
    o_ref[...] = acc + _dot(yb, w_ref[ka:, :])


def _even_out(ya, y, bonus, g, ln_w, ln_b, w, layer, h, *, tm):
    m, n = h.shape
    ka, kb = ya.shape[1], y.shape[1]
    blk = lambda width: pl.BlockSpec((tm, width), lambda i: (i, 0))
    row = pl.BlockSpec((1, kb), lambda i: (0, 0))
    return pl.pallas_call(
        _even_out_kernel,
        grid=(m // tm,),
        in_specs=[blk(ka), blk(kb), blk(kb), blk(kb), row, row,
                  pl.BlockSpec((None, ka + kb, n), lambda i: (layer, 0, 0)), blk(n)],
        out_specs=blk(n),
        out_shape=jax.ShapeDtypeStruct((m, n), F32),
        compiler_params=_cparams(("parallel",)),
        name="even_out",
    )(ya, y, bonus, g, ln_w, ln_b, w, h)


def _pool_kernel(u_ref, up_ref, w_ref, sc_ref, o_ref, buf_ref):
    ts = o_ref.shape[1]
    hist = max(POOL_WINDOWS)
    i = pl.program_id(1)
    u = u_ref[0].astype(F32)
    _fill_history(buf_ref, up_ref[0].astype(F32), u, hist, i == 0)
    pos = i * ts + lax.broadcasted_iota(jnp.int32, (ts, 1), 0)
    outs = []
    for gi, win in enumerate(POOL_WINDOWS):
        cols = pl.ds(gi * POOL_GROUP, POOL_GROUP)
        acc = buf_ref[pl.ds(hist, ts), cols]
        for j in range(1, win):
            acc = acc + buf_ref[pl.ds(hist - j, ts), cols]
        count = jnp.minimum(pos + 1, win).astype(F32)
        dg = acc / count - u[:, gi * POOL_GROUP:(gi + 1) * POOL_GROUP]
        outs.append(_dot(dg.astype(BF16), w_ref[gi]))
    y = jnp.concatenate(outs, axis=1) * sc_ref[...]
    o_ref[0] = y.astype(o_ref.dtype)


def _pool_mixer(p, pool_w, pool_scale, *, ts):
    b, s, _ = p.shape
    hist = max(POOL_WINDOWS)
    rb = ts // hist
    return pl.pallas_call(
        _pool_kernel,
        grid=(b, s // ts),
        in_specs=[
            pl.BlockSpec((1, ts, POOL_DIM), lambda bi, i: (bi, i, 0)),
            pl.BlockSpec((1, hist, POOL_DIM), lambda bi, i: (bi, jnp.maximum(i * rb - 1, 0), 0)),
            pl.BlockSpec((len(POOL_WINDOWS), POOL_GROUP, POOL_GROUP), lambda bi, i: (0, 0, 0)),
            pl.BlockSpec((1, POOL_DIM), lambda bi, i: (0, 0)),
        ],
        out_specs=pl.BlockSpec((1, ts, POOL_DIM), lambda bi, i: (bi, i, 0)),
        out_shape=jax.ShapeDtypeStruct((b, s, POOL_DIM), BF16),
        scratch_shapes=[pltpu.VMEM((ts + hist, POOL_DIM), F32)],
        compiler_params=_cparams(("parallel", "parallel")),
        name="pool_mixer",
    )(p, p, pool_w, pool_scale)


def _mla_prep_kernel(ql_ref, kvl_ref, kpe_ref, kper_ref, ang_ref, qn_ref, kvn_ref,
                     wq_ref, wqr_ref, wk_ref, wv_ref, q_out, k_out, v_out):
    cos = jnp.cos(ang_ref[...])
    sin = jnp.sin(ang_ref[...])
    qn = _rms(ql_ref[...].astype(F32), qn_ref[...]).astype(BF16)
    kvn = _rms(kvl_ref[...].astype(F32), kvn_ref[...]).astype(BF16)
    scale = (QK_NOPE + QK_ROPE) ** -0.5 * LOG2_E
    qa = _dot(qn, wq_ref[...])
    qr = _dot(qn, wqr_ref[...])
    ka = _dot(kvn, wk_ref[...])
    kpe = kpe_ref[...].astype(F32) * cos + kper_ref[...].astype(F32) * sin
    for h in range(MLA_HEADS):
        lo = h * QK_PAD
        q_out[:, lo:lo + LANES] = (qa[:, lo:lo + LANES] * scale).astype(BF16)
        q_pe = qa[:, lo + LANES:lo + 2 * LANES] * cos + qr[:, h * LANES:(h + 1) * LANES] * sin
        q_out[:, lo + LANES:lo + 2 * LANES] = (q_pe * scale).astype(BF16)
        k_out[:, lo:lo + LANES] = ka[:, h * QK_NOPE:(h + 1) * QK_NOPE].astype(BF16)
        k_out[:, lo + LANES:lo + 2 * LANES] = kpe.astype(BF16)
    vt = _dot_nt(wv_ref[...], kvn).astype(BF16)
    ones = jnp.ones((ATTN_SUM_ROWS, vt.shape[1]), BF16)
    for h in range(MLA_HEADS):
        v_out[h * VT_ROWS:h * VT_ROWS + V_HEAD, :] = vt[h * V_HEAD:(h + 1) * V_HEAD, :]
        v_out[h * VT_ROWS + V_HEAD:(h + 1) * VT_ROWS, :] = ones


def _mla_prep(p, ang, q_norm, kv_norm, wq, wqr, wk, wv, layer, *, tm):
    m = p.shape[0]
    hq = MLA_HEADS * QK_PAD
    hv = MLA_HEADS * VT_ROWS
    row = lambda a: pl.BlockSpec(a.shape, lambda i: (0, 0))
    full = lambda a: pl.BlockSpec((None,) + a.shape[1:], lambda i: (layer, 0, 0))
    return pl.pallas_call(
        _mla_prep_kernel,
        grid=(m // tm,),
        in_specs=[
            pl.BlockSpec((tm, Q_LORA), lambda i: (i, 1)),
            pl.BlockSpec((tm, KV_LORA), lambda i: (i, 2)),
            pl.BlockSpec((tm, LANES), lambda i: (i, 12)),
            pl.BlockSpec((tm, LANES), lambda i: (i, 13)),
            pl.BlockSpec((tm, LANES), lambda i: (i, 0)),
            row(q_norm), row(kv_norm), full(wq), full(wqr), full(wk), full(wv),
        ],
        out_specs=[pl.BlockSpec((tm, hq), lambda i: (i, 0)),
                   pl.BlockSpec((tm, hq), lambda i: (i, 0)),
                   pl.BlockSpec((hv, tm), lambda i: (0, i))],
        out_shape=[jax.ShapeDtypeStruct((m, hq), BF16),
                   jax.ShapeDtypeStruct((m, hq), BF16),
                   jax.ShapeDtypeStruct((hv, m), BF16)],
        compiler_params=_cparams(("parallel",)),
        name="mla_prep",
    )(p, p, p, p, ang, q_norm, kv_norm, wq, wqr, wk, wv)


def _attn_kernel(q_ref, k_ref, vt_ref, o_ref, *, tq, tk, nh):
    qi = pl.program_id(2)
    heads = range(nh)
    qs = [q_ref[0, :, h * QK_PAD:(h + 1) * QK_PAD] for h in heads]
    def keys(j):
        return pl.ds(pl.multiple_of(j * tk, tk), tk)

    def scores(j):
        return [_dot_nt(k_ref[0, keys(j), h * QK_PAD:(h + 1) * QK_PAD], qs[h]) for h in heads]

    def softmax_pv(j, s, m, acc, masked):
        if masked:
            key = j * tk + lax.broadcasted_iota(jnp.int32, (tk, tq), 0)
            qry = qi * tq + lax.broadcasted_iota(jnp.int32, (tk, tq), 1)
            s = [jnp.where(key <= qry, x, -1e30) for x in s]
        m_new = [jnp.maximum(a, jnp.max(x, axis=0, keepdims=True)) for a, x in zip(m, s)]
        alpha = [jnp.exp2(a - b) for a, b in zip(m, m_new)]
        p = [jnp.exp2(x - a).astype(BF16) for x, a in zip(s, m_new)]
        pv = [_dot(vt_ref[h * VT_ROWS:(h + 1) * VT_ROWS, keys(j)], p[h]) for h in heads]
        return m_new, [a * b + x for a, b, x in zip(alpha, acc, pv)]

    def body(j, carry):
        return softmax_pv(j, scores(j), *carry, False)

    init = ([jnp.full((1, tq), -1e30, F32) for _ in heads],
            [jnp.zeros((V_HEAD + ATTN_SUM_ROWS, tq), F32) for _ in heads])
    m, acc = lax.fori_loop(0, qi, body, init)
    _, acc = softmax_pv(qi, scores(qi), m, acc, True)
    for h in heads:
        out = acc[h][0:V_HEAD] / acc[h][V_HEAD:V_HEAD + 1]
        o_ref[0, :, h * V_HEAD:(h + 1) * V_HEAD] = jnp.transpose(out).astype(o_ref.dtype)


def _attention(q, k, vt, *, tq, tk, nh):
    b, s, _ = q.shape
    assert tq == tk, "one key block per query tile sits on the causal diagonal"
    return pl.pallas_call(
        functools.partial(_attn_kernel, tq=tq, tk=tk, nh=nh),
        grid=(b, MLA_HEADS // nh, s // tq),
        in_specs=[
            pl.BlockSpec((1, tq, nh * QK_PAD), lambda bi, h, i: (bi, i, h)),
            pl.BlockSpec((1, s, nh * QK_PAD), lambda bi, h, i: (bi, 0, h)),
            pl.BlockSpec((nh * VT_ROWS, s), lambda bi, h, i: (h, bi)),
        ],
        out_specs=pl.BlockSpec((1, tq, nh * V_HEAD), lambda bi, h, i: (bi, i, h)),
        out_shape=jax.ShapeDtypeStruct((b, s, MLA_HEADS * V_HEAD), BF16),
        compiler_params=_cparams(("parallel", "parallel", "arbitrary")),
        name="mla_attention",
    )(q, k, vt)


def _pad_last(w, n):
    return jnp.pad(w, [(0, 0)] * (w.ndim - 1) + [(0, n - w.shape[-1])])


def _pad_rows(w, n):
    return jnp.pad(w, ((0, n - w.shape[0]), (0, 0)))


def _rot_cols(w):
    half = w.shape[-1] // 2
    return jnp.concatenate([-w[..., half:], w[..., :half]], axis=-1)


def _even_weights(w_in, mu):
    c = 3 * CONV_DIM + 3 * RWKV_DIM
    split = lambda t, base: (t[..., base:base + W_LORA],
                             t[..., base + W_LORA:base + W_LORA + A_LORA],
                             t[..., base + W_LORA + A_LORA:])
    xw, xa, xg = split(w_in, c)
    w = jnp.concatenate([w_in[..., :c], _pad_last(xw, LANES), _pad_last(xa, LANES),
                         _pad_last(xg, 2 * LANES)], axis=-1).astype(BF16)
    r = 3 * RWKV_DIM
    mw, ma, mg = split(mu, r)
    mu_l = jnp.concatenate([_pad_last(mw, LANES), _pad_last(ma, LANES), _pad_last(mg, 2 * LANES)],
                           axis=-1)
    return w, mu[:, None, :r], mu_l[:, None, :]


def _odd_weights(w_in, w_uq, w_ukv):
    n = w_in.shape[0]
    o3 = POOL_DIM + Q_LORA + KV_LORA
    kpe = w_in[..., o3:]
    w = jnp.concatenate([w_in[..., :o3], _pad_last(kpe, LANES), _pad_last(_rot_cols(kpe), LANES)],
                        axis=-1).astype(BF16)
    uq = w_uq.reshape(n, Q_LORA, MLA_HEADS, QK_NOPE + QK_ROPE)
    wq = _pad_last(uq, QK_PAD).reshape(n, Q_LORA, MLA_HEADS * QK_PAD).astype(BF16)
    wqr = _pad_last(_rot_cols(uq[..., QK_NOPE:]), LANES)
    wqr = wqr.reshape(n, Q_LORA, MLA_HEADS * LANES).astype(BF16)
    ukv = w_ukv.reshape(n, KV_LORA, MLA_HEADS, QK_NOPE + V_HEAD)
    wk = ukv[..., :QK_NOPE].reshape(n, KV_LORA, MLA_HEADS * QK_NOPE).astype(BF16)
    wv = ukv[..., QK_NOPE:].reshape(n, KV_LORA, MLA_HEADS * V_HEAD)
    return w, wq, wqr, wk, jnp.swapaxes(wv, 1, 2).astype(BF16)


def _tile(n, pref):
    return pref if n % pref == 0 else n


def _tiles(b, s):
    m = b * s
    return dict(
        rows_wide=_tile(m, 1024),
        rows=_tile(m, 512),
        ffn_cols=512,
        even_in_cols=EV_COLS // 4,
        odd_in_cols=OD_COLS // 2,
        seq_mixers=_tile(s, 1024),
        seq_prep=_tile(s, 256),
        seq_wkv=_tile(s, 256),
        wkv_pairs=RWKV_DIM // LANES,
        wkv_batch=2 if b % 2 == 0 else 1,
        attn=_tile(s, 512),
        attn_heads=4,
    )


def kernel(x, positions, ev_norm, ev_w_in, ev_conv_w, ev_mu, ev_w0, ev_w2, ev_a0, ev_a2, ev_g2, ev_k_k, ev_k_a, ev_r_k, ev_ln_w, ev_ln_b, ev_w_out, od_norm, od_w_in, od_pool_w, od_pool_scale, od_q_norm, od_w_uq, od_kv_norm, od_w_ukv, od_w_out, ffn_norm, ffn_w_gate, ffn_w_up, ffn_w_down, final_norm):
    b, s, d = x.shape
    m = b * s
    depth = ffn_norm.shape[0]
    t = _tiles(b, s)

    inv = 1.0 / (ROPE_THETA ** (jnp.arange(0, QK_ROPE, 2, dtype=F32) / QK_ROPE))
    inv = jnp.concatenate([inv, inv, jnp.zeros((LANES - QK_ROPE,), F32)])
    ang = (positions.astype(F32)[..., None] * inv).reshape(m, LANES)

    ev_w, ev_mu_main, ev_mu_lora = _even_weights(ev_w_in, ev_mu)
    od_w, wq, wqr, wk, wv = _odd_weights(od_w_in, od_w_uq, od_w_ukv)
    ev_wo = ev_w_out.astype(BF16)
    od_wo = od_w_out.astype(BF16)
    wg, wu, wd = ffn_w_gate.astype(BF16), ffn_w_up.astype(BF16), ffn_w_down.astype(BF16)
    row = lambda t: t[None, :]

    h = x.reshape(m, d)
    for layer in range(depth):
        j = layer // 2
        if layer % 2 == 0:
            p = _norm_matmul(h, row(ev_norm[j]), ev_w, j, tm=t["rows_wide"], tn=t["even_in_cols"],
                             out_dtype=BF16)
            p = p.reshape(b, s, EV_COLS)
            ya = _conv_mixer(p, ev_conv_w[j].T, ts=t["seq_mixers"], tc=CONV_DIM)
            r, k, v, kk, a, lw, g, bonus = _rwkv_prep(
                p, ev_mu_main[j], ev_mu_lora[j], row(ev_w0[j]),
                _pad_rows(ev_w2[j], LANES).astype(BF16), row(ev_a0[j]),
                _pad_rows(ev_a2[j], LANES).astype(BF16), _pad_rows(ev_g2[j], 2 * LANES).astype(BF16),
                row(ev_k_k[j]), row(ev_k_a[j]), row(ev_r_k[j].reshape(-1)), ts=t["seq_prep"])
            y = _wkv(r, k, v, kk, a, lw, ts=t["seq_wkv"], npair=t["wkv_pairs"], nb=t["wkv_batch"])
            h = _even_out(ya.reshape(m, -1), y.reshape(m, -1), bonus.reshape(m, -1),
                          g.reshape(m, -1), row(ev_ln_w[j]), row(ev_ln_b[j]), ev_wo, j, h,
                          tm=t["rows"])
        else:
            p = _norm_matmul(h, row(od_norm[j]), od_w, j, tm=t["rows_wide"], tn=t["odd_in_cols"],
                             out_dtype=BF16)
            yc = _pool_mixer(p.reshape(b, s, OD_COLS), od_pool_w[j].astype(BF16),
                             row(od_pool_scale[j]), ts=t["seq_mixers"])
            q, kq, vv = _mla_prep(p, ang, row(od_q_norm[j]), row(od_kv_norm[j]),
                                  wq, wqr, wk, wv, j, tm=t["rows"])
            hq = MLA_HEADS * QK_PAD
            yd = _attention(q.reshape(b, s, hq), kq.reshape(b, s, hq), vv, tq=t["attn"],
                            tk=t["attn"], nh=t["attn_heads"])
            h = _proj_residual(yc.reshape(m, -1), yd.reshape(m, -1), od_wo, j, h, tm=t["rows"], tn=d)
        h = _ffn(h, row(ffn_norm[layer]), row(final_norm), wg, wu, wd, layer, tm=t["rows_wide"],
                 tf=t["ffn_cols"], norm_out=layer == depth - 1)
    return h.reshape(b, s, d)
```

```python
import functools

import jax
import jax.numpy as jnp
from jax import lax
from jax.experimental import pallas as pl
from jax.experimental.pallas import tpu as pltpu

F32 = jnp.float32
BF16 = jnp.bfloat16

NORM_EPS = 1e-6
CONV_DIM = 1024
RWKV_DIM = 1024
RWKV_HEAD = 64
W_LORA = 64
A_LORA = 64
GN_EPS = 64e-5
POOL_WINDOWS = (2, 4, 8, 16)
POOL_GROUP = 128
POOL_DIM = 512
MLA_HEADS = 12
Q_LORA = 512
KV_LORA = 512
QK_NOPE = 128
QK_ROPE = 64
V_HEAD = 128
ROPE_THETA = 10000.0
LOG2_E = 1.4426950408889634
EXP_NEG_HALF = 0.6065306597126334

LANES = 128
HIST = 16
VMEM_LIMIT = 60 * 1024 * 1024
QK_PAD = 256
ATTN_SUM_ROWS = 16
VT_ROWS = V_HEAD + ATTN_SUM_ROWS
WKV_CHUNK = 64
EV_COLS = 6656
EV_LORA = 512
OD_COLS = 1792


def _cparams(sem):
    return pltpu.CompilerParams(dimension_semantics=sem, vmem_limit_bytes=VMEM_LIMIT)


def _rms(x, g):
    ms = jnp.mean(x * x, axis=-1, keepdims=True)
    return x * lax.rsqrt(ms + NORM_EPS) * g


def _sigmoid(x):
    return 1.0 / (1.0 + jnp.exp(-x))


def _dot(a, b):
    return jnp.dot(a, b, preferred_element_type=F32)


def _dot_nt(a, b):
    return lax.dot_general(a, b, (((1,), (1,)), ((), ())), preferred_element_type=F32)


def _dot_tn(a, b):
    return lax.dot_general(a, b, (((0,), (0,)), ((), ())), preferred_element_type=F32)


def _norm_mm_kernel(x_ref, g_ref, w_ref, o_ref, xn_ref):
    @pl.when(pl.program_id(1) == 0)
    def _():
        xn_ref[...] = _rms(x_ref[...], g_ref[...]).astype(BF16)

    o_ref[...] = _dot(xn_ref[...], w_ref[...]).astype(o_ref.dtype)


def _norm_matmul(x, g, w, layer, *, tm, tn, out_dtype):
    m, k = x.shape
    n = w.shape[2]
    return pl.pallas_call(
        _norm_mm_kernel,
        grid=(m // tm, n // tn),
        in_specs=[
            pl.BlockSpec((tm, k), lambda i, j: (i, 0)),
            pl.BlockSpec((1, k), lambda i, j: (0, 0)),
            pl.BlockSpec((None, k, tn), lambda i, j: (layer, 0, j)),
        ],
        out_specs=pl.BlockSpec((tm, tn), lambda i, j: (i, j)),
        out_shape=jax.ShapeDtypeStruct((m, n), out_dtype),
        scratch_shapes=[pltpu.VMEM((tm, k), BF16)],
        compiler_params=_cparams(("parallel", "arbitrary")),
        name="norm_matmul",
    )(x, g, w)


def _proj_res_kernel(a_ref, b_ref, w_ref, h_ref, o_ref):
    ka = a_ref.shape[1]
    o_ref[...] = h_ref[...] + _dot(a_ref[...], w_ref[0:ka, :]) + _dot(b_ref[...], w_ref[ka:, :])


def _proj_residual(a, b, w, layer, h, *, tm, tn):
    m, n = h.shape
    ka, kb = a.shape[1], b.shape[1]
    return pl.pallas_call(
        _proj_res_kernel,
        grid=(m // tm, n // tn),
        in_specs=[
            pl.BlockSpec((tm, ka), lambda i, j: (i, 0)),
            pl.BlockSpec((tm, kb), lambda i, j: (i, 0)),
            pl.BlockSpec((None, ka + kb, tn), lambda i, j: (layer, 0, j)),
            pl.BlockSpec((tm, tn), lambda i, j: (i, j)),
        ],
        out_specs=pl.BlockSpec((tm, tn), lambda i, j: (i, j)),
        out_shape=jax.ShapeDtypeStruct((m, n), F32),
        compiler_params=_cparams(("parallel", "arbitrary")),
        name="proj_residual",
    )(a, b, w, h)


def _ffn_kernel(x_ref, g_ref, gout_ref, wg_ref, wu_ref, wd_ref, o_ref, xn_ref, *, norm_out):
    @pl.when(pl.program_id(1) == 0)
    def _():
        x = x_ref[...]
        xn_ref[...] = _rms(x, g_ref[...]).astype(BF16)
        o_ref[...] = x

    xn = xn_ref[...]
    a = _dot(xn, wg_ref[...])
    b = _dot(xn, wu_ref[...])
    act = (a * _sigmoid(a) * b).astype(BF16)
    o_ref[...] += _dot(act, wd_ref[...])

    if norm_out:
        @pl.when(pl.program_id(1) == pl.num_programs(1) - 1)
        def _():
            o_ref[...] = _rms(o_ref[...], gout_ref[...])


def _ffn(x, g, g_out, wg, wu, wd, layer, *, tm, tf, norm_out):
    m, d = x.shape
    f = wg.shape[2]
    return pl.pallas_call(
        functools.partial(_ffn_kernel, norm_out=norm_out),
        grid=(m // tm, f // tf),
        in_specs=[
            pl.BlockSpec((tm, d), lambda i, j: (i, 0)),
            pl.BlockSpec((1, d), lambda i, j: (0, 0)),
            pl.BlockSpec((1, d), lambda i, j: (0, 0)),
            pl.BlockSpec((None, d, tf), lambda i, j: (layer, 0, j)),
            pl.BlockSpec((None, d, tf), lambda i, j: (layer, 0, j)),
            pl.BlockSpec((None, tf, d), lambda i, j: (layer, j, 0)),
        ],
        out_specs=pl.BlockSpec((tm, d), lambda i, j: (i, 0)),
        out_shape=jax.ShapeDtypeStruct((m, d), F32),
        scratch_shapes=[pltpu.VMEM((tm, d), BF16)],
        compiler_params=_cparams(("parallel", "arbitrary")),
        name="ffn",
    )(x, g, g_out, wg, wu, wd)


def _fill_history(buf_ref, prev, cur, hist, is_first):
    buf_ref[pl.ds(0, hist), :] = jnp.where(is_first, 0.0, prev)
    buf_ref[pl.ds(hist, cur.shape[0]), :] = cur


def _conv_kernel(bg_ref, cg_ref, hh_ref, cgp_ref, hhp_ref, w_ref, o_ref, buf_ref):
    ts = o_ref.shape[1]
    u = cg_ref[0].astype(F32) * hh_ref[0].astype(F32)
    up = cgp_ref[0].astype(F32) * hhp_ref[0].astype(F32)
    _fill_history(buf_ref, up, u, HIST, pl.program_id(1) == 0)
    u1 = buf_ref[pl.ds(HIST - 1, ts), :]
    u2 = buf_ref[pl.ds(HIST - 2, ts), :]
    w = w_ref[...]
    y = w[2:3] * u + w[1:2] * u1 + w[0:1] * u2
    o_ref[0] = (bg_ref[0].astype(F32) * y).astype(o_ref.dtype)


def _conv_mixer(p, conv_w_t, *, ts, tc):
    b, s, _ = p.shape
    nc = CONV_DIM // tc
    rb = ts // HIST

    def cur(off):
        return pl.BlockSpec((1, ts, tc), lambda bi, i, c: (bi, i, off * nc + c))

    def prev(off):
        return pl.BlockSpec((1, HIST, tc),
                            lambda bi, i, c: (bi, jnp.maximum(i * rb - 1, 0), off * nc + c))

    return pl.pallas_call(
        _conv_kernel,
        grid=(b, s // ts, nc),
        in_specs=[cur(0), cur(1), cur(2), prev(1), prev(2),
                  pl.BlockSpec((3, tc), lambda bi, i, c: (0, c))],
        out_specs=pl.BlockSpec((1, ts, tc), lambda bi, i, c: (bi, i, c)),
        out_shape=jax.ShapeDtypeStruct((b, s, CONV_DIM), BF16),
        scratch_shapes=[pltpu.VMEM((ts + HIST, tc), F32)],
        compiler_params=_cparams(("parallel", "parallel", "parallel")),
        name="conv_mixer",
    )(p, p, p, p, p, conv_w_t)


def _head_sum(x, split=True):
    r = lax.broadcasted_iota(jnp.int32, (LANES, LANES), 0) // RWKV_HEAD
    c = lax.broadcasted_iota(jnp.int32, (LANES, LANES), 1) // RWKV_HEAD
    ones = jnp.where(r == c, 1.0, 0.0).astype(BF16)
    hi = x.astype(BF16)
    lo = (x - hi.astype(F32)).astype(BF16) if split else None
    parts = []
    for g in range(x.shape[1] // LANES):
        sl = slice(g * LANES, (g + 1) * LANES)
        part = _dot(hi[:, sl], ones)
        parts.append(part + _dot(lo[:, sl], ones) if split else part)
    return jnp.concatenate(parts, axis=1)


def _token_shift(xb, prev_rows, is_first):
    ts = xb.shape[0]
    r = lax.broadcasted_iota(jnp.int32, (ts, ts), 0)
    c = lax.broadcasted_iota(jnp.int32, (ts, ts), 1)
    xs = _dot(jnp.where(r == c + 1, 1.0, 0.0).astype(BF16), xb)
    hist = prev_rows.shape[0]
    last = jnp.where(is_first, 0.0, prev_rows[hist - 1:hist, :].astype(F32))
    top = jnp.where(lax.broadcasted_iota(jnp.int32, (HIST, 1), 0) == 0, last, xs[0:HIST])
    return jnp.concatenate([top, xs[HIST:]], axis=0)


def _rwkv_prep_kernel(x_ref, xl_ref, xp_ref, xlp_ref, mu_ref, mul_ref, w0_ref, w2_ref, a0_ref,
                      a2_ref, g2_ref, kk_ref, ka_ref, rk_ref,
                      r_out, k_out, v_out, kkn_out, a_out, lw_out, g_out, bonus_out):
    first = pl.program_id(1) == 0
    d = RWKV_DIM

    x = x_ref[0].astype(F32)
    xm = x + (_token_shift(x_ref[0], xp_ref[0], first) - x) * mu_ref[...]
    xl = xl_ref[0].astype(F32)
    xlm = xl + (_token_shift(xl_ref[0], xlp_ref[0], first) - xl) * mul_ref[...]

    r = xm[:, 0:d]
    k = xm[:, d:2 * d]
    v = xm[:, 2 * d:3 * d]
    xw = xlm[:, 0:LANES]
    xa = xlm[:, LANES:2 * LANES]
    xg = xlm[:, 2 * LANES:4 * LANES]

    y = w0_ref[...] + _dot(jnp.tanh(xw).astype(BF16), w2_ref[...])
    lw = -EXP_NEG_HALF * _sigmoid(y)
    a = _sigmoid(a0_ref[...] + _dot(xa.astype(BF16), a2_ref[...]))
    g = _dot(_sigmoid(xg).astype(BF16), g2_ref[...])

    kk = k * kk_ref[...]
    kk = kk * lax.rsqrt(jnp.maximum(_head_sum(kk * kk, split=False), 1e-24))
    k = k * (1.0 + (a - 1.0) * ka_ref[...])
    bonus = _head_sum(r * k * rk_ref[...], split=False) * v

    r_out[0] = r.astype(r_out.dtype)
    k_out[0] = k.astype(k_out.dtype)
    v_out[0] = v.astype(v_out.dtype)
    kkn_out[0] = kk.astype(kkn_out.dtype)
    a_out[0] = a.astype(a_out.dtype)
    lw_out[0] = lw
    g_out[0] = g.astype(g_out.dtype)
    bonus_out[0] = bonus.astype(bonus_out.dtype)


def _rwkv_prep(p, mu, mul, w0, w2, a0, a2, g2, k_k, k_a, r_k, *, ts):
    b, s, _ = p.shape
    d = RWKV_DIM
    rb = ts // HIST
    row = lambda shape: pl.BlockSpec(shape, lambda bi, i: (0, 0))
    out = pl.BlockSpec((1, ts, d), lambda bi, i: (bi, i, 0))
    return pl.pallas_call(
        _rwkv_prep_kernel,
        grid=(b, s // ts),
        in_specs=[
            pl.BlockSpec((1, ts, 3 * d), lambda bi, i: (bi, i, 1)),
            pl.BlockSpec((1, ts, EV_LORA), lambda bi, i: (bi, i, 6 * d // EV_LORA)),
            pl.BlockSpec((1, HIST, 3 * d), lambda bi, i: (bi, jnp.maximum(i * rb - 1, 0), 1)),
            pl.BlockSpec((1, HIST, EV_LORA),
                         lambda bi, i: (bi, jnp.maximum(i * rb - 1, 0), 6 * d // EV_LORA)),
            row((1, 3 * d)), row((1, EV_LORA)), row((1, d)), row((LANES, d)), row((1, d)),
            row((LANES, d)), row((2 * LANES, d)), row((1, d)), row((1, d)), row((1, d)),
        ],
        out_specs=[out] * 8,
        out_shape=[jax.ShapeDtypeStruct((b, s, d), BF16)] * 5
        + [jax.ShapeDtypeStruct((b, s, d), F32)] + [jax.ShapeDtypeStruct((b, s, d), BF16)] * 2,
        compiler_params=_cparams(("parallel", "parallel")),
        name="rwkv_prep",
    )(p, p, p, p, mu, mul, w0, w2, a0, a2, g2, k_k, k_a, r_k)


WKV_LEVELS = 6
_MASK_STRICT = WKV_LEVELS
_MASK_INCL = WKV_LEVELS + 1
_MASK_EYE = WKV_LEVELS + 2


def _wkv_kernel(r_ref, k_ref, v_ref, kk_ref, a_ref, lw_ref, y_ref, t_ref, mask_ref):
    n = WKV_CHUNK
    nb, ts = y_ref.shape[0], y_ref.shape[1]
    npair = y_ref.shape[2] // LANES

    @pl.when(pl.program_id(2) == 0)
    def _():
        t_ref[...] = jnp.zeros_like(t_ref)
        row = lax.broadcasted_iota(jnp.int32, (2 * n, 2 * n), 0)
        col = lax.broadcasted_iota(jnp.int32, (2 * n, 2 * n), 1)
        same = (row // n) == (col // n)
        x = row ^ col
        level = jnp.zeros_like(x)
        for bit in range(1, WKV_LEVELS):
            level = level + jnp.where(x >= (1 << bit), 1, 0)
        level = jnp.where(same & (row > col), level, -1)
        for lev in range(WKV_LEVELS):
            mask_ref[lev] = jnp.where(level == lev, 1.0, 0.0)
        mask_ref[_MASK_STRICT] = jnp.where(level >= 0, 1.0, 0.0)
        mask_ref[_MASK_INCL] = jnp.where(same & (row >= col), 1.0, 0.0)
        mask_ref[_MASK_EYE] = jnp.where(row == col, 1.0, 0.0)

    head0 = lax.broadcasted_iota(jnp.int32, (n, LANES), 1) < RWKV_HEAD
    tri = jnp.where(lax.broadcasted_iota(jnp.int32, (n, n), 0)
                    >= lax.broadcasted_iota(jnp.int32, (n, n), 1), 1.0, 0.0).astype(BF16)

    def stack(t):
        z = jnp.zeros_like(t)
        return jnp.concatenate([jnp.where(head0, t, z), jnp.where(head0, z, t)], axis=0).astype(BF16)

    def operands(sl, bi, q, cw):
        lanes = slice(q * LANES, (q + 1) * LANES)
        r = r_ref[bi, sl, lanes].astype(F32)
        k = k_ref[bi, sl, lanes].astype(F32)
        kk = kk_ref[bi, sl, lanes].astype(F32)
        lw = lw_ref[bi, sl, lanes]
        cw_end = cw[n - 1:n, :]
        p_inv = jnp.exp(-cw)
        p_to_end = jnp.exp(cw_end - cw)
        kka = kk * a_ref[bi, sl, lanes].astype(F32)
        xar = jnp.concatenate([stack(-kk * jnp.exp(cw - lw)), stack(r * jnp.exp(cw))], axis=0)
        ybk = jnp.concatenate([stack(kka * p_inv), stack(k * p_inv)], axis=0)
        return dict(xar=xar, ybk=ybk, bt=stack(kka * p_to_end), kt=stack(k * p_to_end),
                    vs=stack(v_ref[bi, sl, lanes].astype(F32)), decay=jnp.exp(cw_end))

    def cumulative(sl, bi):
        lw = lw_ref[bi, sl, :]
        hi = lw.astype(BF16)
        r1 = lw - hi.astype(F32)
        mid = r1.astype(BF16)
        lo = (r1 - mid.astype(F32)).astype(BF16)
        return _dot(tri, hi) + _dot(tri, mid) + _dot(tri, lo)

    def chunk(c, carry):
        sl = pl.ds(pl.multiple_of(c * n, n), n)
        cw = [cumulative(sl, bi) for bi in range(nb)]
        strict = mask_ref[_MASK_STRICT] > 0.5
        incl = mask_ref[_MASK_INCL] > 0.5

        def scores(group):
            ops = [operands(sl, bi, q, cw[bi][:, q * LANES:(q + 1) * LANES]) for bi, q in group]
            sc = [_dot_nt(o["xar"], o["ybk"]) for o in ops]
            for o, s in zip(ops, sc):
                o["a_ab"] = jnp.where(strict, s[0:2 * n, 0:2 * n], 0.0)
                o["a_k"] = jnp.concatenate([jnp.where(strict, s[0:2 * n, 2 * n:4 * n], 0.0),
                                            jnp.where(incl, s[2 * n:4 * n, 2 * n:4 * n], 0.0)],
                                           axis=0).astype(BF16)
                o["a_rb"] = jnp.where(incl, s[2 * n:4 * n, 0:2 * n], 0.0).astype(BF16)
            return ops

        def inverse(ops):
            m = [mask_ref[_MASK_EYE] + o["a_ab"] * mask_ref[0] for o in ops]
            for lev in range(1, WKV_LEVELS):
                mb = [x.astype(BF16) for x in m]
                em = [_dot((o["a_ab"] * mask_ref[lev]).astype(BF16), x).astype(BF16)
                      for o, x in zip(ops, mb)]
                m = [x + _dot(xb, y) for x, xb, y in zip(m, mb, em)]
            return m

        def finish(group, ops, m):
            t = [t_ref[bi * npair + q] for bi, q in group]
            xtv = [_dot(jnp.concatenate([o["xar"], o["a_k"]], axis=1),
                        jnp.concatenate([x.astype(BF16), o["vs"]], axis=0)) for o, x in zip(ops, t)]
            u = [_dot(x.astype(BF16), y[0:2 * n].astype(BF16)).astype(BF16) for x, y in zip(m, xtv)]
            for i, (bi, q) in enumerate(group):
                o = ops[i]
                ys = xtv[i][2 * n:4 * n] + _dot(o["a_rb"], u[i])
                y_ref[bi, sl, q * LANES:(q + 1) * LANES] = ys[0:n] + ys[n:2 * n]
                t_ref[bi * npair + q] = \
                    t[i] * jnp.transpose(jnp.broadcast_to(o["decay"], (LANES, LANES))) \
                    + _dot_tn(jnp.concatenate([o["bt"], o["kt"]], axis=0),
                              jnp.concatenate([u[i], o["vs"]], axis=0))

        group = [(bi, q) for bi in range(nb) for q in range(npair)]
        ops = scores(group)
        finish(group, ops, inverse(ops))
        return carry

    lax.fori_loop(0, ts // n, chunk, 0)


def _wkv(r, k, v, kk, a, lw, *, ts, npair, nb):
    b, s, d = r.shape
    wd = npair * LANES
    spec = pl.BlockSpec((nb, ts, wd), lambda bi, p, i: (bi, i, p))
    return pl.pallas_call(
        _wkv_kernel,
        grid=(b // nb, d // wd, s // ts),
        in_specs=[spec] * 6,
        out_specs=spec,
        out_shape=jax.ShapeDtypeStruct((b, s, d), F32),
        scratch_shapes=[pltpu.VMEM((nb * npair, LANES, LANES), F32),
                        pltpu.VMEM((WKV_LEVELS + 3, LANES, LANES), F32)],
        compiler_params=_cparams(("parallel", "parallel", "arbitrary")),
        name="wkv7",
    )(r, k, v, kk, a, lw)


def _even_out_kernel(ya_ref, y_ref, bonus_ref, g_ref, lnw_ref, lnb_ref, w_ref, h_ref, o_ref):
    ka = ya_ref.shape[1]
    y = y_ref[...]
    inv_n = 1.0 / RWKV_HEAD
    s1 = _head_sum(y)
    s2 = _head_sum(y * y)
    acc = h_ref[...] + _dot(ya_ref[...], w_ref[0:ka, :])
    mean = s1 * inv_n
    var = jnp.maximum(s2 * inv_n - mean * mean, 0.0)
    yn = (y - mean) * lax.rsqrt(var + GN_EPS) * lnw_ref[...] + lnb_ref[...]
    yb = ((yn + bonus_ref[...].astype(F32)) * g_ref[...].astype(F32)).astype(BF16)
    o_ref[...] = acc + _dot(yb, w_ref[ka:, :])


def _even_out(ya, y, bonus, g, ln_w, ln_b, w, layer, h, *, tm):
    m, n = h.shape
    ka, kb = ya.shape[1], y.shape[1]
    blk = lambda width: pl.BlockSpec((tm, width), lambda i: (i, 0))
    row = pl.BlockSpec((1, kb), lambda i: (0, 0))
    return pl.pallas_call(
        _even_out_kernel,
        grid=(m // tm,),
        in_specs=[blk(ka), blk(kb), blk(kb), blk(kb), row, row,
                  pl.BlockSpec((None, ka + kb, n), lambda i: (layer, 0, 0)), blk(n)],
        out_specs=blk(n),
        out_shape=jax.ShapeDtypeStruct((m, n), F32),
        compiler_params=_cparams(("parallel",)),
        name="even_out",
    )(ya, y, bonus, g, ln_w, ln_b, w, h)


def _pool_kernel(u_ref, up_ref, w_ref, sc_ref, o_ref, buf_ref):
    ts = o_ref.shape[1]
    hist = max(POOL_WINDOWS)
    i = pl.program_id(1)
    u = u_ref[0].astype(F32)
    _fill_history(buf_ref, up_ref[0].astype(F32), u, hist, i == 0)
    pos = i * ts + lax.broadcasted_iota(jnp.int32, (ts, 1), 0)
    outs = []
    for gi, win in enumerate(POOL_WINDOWS):
        cols = pl.ds(gi * POOL_GROUP, POOL_GROUP)
        acc = buf_ref[pl.ds(hist, ts), cols]
        for j in range(1, win):
            acc = acc + buf_ref[pl.ds(hist - j, ts), cols]
        count = jnp.minimum(pos + 1, win).astype(F32)
        dg = acc / count - u[:, gi * POOL_GROUP:(gi + 1) * POOL_GROUP]
        outs.append(_dot(dg.astype(BF16), w_ref[gi]))
    y = jnp.concatenate(outs, axis=1) * sc_ref[...]
    o_ref[0] = y.astype(o_ref.dtype)


def _pool_mixer(p, pool_w, pool_scale, *, ts):
    b, s, _ = p.shape
    hist = max(POOL_WINDOWS)
    rb = ts // hist
    return pl.pallas_call(
        _pool_kernel,
        grid=(b, s // ts),
        in_specs=[
            pl.BlockSpec((1, ts, POOL_DIM), lambda bi, i: (bi, i, 0)),
            pl.BlockSpec((1, hist, POOL_DIM), lambda bi, i: (bi, jnp.maximum(i * rb - 1, 0), 0)),
            pl.BlockSpec((len(POOL_WINDOWS), POOL_GROUP, POOL_GROUP), lambda bi, i: (0, 0, 0)),
            pl.BlockSpec((1, POOL_DIM), lambda bi, i: (0, 0)),
        ],
        out_specs=pl.BlockSpec((1, ts, POOL_DIM), lambda bi, i: (bi, i, 0)),
        out_shape=jax.ShapeDtypeStruct((b, s, POOL_DIM), BF16),
        scratch_shapes=[pltpu.VMEM((ts + hist, POOL_DIM), F32)],
        compiler_params=_cparams(("parallel", "parallel")),
        name="pool_mixer",
    )(p, p, pool_w, pool_scale)


def _mla_prep_kernel(ql_ref, kvl_ref, kpe_ref, kper_ref, ang_ref, qn_ref, kvn_ref,
                     wq_ref, wqr_ref, wk_ref, wv_ref, q_out, k_out, v_out):
    cos = jnp.cos(ang_ref[...])
    sin = jnp.sin(ang_ref[...])
    qn = _rms(ql_ref[...].astype(F32), qn_ref[...]).astype(BF16)
    kvn = _rms(kvl_ref[...].astype(F32), kvn_ref[...]).astype(BF16)
    scale = (QK_NOPE + QK_ROPE) ** -0.5 * LOG2_E
    qa = _dot(qn, wq_ref[...])
    qr = _dot(qn, wqr_ref[...])
    ka = _dot(kvn, wk_ref[...])
    kpe = kpe_ref[...].astype(F32) * cos + kper_ref[...].astype(F32) * sin
    for h in range(MLA_HEADS):
        lo = h * QK_PAD
        q_out[:, lo:lo + LANES] = (qa[:, lo:lo + LANES] * scale).astype(BF16)
        q_pe = qa[:, lo + LANES:lo + 2 * LANES] * cos + qr[:, h * LANES:(h + 1) * LANES] * sin
        q_out[:, lo + LANES:lo + 2 * LANES] = (q_pe * scale).astype(BF16)
        k_out[:, lo:lo + LANES] = ka[:, h * QK_NOPE:(h + 1) * QK_NOPE].astype(BF16)
        k_out[:, lo + LANES:lo + 2 * LANES] = kpe.astype(BF16)
    vt = _dot_nt(wv_ref[...], kvn).astype(BF16)
    ones = jnp.ones((ATTN_SUM_ROWS, vt.shape[1]), BF16)
    for h in range(MLA_HEADS):
        v_out[h * VT_ROWS:h * VT_ROWS + V_HEAD, :] = vt[h * V_HEAD:(h + 1) * V_HEAD, :]
        v_out[h * VT_ROWS + V_HEAD:(h + 1) * VT_ROWS, :] = ones


def _mla_prep(p, ang, q_norm, kv_norm, wq, wqr, wk, wv, layer, *, tm):
    m = p.shape[0]
    hq = MLA_HEADS * QK_PAD
    hv = MLA_HEADS * VT_ROWS
    row = lambda a: pl.BlockSpec(a.shape, lambda i: (0, 0))
    full = lambda a: pl.BlockSpec((None,) + a.shape[1:], lambda i: (layer, 0, 0))
    return pl.pallas_call(
        _mla_prep_kernel,
        grid=(m // tm,),
        in_specs=[
            pl.BlockSpec((tm, Q_LORA), lambda i: (i, 1)),
            pl.BlockSpec((tm, KV_LORA), lambda i: (i, 2)),
            pl.BlockSpec((tm, LANES), lambda i: (i, 12)),
            pl.BlockSpec((tm, LANES), lambda i: (i, 13)),
            pl.BlockSpec((tm, LANES), lambda i: (i, 0)),
            row(q_norm), row(kv_norm), full(wq), full(wqr), full(wk), full(wv),
        ],
        out_specs=[pl.BlockSpec((tm, hq), lambda i: (i, 0)),
                   pl.BlockSpec((tm, hq), lambda i: (i, 0)),
                   pl.BlockSpec((hv, tm), lambda i: (0, i))],
        out_shape=[jax.ShapeDtypeStruct((m, hq), BF16),
                   jax.ShapeDtypeStruct((m, hq), BF16),
                   jax.ShapeDtypeStruct((hv, m), BF16)],
        compiler_params=_cparams(("parallel",)),
        name="mla_prep",
    )(p, p, p, p, ang, q_norm, kv_norm, wq, wqr, wk, wv)


def _attn_kernel(q_ref, k_ref, vt_ref, o_ref, *, tq, tk, nh):
    qi = pl.program_id(2)
    heads = range(nh)
    qs = [q_ref[0, :, h * QK_PAD:(h + 1) * QK_PAD] for h in heads]
    def keys(j):
        return pl.ds(pl.multiple_of(j * tk, tk), tk)

    def scores(j):
        return [_dot_nt(k_ref[0, keys(j), h * QK_PAD:(h + 1) * QK_PAD], qs[h]) for h in heads]

    def softmax_pv(j, s, m, acc, masked):
        if masked:
            key = j * tk + lax.broadcasted_iota(jnp.int32, (tk, tq), 0)
            qry = qi * tq + lax.broadcasted_iota(jnp.int32, (tk, tq), 1)
            s = [jnp.where(key <= qry, x, -1e30) for x in s]
        m_new = [jnp.maximum(a, jnp.max(x, axis=0, keepdims=True)) for a, x in zip(m, s)]
        alpha = [jnp.exp2(a - b) for a, b in zip(m, m_new)]
        p = [jnp.exp2(x - a).astype(BF16) for x, a in zip(s, m_new)]
        pv = [_dot(vt_ref[h * VT_ROWS:(h + 1) * VT_ROWS, keys(j)], p[h]) for h in heads]
        return m_new, [a * b + x for a, b, x in zip(alpha, acc, pv)]

    def two_blocks(j, m, acc, last_masked):
        s0, s1 = scores(j), scores(j + 1)
        m, acc = softmax_pv(j, s0, m, acc, False)
        return softmax_pv(j + 1, s1, m, acc, last_masked)

    def finish(acc):
        for h in heads:
            out = acc[h][0:V_HEAD] / acc[h][V_HEAD:V_HEAD + 1]
            o_ref[0, :, h * V_HEAD:(h + 1) * V_HEAD] = jnp.transpose(out).astype(o_ref.dtype)

    init = ([jnp.full((1, tq), -1e30, F32) for _ in heads],
            [jnp.zeros((V_HEAD + ATTN_SUM_ROWS, tq), F32) for _ in heads])
    pairs = qi // 2
    m, acc = lax.fori_loop(0, pairs, lambda i, c: two_blocks(2 * i, *c, False), init)

    @pl.when(qi % 2 == 1)
    def _():
        finish(two_blocks(qi - 1, m, acc, True)[1])

    @pl.when(qi % 2 == 0)
    def _():
        finish(softmax_pv(qi, scores(qi), m, acc, True)[1])


def _attention(q, k, vt, *, tq, tk, nh):
    b, s, _ = q.shape
    assert tq == tk, "one key block per query tile sits on the causal diagonal"
    return pl.pallas_call(
        functools.partial(_attn_kernel, tq=tq, tk=tk, nh=nh),
        grid=(b, MLA_HEADS // nh, s // tq),
        in_specs=[
            pl.BlockSpec((1, tq, nh * QK_PAD), lambda bi, h, i: (bi, i, h)),
            pl.BlockSpec((1, s, nh * QK_PAD), lambda bi, h, i: (bi, 0, h)),
            pl.BlockSpec((nh * VT_ROWS, s), lambda bi, h, i: (h, bi)),
        ],
        out_specs=pl.BlockSpec((1, tq, nh * V_HEAD), lambda bi, h, i: (bi, i, h)),
        out_shape=jax.ShapeDtypeStruct((b, s, MLA_HEADS * V_HEAD), BF16),
        compiler_params=_cparams(("parallel", "parallel", "arbitrary")),
        name="mla_attention",
    )(q, k, vt)


def _pad_last(w, n):
    return jnp.pad(w, [(0, 0)] * (w.ndim - 1) + [(0, n - w.shape[-1])])


def _pad_rows(w, n):
    return jnp.pad(w, ((0, n - w.shape[0]), (0, 0)))


def _rot_cols(w):
    half = w.shape[-1] // 2
    return jnp.concatenate([-w[..., half:], w[..., :half]], axis=-1)


def _even_weights(w_in, mu):
    c = 3 * CONV_DIM + 3 * RWKV_DIM
    split = lambda t, base: (t[..., base:base + W_LORA],
                             t[..., base + W_LORA:base + W_LORA + A_LORA],
                             t[..., base + W_LORA + A_LORA:])
    xw, xa, xg = split(w_in, c)
    w = jnp.concatenate([w_in[..., :c], _pad_last(xw, LANES), _pad_last(xa, LANES),
                         _pad_last(xg, 2 * LANES)], axis=-1).astype(BF16)
    r = 3 * RWKV_DIM
    mw, ma, mg = split(mu, r)
    mu_l = jnp.concatenate([_pad_last(mw, LANES), _pad_last(ma, LANES), _pad_last(mg, 2 * LANES)],
                           axis=-1)
    return w, mu[:, None, :r], mu_l[:, None, :]


def _odd_weights(w_in, w_uq, w_ukv):
    n = w_in.shape[0]
    o3 = POOL_DIM + Q_LORA + KV_LORA
    kpe = w_in[..., o3:]
    w = jnp.concatenate([w_in[..., :o3], _pad_last(kpe, LANES), _pad_last(_rot_cols(kpe), LANES)],
                        axis=-1).astype(BF16)
    uq = w_uq.reshape(n, Q_LORA, MLA_HEADS, QK_NOPE + QK_ROPE)
    wq = _pad_last(uq, QK_PAD).reshape(n, Q_LORA, MLA_HEADS * QK_PAD).astype(BF16)
    wqr = _pad_last(_rot_cols(uq[..., QK_NOPE:]), LANES)
    wqr = wqr.reshape(n, Q_LORA, MLA_HEADS * LANES).astype(BF16)
    ukv = w_ukv.reshape(n, KV_LORA, MLA_HEADS, QK_NOPE + V_HEAD)
    wk = ukv[..., :QK_NOPE].reshape(n, KV_LORA, MLA_HEADS * QK_NOPE).astype(BF16)
    wv = ukv[..., QK_NOPE:].reshape(n, KV_LORA, MLA_HEADS * V_HEAD)
    return w, wq, wqr, wk, jnp.swapaxes(wv, 1, 2).astype(BF16)


def _tile(n, pref):
    return pref if n % pref == 0 else n


def _tiles(b, s):
    m = b * s
    return dict(
        rows_wide=_tile(m, 1024),
        rows=_tile(m, 512),
        ffn_cols=512,
        even_in_cols=EV_COLS // 4,
        odd_in_cols=OD_COLS // 2,
        seq_mixers=_tile(s, 1024),
        seq_prep=_tile(s, 256),
        seq_wkv=_tile(s, 256),
        wkv_pairs=RWKV_DIM // LANES,
        wkv_batch=2 if b % 2 == 0 else 1,
        attn=_tile(s, 512),
        attn_heads=4,
    )


def kernel(x, positions, ev_norm, ev_w_in, ev_conv_w, ev_mu, ev_w0, ev_w2, ev_a0, ev_a2, ev_g2, ev_k_k, ev_k_a, ev_r_k, ev_ln_w, ev_ln_b, ev_w_out, od_norm, od_w_in, od_pool_w, od_pool_scale, od_q_norm, od_w_uq, od_kv_norm, od_w_ukv, od_w_out, ffn_norm, ffn_w_gate, ffn_w_up, ffn_w_down, final_norm):
    b, s, d = x.shape
    m = b * s
    depth = ffn_norm.shape[0]
    t = _tiles(b, s)

    inv = 1.0 / (ROPE_THETA ** (jnp.arange(0, QK_ROPE, 2, dtype=F32) / QK_ROPE))
    inv = jnp.concatenate([inv, inv, jnp.zeros((LANES - QK_ROPE,), F32)])
    ang = (positions.astype(F32)[..., None] * inv).reshape(m, LANES)

    ev_w, ev_mu_main, ev_mu_lora = _even_weights(ev_w_in, ev_mu)
    od_w, wq, wqr, wk, wv = _odd_weights(od_w_in, od_w_uq, od_w_ukv)
    ev_wo = ev_w_out.astype(BF16)
    od_wo = od_w_out.astype(BF16)
    wg, wu, wd = ffn_w_gate.astype(BF16), ffn_w_up.astype(BF16), ffn_w_down.astype(BF16)
    row = lambda t: t[None, :]

    h = x.reshape(m, d)
    for layer in range(depth):
        j = layer // 2
        if layer % 2 == 0:
            p = _norm_matmul(h, row(ev_norm[j]), ev_w, j, tm=t["rows_wide"], tn=t["even_in_cols"],
                             out_dtype=BF16)
            p = p.reshape(b, s, EV_COLS)
            ya = _conv_mixer(p, ev_conv_w[j].T, ts=t["seq_mixers"], tc=CONV_DIM)
            r, k, v, kk, a, lw, g, bonus = _rwkv_prep(
                p, ev_mu_main[j], ev_mu_lora[j], row(ev_w0[j]),
                _pad_rows(ev_w2[j], LANES).astype(BF16), row(ev_a0[j]),
                _pad_rows(ev_a2[j], LANES).astype(BF16), _pad_rows(ev_g2[j], 2 * LANES).astype(BF16),
                row(ev_k_k[j]), row(ev_k_a[j]), row(ev_r_k[j].reshape(-1)), ts=t["seq_prep"])
            y = _wkv(r, k, v, kk, a, lw, ts=t["seq_wkv"], npair=t["wkv_pairs"], nb=t["wkv_batch"])
            h = _even_out(ya.reshape(m, -1), y.reshape(m, -1), bonus.reshape(m, -1),
                          g.reshape(m, -1), row(ev_ln_w[j]), row(ev_ln_b[j]), ev_wo, j, h,
                          tm=t["rows"])
        else:
            p = _norm_matmul(h, row(od_norm[j]), od_w, j, tm=t["rows_wide"], tn=t["odd_in_cols"],
                             out_dtype=BF16)
            yc = _pool_mixer(p.reshape(b, s, OD_COLS), od_pool_w[j].astype(BF16),
                             row(od_pool_scale[j]), ts=t["seq_mixers"])
            q, kq, vv = _mla_prep(p, ang, row(od_q_norm[j]), row(od_kv_norm[j]),
                                  wq, wqr, wk, wv, j, tm=t["rows"])
            hq = MLA_HEADS * QK_PAD
            yd = _attention(q.reshape(b, s, hq), kq.reshape(b, s, hq), vv, tq=t["attn"],
                            tk=t["attn"], nh=t["attn_heads"])
            h = _proj_residual(yc.reshape(m, -1), yd.reshape(m, -1), od_wo, j, h, tm=t["rows"], tn=d)
        h = _ffn(h, row(ffn_norm[layer]), row(final_norm), wg, wu, wd, layer, tm=t["rows_wide"],
                 tf=t["ffn_cols"], norm_out=layer == depth - 1)
    return h.reshape(b, s, d)
```

```python
import functools

import jax
import jax.numpy as jnp
from jax import lax
from jax.experimental import pallas as pl
from jax.experimental.pallas import tpu as pltpu

F32 = jnp.float32
BF16 = jnp.bfloat16

NORM_EPS = 1e-6
CONV_DIM = 1024
RWKV_DIM = 1024
RWKV_HEAD = 64
W_LORA = 64
A_LORA = 64
GN_EPS = 64e-5
POOL_WINDOWS = (2, 4, 8, 16)
POOL_GROUP = 128
POOL_DIM = 512
MLA_HEADS = 12
Q_LORA = 512
KV_LORA = 512
QK_NOPE = 128
QK_ROPE = 64
V_HEAD = 128
ROPE_THETA = 10000.0
LOG2_E = 1.4426950408889634
EXP_NEG_HALF = 0.6065306597126334

LANES = 128
HIST = 16
VMEM_LIMIT = 60 * 1024 * 1024
QK_PAD = 256
ATTN_SUM_ROWS = 16
VT_ROWS = V_HEAD + ATTN_SUM_ROWS
WKV_CHUNK = 64
EV_COLS = 6656
EV_LORA = 512
OD_COLS = 1792


def _cparams(sem):
    return pltpu.CompilerParams(dimension_semantics=sem, vmem_limit_bytes=VMEM_LIMIT)


def _rms(x, g):
    ms = jnp.mean(x * x, axis=-1, keepdims=True)
    return x * lax.rsqrt(ms + NORM_EPS) * g


def _sigmoid(x):
    return 1.0 / (1.0 + jnp.exp(-x))


def _dot(a, b):
    return jnp.dot(a, b, preferred_element_type=F32)


def _dot_nt(a, b):
    return lax.dot_general(a, b, (((1,), (1,)), ((), ())), preferred_element_type=F32)


def _dot_tn(a, b):
    return lax.dot_general(a, b, (((0,), (0,)), ((), ())), preferred_element_type=F32)


def _norm_mm_kernel(x_ref, g_ref, w_ref, o_ref, xn_ref):
    @pl.when(pl.program_id(1) == 0)
    def _():
        xn_ref[...] = _rms(x_ref[...], g_ref[...]).astype(BF16)

    o_ref[...] = _dot(xn_ref[...], w_ref[...]).astype(o_ref.dtype)


def _norm_matmul(x, g, w, layer, *, tm, tn, out_dtype):
    m, k = x.shape
    n = w.shape[2]
    return pl.pallas_call(
        _norm_mm_kernel,
        grid=(m // tm, n // tn),
        in_specs=[
            pl.BlockSpec((tm, k), lambda i, j: (i, 0)),
            pl.BlockSpec((1, k), lambda i, j: (0, 0)),
            pl.BlockSpec((None, k, tn), lambda i, j: (layer, 0, j)),
        ],
        out_specs=pl.BlockSpec((tm, tn), lambda i, j: (i, j)),
        out_shape=jax.ShapeDtypeStruct((m, n), out_dtype),
        scratch_shapes=[pltpu.VMEM((tm, k), BF16)],
        compiler_params=_cparams(("parallel", "arbitrary")),
        name="norm_matmul",
    )(x, g, w)


def _proj_res_kernel(a_ref, b_ref, w_ref, h_ref, o_ref):
    ka = a_ref.shape[1]
    o_ref[...] = h_ref[...] + _dot(a_ref[...], w_ref[0:ka, :]) + _dot(b_ref[...], w_ref[ka:, :])


def _proj_residual(a, b, w, layer, h, *, tm, tn):
    m, n = h.shape
    ka, kb = a.shape[1], b.shape[1]
    return pl.pallas_call(
        _proj_res_kernel,
        grid=(m // tm, n // tn),
        in_specs=[
            pl.BlockSpec((tm, ka), lambda i, j: (i, 0)),
            pl.BlockSpec((tm, kb), lambda i, j: (i, 0)),
            pl.BlockSpec((None, ka + kb, tn), lambda i, j: (layer, 0, j)),
            pl.BlockSpec((tm, tn), lambda i, j: (i, j)),
        ],
        out_specs=pl.BlockSpec((tm, tn), lambda i, j: (i, j)),
        out_shape=jax.ShapeDtypeStruct((m, n), F32),
        compiler_params=_cparams(("parallel", "arbitrary")),
        name="proj_residual",
    )(a, b, w, h)


def _ffn_kernel(x_ref, g_ref, gout_ref, wg_ref, wu_ref, wd_ref, o_ref, xn_ref, *, norm_out):
    @pl.when(pl.program_id(1) == 0)
    def _():
        x = x_ref[...]
        xn_ref[...] = _rms(x, g_ref[...]).astype(BF16)
        o_ref[...] = x

    xn = xn_ref[...]
    a = _dot(xn, wg_ref[...])
    b = _dot(xn, wu_ref[...])
    act = (a * _sigmoid(a) * b).astype(BF16)
    o_ref[...] += _dot(act, wd_ref[...].astype(BF16))

    if norm_out:
        @pl.when(pl.program_id(1) == pl.num_programs(1) - 1)
        def _():
            o_ref[...] = _rms(o_ref[...], gout_ref[...])


def _ffn(x, g, g_out, wg, wu, wd, layer, *, tm, tf, norm_out):
    m, d = x.shape
    f = wg.shape[2]
    return pl.pallas_call(
        functools.partial(_ffn_kernel, norm_out=norm_out),
        grid=(m // tm, f // tf),
        in_specs=[
            pl.BlockSpec((tm, d), lambda i, j: (i, 0)),
            pl.BlockSpec((1, d), lambda i, j: (0, 0)),
            pl.BlockSpec((1, d), lambda i, j: (0, 0)),
            pl.BlockSpec((None, d, tf), lambda i, j: (layer, 0, j)),
            pl.BlockSpec((None, d, tf), lambda i, j: (layer, 0, j)),
            pl.BlockSpec((None, tf, d), lambda i, j: (layer, j, 0)),
        ],
        out_specs=pl.BlockSpec((tm, d), lambda i, j: (i, 0)),
        out_shape=jax.ShapeDtypeStruct((m, d), F32),
        scratch_shapes=[pltpu.VMEM((tm, d), BF16)],
        compiler_params=_cparams(("parallel", "arbitrary")),
        name="ffn",
    )(x, g, g_out, wg, wu, wd)


def _fill_history(buf_ref, prev, cur, hist, is_first):
    buf_ref[pl.ds(0, hist), :] = jnp.where(is_first, 0.0, prev)
    buf_ref[pl.ds(hist, cur.shape[0]), :] = cur


def _conv_kernel(bg_ref, cg_ref, hh_ref, cgp_ref, hhp_ref, w_ref, o_ref, buf_ref):
    ts = o_ref.shape[1]
    u = cg_ref[0].astype(F32) * hh_ref[0].astype(F32)
    up = cgp_ref[0].astype(F32) * hhp_ref[0].astype(F32)
    _fill_history(buf_ref, up, u, HIST, pl.program_id(1) == 0)
    u1 = buf_ref[pl.ds(HIST - 1, ts), :]
    u2 = buf_ref[pl.ds(HIST - 2, ts), :]
    w = w_ref[...]
    y = w[2:3] * u + w[1:2] * u1 + w[0:1] * u2
    o_ref[0] = (bg_ref[0].astype(F32) * y).astype(o_ref.dtype)


def _conv_mixer(p, conv_w_t, *, ts, tc):
    b, s, _ = p.shape
    nc = CONV_DIM // tc
    rb = ts // HIST

    def cur(off):
        return pl.BlockSpec((1, ts, tc), lambda bi, i, c: (bi, i, off * nc + c))

    def prev(off):
        return pl.BlockSpec((1, HIST, tc),
                            lambda bi, i, c: (bi, jnp.maximum(i * rb - 1, 0), off * nc + c))

    return pl.pallas_call(
        _conv_kernel,
        grid=(b, s // ts, nc),
        in_specs=[cur(0), cur(1), cur(2), prev(1), prev(2),
                  pl.BlockSpec((3, tc), lambda bi, i, c: (0, c))],
        out_specs=pl.BlockSpec((1, ts, tc), lambda bi, i, c: (bi, i, c)),
        out_shape=jax.ShapeDtypeStruct((b, s, CONV_DIM), BF16),
        scratch_shapes=[pltpu.VMEM((ts + HIST, tc), F32)],
        compiler_params=_cparams(("parallel", "parallel", "parallel")),
        name="conv_mixer",
    )(p, p, p, p, p, conv_w_t)


def _head_sum(x, split=True):
    r = lax.broadcasted_iota(jnp.int32, (LANES, LANES), 0) // RWKV_HEAD
    c = lax.broadcasted_iota(jnp.int32, (LANES, LANES), 1) // RWKV_HEAD
    ones = jnp.where(r == c, 1.0, 0.0).astype(BF16)
    hi = x.astype(BF16)
    lo = (x - hi.astype(F32)).astype(BF16) if split else None
    parts = []
    for g in range(x.shape[1] // LANES):
        sl = slice(g * LANES, (g + 1) * LANES)
        part = _dot(hi[:, sl], ones)
        parts.append(part + _dot(lo[:, sl], ones) if split else part)
    return jnp.concatenate(parts, axis=1)


def _token_shift(xb, prev_rows, is_first):
    ts = xb.shape[0]
    r = lax.broadcasted_iota(jnp.int32, (ts, ts), 0)
    c = lax.broadcasted_iota(jnp.int32, (ts, ts), 1)
    xs = _dot(jnp.where(r == c + 1, 1.0, 0.0).astype(BF16), xb)
    hist = prev_rows.shape[0]
    last = jnp.where(is_first, 0.0, prev_rows[hist - 1:hist, :].astype(F32))
    top = jnp.where(lax.broadcasted_iota(jnp.int32, (HIST, 1), 0) == 0, last, xs[0:HIST])
    return jnp.concatenate([top, xs[HIST:]], axis=0)


def _rwkv_prep_kernel(x_ref, xl_ref, xp_ref, xlp_ref, mu_ref, mul_ref, w0_ref, w2_ref, a0_ref,
                      a2_ref, g2_ref, kk_ref, ka_ref, rk_ref,
                      r_out, k_out, v_out, kkn_out, a_out, lw_out, g_out, bonus_out):
    first = pl.program_id(1) == 0
    d = RWKV_DIM

    x = x_ref[0].astype(F32)
    xm = x + (_token_shift(x_ref[0], xp_ref[0], first) - x) * mu_ref[...]
    xl = xl_ref[0].astype(F32)
    xlm = xl + (_token_shift(xl_ref[0], xlp_ref[0], first) - xl) * mul_ref[...]

    r = xm[:, 0:d]
    k = xm[:, d:2 * d]
    v = xm[:, 2 * d:3 * d]
    xw = xlm[:, 0:LANES]
    xa = xlm[:, LANES:2 * LANES]
    xg = xlm[:, 2 * LANES:4 * LANES]

    y = w0_ref[...] + _dot(jnp.tanh(xw).astype(BF16), w2_ref[...])
    lw = -EXP_NEG_HALF * _sigmoid(y)
    a = _sigmoid(a0_ref[...] + _dot(xa.astype(BF16), a2_ref[...]))
    g = _dot(_sigmoid(xg).astype(BF16), g2_ref[...])

    kk = k * kk_ref[...]
    kk = kk * lax.rsqrt(jnp.maximum(_head_sum(kk * kk, split=False), 1e-24))
    k = k * (1.0 + (a - 1.0) * ka_ref[...])
    bonus = _head_sum(r * k * rk_ref[...], split=False) * v

    r_out[0] = r.astype(r_out.dtype)
    k_out[0] = k.astype(k_out.dtype)
    v_out[0] = v.astype(v_out.dtype)
    kkn_out[0] = kk.astype(kkn_out.dtype)
    a_out[0] = a.astype(a_out.dtype)
    lw_out[0] = lw
    g_out[0] = g.astype(g_out.dtype)
    bonus_out[0] = bonus.astype(bonus_out.dtype)


def _rwkv_prep(p, mu, mul, w0, w2, a0, a2, g2, k_k, k_a, r_k, *, ts):
    b, s, _ = p.shape
    d = RWKV_DIM
    rb = ts // HIST
    row = lambda shape: pl.BlockSpec(shape, lambda bi, i: (0, 0))
    out = pl.BlockSpec((1, ts, d), lambda bi, i: (bi, i, 0))
    return pl.pallas_call(
        _rwkv_prep_kernel,
        grid=(b, s // ts),
        in_specs=[
            pl.BlockSpec((1, ts, 3 * d), lambda bi, i: (bi, i, 1)),
            pl.BlockSpec((1, ts, EV_LORA), lambda bi, i: (bi, i, 6 * d // EV_LORA)),
            pl.BlockSpec((1, HIST, 3 * d), lambda bi, i: (bi, jnp.maximum(i * rb - 1, 0), 1)),
            pl.BlockSpec((1, HIST, EV_LORA),
                         lambda bi, i: (bi, jnp.maximum(i * rb - 1, 0), 6 * d // EV_LORA)),
            row((1, 3 * d)), row((1, EV_LORA)), row((1, d)), row((LANES, d)), row((1, d)),
            row((LANES, d)), row((2 * LANES, d)), row((1, d)), row((1, d)), row((1, d)),
        ],
        out_specs=[out] * 8,
        out_shape=[jax.ShapeDtypeStruct((b, s, d), BF16)] * 5
        + [jax.ShapeDtypeStruct((b, s, d), F32)] + [jax.ShapeDtypeStruct((b, s, d), BF16)] * 2,
        compiler_params=_cparams(("parallel", "parallel")),
        name="rwkv_prep",
    )(p, p, p, p, mu, mul, w0, w2, a0, a2, g2, k_k, k_a, r_k)


WKV_LEVELS = 6
_MASK_STRICT = WKV_LEVELS
_MASK_INCL = WKV_LEVELS + 1
_MASK_EYE = WKV_LEVELS + 2


def _wkv_kernel(r_ref, k_ref, v_ref, kk_ref, a_ref, lw_ref, y_ref, t_ref, mask_ref):
    n = WKV_CHUNK
    nb, ts = y_ref.shape[0], y_ref.shape[1]
    npair = y_ref.shape[2] // LANES

    @pl.when(pl.program_id(2) == 0)
    def _():
        t_ref[...] = jnp.zeros_like(t_ref)
        row = lax.broadcasted_iota(jnp.int32, (2 * n, 2 * n), 0)
        col = lax.broadcasted_iota(jnp.int32, (2 * n, 2 * n), 1)
        same = (row // n) == (col // n)
        x = row ^ col
        level = jnp.zeros_like(x)
        for bit in range(1, WKV_LEVELS):
            level = level + jnp.where(x >= (1 << bit), 1, 0)
        level = jnp.where(same & (row > col), level, -1)
        for lev in range(WKV_LEVELS):
            mask_ref[lev] = jnp.where(level == lev, 1.0, 0.0)
        mask_ref[_MASK_STRICT] = jnp.where(level >= 0, 1.0, 0.0)
        mask_ref[_MASK_INCL] = jnp.where(same & (row >= col), 1.0, 0.0)
        mask_ref[_MASK_EYE] = jnp.where(row == col, 1.0, 0.0)

    head0 = lax.broadcasted_iota(jnp.int32, (n, LANES), 1) < RWKV_HEAD
    tri = jnp.where(lax.broadcasted_iota(jnp.int32, (n, n), 0)
                    >= lax.broadcasted_iota(jnp.int32, (n, n), 1), 1.0, 0.0).astype(BF16)

    def stack(t):
        z = jnp.zeros_like(t)
        return jnp.concatenate([jnp.where(head0, t, z), jnp.where(head0, z, t)], axis=0).astype(BF16)

    def operands(sl, bi, q, cw):
        lanes = slice(q * LANES, (q + 1) * LANES)
        r = r_ref[bi, sl, lanes].astype(F32)
        k = k_ref[bi, sl, lanes].astype(F32)
        kk = kk_ref[bi, sl, lanes].astype(F32)
        lw = lw_ref[bi, sl, lanes]
        cw_end = cw[n - 1:n, :]
        p_inv = jnp.exp(-cw)
        p_to_end = jnp.exp(cw_end - cw)
        kka = kk * a_ref[bi, sl, lanes].astype(F32)
        xar = jnp.concatenate([stack(-kk * jnp.exp(cw - lw)), stack(r * jnp.exp(cw))], axis=0)
        ybk = jnp.concatenate([stack(kka * p_inv), stack(k * p_inv)], axis=0)
        return dict(xar=xar, ybk=ybk, bt=stack(kka * p_to_end), kt=stack(k * p_to_end),
                    vs=stack(v_ref[bi, sl, lanes].astype(F32)), decay=jnp.exp(cw_end))

    def cumulative(sl, bi):
        lw = lw_ref[bi, sl, :]
        hi = lw.astype(BF16)
        r1 = lw - hi.astype(F32)
        mid = r1.astype(BF16)
        lo = (r1 - mid.astype(F32)).astype(BF16)
        return _dot(tri, hi) + _dot(tri, mid) + _dot(tri, lo)

    def chunk(c, carry):
        sl = pl.ds(pl.multiple_of(c * n, n), n)
        cw = [cumulative(sl, bi) for bi in range(nb)]
        strict = mask_ref[_MASK_STRICT] > 0.5
        incl = mask_ref[_MASK_INCL] > 0.5

        def scores(group):
            ops = [operands(sl, bi, q, cw[bi][:, q * LANES:(q + 1) * LANES]) for bi, q in group]
            sc = [_dot_nt(o["xar"], o["ybk"]) for o in ops]
            for o, s in zip(ops, sc):
                o["a_ab"] = jnp.where(strict, s[0:2 * n, 0:2 * n], 0.0)
                o["a_k"] = jnp.concatenate([jnp.where(strict, s[0:2 * n, 2 * n:4 * n], 0.0),
                                            jnp.where(incl, s[2 * n:4 * n, 2 * n:4 * n], 0.0)],
                                           axis=0).astype(BF16)
                o["a_rb"] = jnp.where(incl, s[2 * n:4 * n, 0:2 * n], 0.0).astype(BF16)
            return ops

        def inverse(ops):
            m = [mask_ref[_MASK_EYE] + o["a_ab"] * mask_ref[0] for o in ops]
            for lev in range(1, WKV_LEVELS):
                mb = [x.astype(BF16) for x in m]
                em = [_dot((o["a_ab"] * mask_ref[lev]).astype(BF16), x).astype(BF16)
                      for o, x in zip(ops, mb)]
                m = [x + _dot(xb, y) for x, xb, y in zip(m, mb, em)]
            return m

        def finish(group, ops, m):
            t = [t_ref[bi * npair + q] for bi, q in group]
            xtv = [_dot(jnp.concatenate([o["xar"], o["a_k"]], axis=1),
                        jnp.concatenate([x.astype(BF16), o["vs"]], axis=0)) for o, x in zip(ops, t)]
            u = [_dot(x.astype(BF16), y[0:2 * n].astype(BF16)).astype(BF16) for x, y in zip(m, xtv)]
            for i, (bi, q) in enumerate(group):
                o = ops[i]
                ys = xtv[i][2 * n:4 * n] + _dot(o["a_rb"], u[i])
                y_ref[bi, sl, q * LANES:(q + 1) * LANES] = ys[0:n] + ys[n:2 * n]
                t_ref[bi * npair + q] = \
                    t[i] * jnp.transpose(jnp.broadcast_to(o["decay"], (LANES, LANES))) \
                    + _dot_tn(jnp.concatenate([o["bt"], o["kt"]], axis=0),
                              jnp.concatenate([u[i], o["vs"]], axis=0))

        group = [(bi, q) for bi in range(nb) for q in range(npair)]
        ops = scores(group)
        finish(group, ops, inverse(ops))
        return carry

    lax.fori_loop(0, ts // n, chunk, 0)


def _wkv(r, k, v, kk, a, lw, *, ts, npair, nb):
    b, s, d = r.shape
    wd = npair * LANES
    spec = pl.BlockSpec((nb, ts, wd), lambda bi, p, i: (bi, i, p))
    return pl.pallas_call(
        _wkv_kernel,
        grid=(b // nb, d // wd, s // ts),
        in_specs=[spec] * 6,
        out_specs=spec,
        out_shape=jax.ShapeDtypeStruct((b, s, d), F32),
        scratch_shapes=[pltpu.VMEM((nb * npair, LANES, LANES), F32),
                        pltpu.VMEM((WKV_LEVELS + 3, LANES, LANES), F32)],
        compiler_params=_cparams(("parallel", "parallel", "arbitrary")),
        name="wkv7",
    )(r, k, v, kk, a, lw)


def _even_out_kernel(ya_ref, y_ref, bonus_ref, g_ref, lnw_ref, lnb_ref, w_ref, h_ref, o_ref):
    ka = ya_ref.shape[1]
    y = y_ref[...]
    inv_n = 1.0 / RWKV_HEAD
    s1 = _head_sum(y)
    s2 = _head_sum(y * y)
    acc = h_ref[...] + _dot(ya_ref[...], w_ref[0:ka, :])
    mean = s1 * inv_n
    var = jnp.maximum(s2 * inv_n - mean * mean, 0.0)
    yn = (y - mean) * lax.rsqrt(var + GN_EPS) * lnw_ref[...] + lnb_ref[...]
    yb = ((yn + bonus_ref[...].astype(F32)) * g_ref[...].astype(F32)).astype(BF16)
    o_ref[...] = acc + _dot(yb, w_ref[ka:, :])


def _even_out(ya, y, bonus, g, ln_w, ln_b, w, layer, h, *, tm):
    m, n = h.shape
    ka, kb = ya.shape[1], y.shape[1]
    blk = lambda width: pl.BlockSpec((tm, width), lambda i: (i, 0))
    row = pl.BlockSpec((1, kb), lambda i: (0, 0))
    return pl.pallas_call(
        _even_out_kernel,
        grid=(m // tm,),
        in_specs=[blk(ka), blk(kb), blk(kb), blk(kb), row, row,
                  pl.BlockSpec((None, ka + kb, n), lambda i: (layer, 0, 0)), blk(n)],
        out_specs=blk(n),
        out_shape=jax.ShapeDtypeStruct((m, n), F32),
        compiler_params=_cparams(("parallel",)),
        name="even_out",
    )(ya, y, bonus, g, ln_w, ln_b, w, h)


def _pool_kernel(u_ref, up_ref, w_ref, sc_ref, o_ref, buf_ref):
    ts = o_ref.shape[1]
    hist = max(POOL_WINDOWS)
    i = pl.program_id(1)
    u = u_ref[0].astype(F32)
    _fill_history(buf_ref, up_ref[0].astype(F32), u, hist, i == 0)
    pos = i * ts + lax.broadcasted_iota(jnp.int32, (ts, 1), 0)
    outs = []
    for gi, win in enumerate(POOL_WINDOWS):
        cols = pl.ds(gi * POOL_GROUP, POOL_GROUP)
        acc = buf_ref[pl.ds(hist, ts), cols]
        for j in range(1, win):
            acc = acc + buf_ref[pl.ds(hist - j, ts), cols]
        count = jnp.minimum(pos + 1, win).astype(F32)
        dg = acc / count - u[:, gi * POOL_GROUP:(gi + 1) * POOL_GROUP]
        outs.append(_dot(dg.astype(BF16), w_ref[gi]))
    y = jnp.concatenate(outs, axis=1) * sc_ref[...]
    o_ref[0] = y.astype(o_ref.dtype)


def _pool_mixer(p, pool_w, pool_scale, *, ts):
    b, s, _ = p.shape
    hist = max(POOL_WINDOWS)
    rb = ts // hist
    return pl.pallas_call(
        _pool_kernel,
        grid=(b, s // ts),
        in_specs=[
            pl.BlockSpec((1, ts, POOL_DIM), lambda bi, i: (bi, i, 0)),
            pl.BlockSpec((1, hist, POOL_DIM), lambda bi, i: (bi, jnp.maximum(i * rb - 1, 0), 0)),
            pl.BlockSpec((len(POOL_WINDOWS), POOL_GROUP, POOL_GROUP), lambda bi, i: (0, 0, 0)),
            pl.BlockSpec((1, POOL_DIM), lambda bi, i: (0, 0)),
        ],
        out_specs=pl.BlockSpec((1, ts, POOL_DIM), lambda bi, i: (bi, i, 0)),
        out_shape=jax.ShapeDtypeStruct((b, s, POOL_DIM), BF16),
        scratch_shapes=[pltpu.VMEM((ts + hist, POOL_DIM), F32)],
        compiler_params=_cparams(("parallel", "parallel")),
        name="pool_mixer",
    )(p, p, pool_w, pool_scale)


def _mla_prep_kernel(ql_ref, kvl_ref, kpe_ref, kper_ref, ang_ref, qn_ref, kvn_ref,
                     wq_ref, wqr_ref, wk_ref, wv_ref, q_out, k_out, v_out):
    cos = jnp.cos(ang_ref[...])
    sin = jnp.sin(ang_ref[...])
    qn = _rms(ql_ref[...].astype(F32), qn_ref[...]).astype(BF16)
    kvn = _rms(kvl_ref[...].astype(F32), kvn_ref[...]).astype(BF16)
    scale = (QK_NOPE + QK_ROPE) ** -0.5 * LOG2_E
    qa = _dot(qn, wq_ref[...])
    qr = _dot(qn, wqr_ref[...])
    ka = _dot(kvn, wk_ref[...])
    kpe = kpe_ref[...].astype(F32) * cos + kper_ref[...].astype(F32) * sin
    for h in range(MLA_HEADS):
        lo = h * QK_PAD
        q_out[:, lo:lo + LANES] = (qa[:, lo:lo + LANES] * scale).astype(BF16)
        q_pe = qa[:, lo + LANES:lo + 2 * LANES] * cos + qr[:, h * LANES:(h + 1) * LANES] * sin
        q_out[:, lo + LANES:lo + 2 * LANES] = (q_pe * scale).astype(BF16)
        k_out[:, lo:lo + LANES] = ka[:, h * QK_NOPE:(h + 1) * QK_NOPE].astype(BF16)
        k_out[:, lo + LANES:lo + 2 * LANES] = kpe.astype(BF16)
    vt = _dot_nt(wv_ref[...], kvn).astype(BF16)
    ones = jnp.ones((ATTN_SUM_ROWS, vt.shape[1]), BF16)
    for h in range(MLA_HEADS):
        v_out[h * VT_ROWS:h * VT_ROWS + V_HEAD, :] = vt[h * V_HEAD:(h + 1) * V_HEAD, :]
        v_out[h * VT_ROWS + V_HEAD:(h + 1) * VT_ROWS, :] = ones


def _mla_prep(p, ang, q_norm, kv_norm, wq, wqr, wk, wv, layer, *, tm):
    m = p.shape[0]
    hq = MLA_HEADS * QK_PAD
    hv = MLA_HEADS * VT_ROWS
    row = lambda a: pl.BlockSpec(a.shape, lambda i: (0, 0))
    full = lambda a: pl.BlockSpec((None,) + a.shape[1:], lambda i: (layer, 0, 0))
    return pl.pallas_call(
        _mla_prep_kernel,
        grid=(m // tm,),
        in_specs=[
            pl.BlockSpec((tm, Q_LORA), lambda i: (i, 1)),
            pl.BlockSpec((tm, KV_LORA), lambda i: (i, 2)),
            pl.BlockSpec((tm, LANES), lambda i: (i, 12)),
            pl.BlockSpec((tm, LANES), lambda i: (i, 13)),
            pl.BlockSpec((tm, LANES), lambda i: (i, 0)),
            row(q_norm), row(kv_norm), full(wq), full(wqr), full(wk), full(wv),
        ],
        out_specs=[pl.BlockSpec((tm, hq), lambda i: (i, 0)),
                   pl.BlockSpec((tm, hq), lambda i: (i, 0)),
                   pl.BlockSpec((hv, tm), lambda i: (0, i))],
        out_shape=[jax.ShapeDtypeStruct((m, hq), BF16),
                   jax.ShapeDtypeStruct((m, hq), BF16),
                   jax.ShapeDtypeStruct((hv, m), BF16)],
        compiler_params=_cparams(("parallel",)),
        name="mla_prep",
    )(p, p, p, p, ang, q_norm, kv_norm, wq, wqr, wk, wv)


def _attn_kernel(q_ref, k_ref, vt_ref, o_ref, *, tq, tk, nh):
    qi = pl.program_id(2)
    heads = range(nh)
    qs = [q_ref[0, :, h * QK_PAD:(h + 1) * QK_PAD] for h in heads]
    def keys(j):
        return pl.ds(pl.multiple_of(j * tk, tk), tk)

    def scores(j):
        return [_dot_nt(k_ref[0, keys(j), h * QK_PAD:(h + 1) * QK_PAD], qs[h]) for h in heads]

    def softmax_pv(j, s, m, acc, masked):
        if masked:
            key = j * tk + lax.broadcasted_iota(jnp.int32, (tk, tq), 0)
            qry = qi * tq + lax.broadcasted_iota(jnp.int32, (tk, tq), 1)
            s = [jnp.where(key <= qry, x, -1e30) for x in s]
        m_new = [jnp.maximum(a, jnp.max(x, axis=0, keepdims=True)) for a, x in zip(m, s)]
        alpha = [jnp.exp2(a - b) for a, b in zip(m, m_new)]
        p = [jnp.exp2(x - a).astype(BF16) for x, a in zip(s, m_new)]
        pv = [_dot(vt_ref[h * VT_ROWS:(h + 1) * VT_ROWS, keys(j)], p[h]) for h in heads]
        return m_new, [a * b + x for a, b, x in zip(alpha, acc, pv)]

    def two_blocks(j, m, acc, last_masked):
        s0, s1 = scores(j), scores(j + 1)
        m, acc = softmax_pv(j, s0, m, acc, False)
        return softmax_pv(j + 1, s1, m, acc, last_masked)

    def finish(acc):
        for h in heads:
            out = acc[h][0:V_HEAD] / acc[h][V_HEAD:V_HEAD + 1]
            o_ref[0, :, h * V_HEAD:(h + 1) * V_HEAD] = jnp.transpose(out).astype(o_ref.dtype)

    init = ([jnp.full((1, tq), -1e30, F32) for _ in heads],
            [jnp.zeros((V_HEAD + ATTN_SUM_ROWS, tq), F32) for _ in heads])
    pairs = qi // 2
    m, acc = lax.fori_loop(0, pairs, lambda i, c: two_blocks(2 * i, *c, False), init)

    @pl.when(qi % 2 == 1)
    def _():
        finish(two_blocks(qi - 1, m, acc, True)[1])

    @pl.when(qi % 2 == 0)
    def _():
        finish(softmax_pv(qi, scores(qi), m, acc, True)[1])


def _attention(q, k, vt, *, tq, tk, nh):
    b, s, _ = q.shape
    assert tq == tk, "one key block per query tile sits on the causal diagonal"
    return pl.pallas_call(
        functools.partial(_attn_kernel, tq=tq, tk=tk, nh=nh),
        grid=(b, MLA_HEADS // nh, s // tq),
        in_specs=[
            pl.BlockSpec((1, tq, nh * QK_PAD), lambda bi, h, i: (bi, i, h)),
            pl.BlockSpec((1, s, nh * QK_PAD), lambda bi, h, i: (bi, 0, h)),
            pl.BlockSpec((nh * VT_ROWS, s), lambda bi, h, i: (h, bi)),
        ],
        out_specs=pl.BlockSpec((1, tq, nh * V_HEAD), lambda bi, h, i: (bi, i, h)),
        out_shape=jax.ShapeDtypeStruct((b, s, MLA_HEADS * V_HEAD), BF16),
        compiler_params=_cparams(("parallel", "parallel", "arbitrary")),
        name="mla_attention",
    )(q, k, vt)


def _pad_last(w, n):
    return jnp.pad(w, [(0, 0)] * (w.ndim - 1) + [(0, n - w.shape[-1])])


def _pad_rows(w, n):
    return jnp.pad(w, ((0, n - w.shape[0]), (0, 0)))


def _rot_cols(w):
    half = w.shape[-1] // 2
    return jnp.concatenate([-w[..., half:], w[..., :half]], axis=-1)


def _even_weights(w_in, mu):
    c = 3 * CONV_DIM + 3 * RWKV_DIM
    split = lambda t, base: (t[..., base:base + W_LORA],
                             t[..., base + W_LORA:base + W_LORA + A_LORA],
                             t[..., base + W_LORA + A_LORA:])
    xw, xa, xg = split(w_in, c)
    w = jnp.concatenate([w_in[..., :c], _pad_last(xw, LANES), _pad_last(xa, LANES),
                         _pad_last(xg, 2 * LANES)], axis=-1).astype(BF16)
    r = 3 * RWKV_DIM
    mw, ma, mg = split(mu, r)
    mu_l = jnp.concatenate([_pad_last(mw, LANES), _pad_last(ma, LANES), _pad_last(mg, 2 * LANES)],
                           axis=-1)
    return w, mu[:, None, :r], mu_l[:, None, :]


def _odd_weights(w_in, w_uq, w_ukv):
    n = w_in.shape[0]
    o3 = POOL_DIM + Q_LORA + KV_LORA
    kpe = w_in[..., o3:]
    w = jnp.concatenate([w_in[..., :o3], _pad_last(kpe, LANES), _pad_last(_rot_cols(kpe), LANES)],
                        axis=-1).astype(BF16)
    uq = w_uq.reshape(n, Q_LORA, MLA_HEADS, QK_NOPE + QK_ROPE)
    wq = _pad_last(uq, QK_PAD).reshape(n, Q_LORA, MLA_HEADS * QK_PAD).astype(BF16)
    wqr = _pad_last(_rot_cols(uq[..., QK_NOPE:]), LANES)
    wqr = wqr.reshape(n, Q_LORA, MLA_HEADS * LANES).astype(BF16)
    ukv = w_ukv.reshape(n, KV_LORA, MLA_HEADS, QK_NOPE + V_HEAD)
    wk = ukv[..., :QK_NOPE].reshape(n, KV_LORA, MLA_HEADS * QK_NOPE).astype(BF16)
    wv = ukv[..., QK_NOPE:].reshape(n, KV_LORA, MLA_HEADS * V_HEAD)
    return w, wq, wqr, wk, jnp.swapaxes(wv, 1, 2).astype(BF16)


def _tile(n, pref):
    return pref if n % pref == 0 else n


def _tiles(b, s):
    m = b * s
    return dict(
        rows_wide=_tile(m, 1024),
        rows=_tile(m, 512),
        ffn_cols=512,
        even_in_cols=EV_COLS // 4,
        odd_in_cols=OD_COLS // 2,
        seq_mixers=_tile(s, 1024),
        seq_prep=_tile(s, 256),
        seq_wkv=_tile(s, 256),
        wkv_pairs=RWKV_DIM // LANES,
        wkv_batch=2 if b % 2 == 0 else 1,
        attn=_tile(s, 512),
        attn_heads=4,
    )


def kernel(x, positions, ev_norm, ev_w_in, ev_conv_w, ev_mu, ev_w0, ev_w2, ev_a0, ev_a2, ev_g2, ev_k_k, ev_k_a, ev_r_k, ev_ln_w, ev_ln_b, ev_w_out, od_norm, od_w_in, od_pool_w, od_pool_scale, od_q_norm, od_w_uq, od_kv_norm, od_w_ukv, od_w_out, ffn_norm, ffn_w_gate, ffn_w_up, ffn_w_down, final_norm):
    b, s, d = x.shape
    m = b * s
    depth = ffn_norm.shape[0]
    t = _tiles(b, s)

    inv = 1.0 / (ROPE_THETA ** (jnp.arange(0, QK_ROPE, 2, dtype=F32) / QK_ROPE))
    inv = jnp.concatenate([inv, inv, jnp.zeros((LANES - QK_ROPE,), F32)])
    ang = (positions.astype(F32)[..., None] * inv).reshape(m, LANES)

    ev_w, ev_mu_main, ev_mu_lora = _even_weights(ev_w_in, ev_mu)
    od_w, wq, wqr, wk, wv = _odd_weights(od_w_in, od_w_uq, od_w_ukv)
    ev_wo = ev_w_out.astype(BF16)
    od_wo = od_w_out.astype(BF16)
    wg, wu, wd = ffn_w_gate.astype(BF16), ffn_w_up.astype(BF16), ffn_w_down
    row = lambda t: t[None, :]

    h = x.reshape(m, d)
    for layer in range(depth):
        j = layer // 2
        if layer % 2 == 0:
            p = _norm_matmul(h, row(ev_norm[j]), ev_w, j, tm=t["rows_wide"], tn=t["even_in_cols"],
                             out_dtype=BF16)
            p = p.reshape(b, s, EV_COLS)
            ya = _conv_mixer(p, ev_conv_w[j].T, ts=t["seq_mixers"], tc=CONV_DIM)
            r, k, v, kk, a, lw, g, bonus = _rwkv_prep(
                p, ev_mu_main[j], ev_mu_lora[j], row(ev_w0[j]),
                _pad_rows(ev_w2[j], LANES).astype(BF16), row(ev_a0[j]),
                _pad_rows(ev_a2[j], LANES).astype(BF16), _pad_rows(ev_g2[j], 2 * LANES).astype(BF16),
                row(ev_k_k[j]), row(ev_k_a[j]), row(ev_r_k[j].reshape(-1)), ts=t["seq_prep"])
            y = _wkv(r, k, v, kk, a, lw, ts=t["seq_wkv"], npair=t["wkv_pairs"], nb=t["wkv_batch"])
            h = _even_out(ya.reshape(m, -1), y.reshape(m, -1), bonus.reshape(m, -1),
                          g.reshape(m, -1), row(ev_ln_w[j]), row(ev_ln_b[j]), ev_wo, j, h,
                          tm=t["rows"])
        else:
            p = _norm_matmul(h, row(od_norm[j]), od_w, j, tm=t["rows_wide"], tn=t["odd_in_cols"],
                             out_dtype=BF16)
            yc = _pool_mixer(p.reshape(b, s, OD_COLS), od_pool_w[j].astype(BF16),
                             row(od_pool_scale[j]), ts=t["seq_mixers"])
            q, kq, vv = _mla_prep(p, ang, row(od_q_norm[j]), row(od_kv_norm[j]),
                                  wq, wqr, wk, wv, j, tm=t["rows"])
            hq = MLA_HEADS * QK_PAD
            yd = _attention(q.reshape(b, s, hq), kq.reshape(b, s, hq), vv, tq=t["attn"],
                            tk=t["attn"], nh=t["attn_heads"])
            h = _proj_residual(yc.reshape(m, -1), yd.reshape(m, -1), od_wo, j, h, tm=t["rows"], tn=d)
        h = _ffn(h, row(ffn_norm[layer]), row(final_norm), wg, wu, wd, layer, tm=t["rows_wide"],
                 tf=t["ffn_cols"], norm_out=layer == depth - 1)
    return h.reshape(b, s, d)
```

```python
import functools

import jax
import jax.numpy as jnp
from jax import lax
from jax.experimental import pallas as pl
from jax.experimental.pallas import tpu as pltpu

F32 = jnp.float32
BF16 = jnp.bfloat16

NORM_EPS = 1e-6
CONV_DIM = 1024
RWKV_DIM = 1024
RWKV_HEAD = 64
W_LORA = 64
A_LORA = 64
GN_EPS = 64e-5
POOL_WINDOWS = (2, 4, 8, 16)
POOL_GROUP = 128
POOL_DIM = 512
MLA_HEADS = 12
Q_LORA = 512
KV_LORA = 512
QK_NOPE = 128
QK_ROPE = 64
V_HEAD = 128
ROPE_THETA = 10000.0
LOG2_E = 1.4426950408889634
EXP_NEG_HALF = 0.6065306597126334

LANES = 128
HIST = 16
VMEM_LIMIT = 60 * 1024 * 1024
QK_PAD = 256
ATTN_SUM_ROWS = 16
VT_ROWS = V_HEAD + ATTN_SUM_ROWS
WKV_CHUNK = 64
EV_COLS = 6912
EV_LORA = 512
OD_COLS = 1792


def _cparams(sem):
    return pltpu.CompilerParams(dimension_semantics=sem, vmem_limit_bytes=VMEM_LIMIT)


def _rms(x, g):
    ms = jnp.mean(x * x, axis=-1, keepdims=True)
    return x * lax.rsqrt(ms + NORM_EPS) * g


def _sigmoid(x):
    return 1.0 / (1.0 + jnp.exp(-x))


def _dot(a, b):
    return jnp.dot(a, b, preferred_element_type=F32)


def _dot_nt(a, b):
    return lax.dot_general(a, b, (((1,), (1,)), ((), ())), preferred_element_type=F32)


def _dot_tn(a, b):
    return lax.dot_general(a, b, (((0,), (0,)), ((), ())), preferred_element_type=F32)


def _norm_mm_kernel(x_ref, g_ref, w_ref, o_ref, xn_ref):
    @pl.when(pl.program_id(1) == 0)
    def _():
        xn_ref[...] = _rms(x_ref[...], g_ref[...]).astype(BF16)

    o_ref[...] = _dot(xn_ref[...], w_ref[...]).astype(o_ref.dtype)


def _norm_matmul(x, g, w, layer, *, tm, tn, out_dtype):
    m, k = x.shape
    n = w.shape[2]
    return pl.pallas_call(
        _norm_mm_kernel,
        grid=(m // tm, n // tn),
        in_specs=[
            pl.BlockSpec((tm, k), lambda i, j: (i, 0)),
            pl.BlockSpec((1, k), lambda i, j: (0, 0)),
            pl.BlockSpec((None, k, tn), lambda i, j: (layer, 0, j)),
        ],
        out_specs=pl.BlockSpec((tm, tn), lambda i, j: (i, j)),
        out_shape=jax.ShapeDtypeStruct((m, n), out_dtype),
        scratch_shapes=[pltpu.VMEM((tm, k), BF16)],
        compiler_params=_cparams(("parallel", "arbitrary")),
        name="norm_matmul",
    )(x, g, w)


def _proj_res_kernel(a_ref, b_ref, w_ref, h_ref, o_ref):
    ka = a_ref.shape[1]
    o_ref[...] = h_ref[...] + _dot(a_ref[...], w_ref[0:ka, :]) + _dot(b_ref[...], w_ref[ka:, :])


def _proj_residual(a, b, w, layer, h, *, tm, tn):
    m, n = h.shape
    ka, kb = a.shape[1], b.shape[1]
    return pl.pallas_call(
        _proj_res_kernel,
        grid=(m // tm, n // tn),
        in_specs=[
            pl.BlockSpec((tm, ka), lambda i, j: (i, 0)),
            pl.BlockSpec((tm, kb), lambda i, j: (i, 0)),
            pl.BlockSpec((None, ka + kb, tn), lambda i, j: (layer, 0, j)),
            pl.BlockSpec((tm, tn), lambda i, j: (i, j)),
        ],
        out_specs=pl.BlockSpec((tm, tn), lambda i, j: (i, j)),
        out_shape=jax.ShapeDtypeStruct((m, n), F32),
        compiler_params=_cparams(("parallel", "arbitrary")),
        name="proj_residual",
    )(a, b, w, h)


def _ffn_kernel(x_ref, g_ref, gout_ref, wg_ref, wu_ref, wd_ref, o_ref, xn_ref, *, norm_out):
    @pl.when(pl.program_id(1) == 0)
    def _():
        x = x_ref[...]
        xn_ref[...] = _rms(x, g_ref[...]).astype(BF16)
        o_ref[...] = x

    xn = xn_ref[...]
    a = _dot(xn, wg_ref[...])
    b = _dot(xn, wu_ref[...])
    act = (a * _sigmoid(a) * b).astype(BF16)
    o_ref[...] += _dot(act, wd_ref[...].astype(BF16))

    if norm_out:
        @pl.when(pl.program_id(1) == pl.num_programs(1) - 1)
        def _():
            o_ref[...] = _rms(o_ref[...], gout_ref[...])


def _ffn(x, g, g_out, wg, wu, wd, layer, *, tm, tf, norm_out):
    m, d = x.shape
    f = wg.shape[2]
    return pl.pallas_call(
        functools.partial(_ffn_kernel, norm_out=norm_out),
        grid=(m // tm, f // tf),
        in_specs=[
            pl.BlockSpec((tm, d), lambda i, j: (i, 0)),
            pl.BlockSpec((1, d), lambda i, j: (0, 0)),
            pl.BlockSpec((1, d), lambda i, j: (0, 0)),
            pl.BlockSpec((None, d, tf), lambda i, j: (layer, 0, j)),
            pl.BlockSpec((None, d, tf), lambda i, j: (layer, 0, j)),
            pl.BlockSpec((None, tf, d), lambda i, j: (layer, j, 0)),
        ],
        out_specs=pl.BlockSpec((tm, d), lambda i, j: (i, 0)),
        out_shape=jax.ShapeDtypeStruct((m, d), F32),
        scratch_shapes=[pltpu.VMEM((tm, d), BF16)],
        compiler_params=_cparams(("parallel", "arbitrary")),
        name="ffn",
    )(x, g, g_out, wg, wu, wd)


def _fill_history(buf_ref, prev, cur, hist, is_first):
    buf_ref[pl.ds(0, hist), :] = jnp.where(is_first, 0.0, prev)
    buf_ref[pl.ds(hist, cur.shape[0]), :] = cur


def _conv_kernel(bg_ref, cg_ref, hh_ref, cgp_ref, hhp_ref, w_ref, o_ref, buf_ref):
    ts = o_ref.shape[1]
    u = cg_ref[0].astype(F32) * hh_ref[0].astype(F32)
    up = cgp_ref[0].astype(F32) * hhp_ref[0].astype(F32)
    _fill_history(buf_ref, up, u, HIST, pl.program_id(1) == 0)
    u1 = buf_ref[pl.ds(HIST - 1, ts), :]
    u2 = buf_ref[pl.ds(HIST - 2, ts), :]
    w = w_ref[...]
    y = w[2:3] * u + w[1:2] * u1 + w[0:1] * u2
    o_ref[0] = (bg_ref[0].astype(F32) * y).astype(o_ref.dtype)


def _conv_mixer(p, conv_w_t, *, ts, tc):
    b, s, _ = p.shape
    nc = CONV_DIM // tc
    rb = ts // HIST

    def cur(off):
        return pl.BlockSpec((1, ts, tc), lambda bi, i, c: (bi, i, off * nc + c))

    def prev(off):
        return pl.BlockSpec((1, HIST, tc),
                            lambda bi, i, c: (bi, jnp.maximum(i * rb - 1, 0), off * nc + c))

    return pl.pallas_call(
        _conv_kernel,
        grid=(b, s // ts, nc),
        in_specs=[cur(0), cur(1), cur(2), prev(1), prev(2),
                  pl.BlockSpec((3, tc), lambda bi, i, c: (0, c))],
        out_specs=pl.BlockSpec((1, ts, tc), lambda bi, i, c: (bi, i, c)),
        out_shape=jax.ShapeDtypeStruct((b, s, CONV_DIM), BF16),
        scratch_shapes=[pltpu.VMEM((ts + HIST, tc), F32)],
        compiler_params=_cparams(("parallel", "parallel", "parallel")),
        name="conv_mixer",
    )(p, p, p, p, p, conv_w_t)


def _head_sum(x, split=True):
    r = lax.broadcasted_iota(jnp.int32, (LANES, LANES), 0) // RWKV_HEAD
    c = lax.broadcasted_iota(jnp.int32, (LANES, LANES), 1) // RWKV_HEAD
    ones = jnp.where(r == c, 1.0, 0.0).astype(BF16)
    hi = x.astype(BF16)
    lo = (x - hi.astype(F32)).astype(BF16) if split else None
    parts = []
    for g in range(x.shape[1] // LANES):
        sl = slice(g * LANES, (g + 1) * LANES)
        part = _dot(hi[:, sl], ones)
        parts.append(part + _dot(lo[:, sl], ones) if split else part)
    return jnp.concatenate(parts, axis=1)


def _token_shift(xb, prev_rows, is_first):
    ts = xb.shape[0]
    r = lax.broadcasted_iota(jnp.int32, (ts, ts), 0)
    c = lax.broadcasted_iota(jnp.int32, (ts, ts), 1)
    xs = _dot(jnp.where(r == c + 1, 1.0, 0.0).astype(BF16), xb)
    hist = prev_rows.shape[0]
    last = jnp.where(is_first, 0.0, prev_rows[hist - 1:hist, :].astype(F32))
    top = jnp.where(lax.broadcasted_iota(jnp.int32, (HIST, 1), 0) == 0, last, xs[0:HIST])
    return jnp.concatenate([top, xs[HIST:]], axis=0)


def _rwkv_prep_kernel(x_ref, xl_ref, xp_ref, xlp_ref, mu_ref, mul_ref, w0_ref, w2_ref, a0_ref,
                      a2_ref, g2_ref, kk_ref, ka_ref, rk_ref,
                      r_out, k_out, v_out, kkn_out, a_out, lw_out, g_out, bonus_out):
    first = pl.program_id(1) == 0
    d = RWKV_DIM

    x = x_ref[0].astype(F32)
    xm = x + (_token_shift(x_ref[0], xp_ref[0], first) - x) * mu_ref[...]
    xl = xl_ref[0].astype(F32)
    xlm = xl + (_token_shift(xl_ref[0], xlp_ref[0], first) - xl) * mul_ref[...]

    r = xm[:, 0:d]
    k = xm[:, d:2 * d]
    v = xm[:, 2 * d:3 * d]
    xw = xlm[:, 0:LANES]
    xa = xlm[:, LANES:2 * LANES]
    xg = xlm[:, 2 * LANES:4 * LANES]

    y = w0_ref[...] + _dot(jnp.tanh(xw).astype(BF16), w2_ref[...])
    lw = -EXP_NEG_HALF * _sigmoid(y)
    a = _sigmoid(a0_ref[...] + _dot(xa.astype(BF16), a2_ref[...]))
    g = _dot(_sigmoid(xg).astype(BF16), g2_ref[...])

    kk = k * kk_ref[...]
    kk = kk * lax.rsqrt(jnp.maximum(_head_sum(kk * kk, split=False), 1e-24))
    k = k * (1.0 + (a - 1.0) * ka_ref[...])
    bonus = _head_sum(r * k * rk_ref[...], split=False) * v

    r_out[0] = r.astype(r_out.dtype)
    k_out[0] = k.astype(k_out.dtype)
    v_out[0] = v.astype(v_out.dtype)
    kkn_out[0] = kk.astype(kkn_out.dtype)
    a_out[0] = a.astype(a_out.dtype)
    lw_out[0] = lw
    g_out[0] = g.astype(g_out.dtype)
    bonus_out[0] = bonus.astype(bonus_out.dtype)


def _rwkv_prep(p, mu, mul, w0, w2, a0, a2, g2, k_k, k_a, r_k, *, ts):
    b, s, _ = p.shape
    d = RWKV_DIM
    rb = ts // HIST
    row = lambda shape: pl.BlockSpec(shape, lambda bi, i: (0, 0))
    out = pl.BlockSpec((1, ts, d), lambda bi, i: (bi, i, 0))
    return pl.pallas_call(
        _rwkv_prep_kernel,
        grid=(b, s // ts),
        in_specs=[
            pl.BlockSpec((1, ts, 3 * d), lambda bi, i: (bi, i, 1)),
            pl.BlockSpec((1, ts, EV_LORA), lambda bi, i: (bi, i, 6 * d // EV_LORA)),
            pl.BlockSpec((1, HIST, 3 * d), lambda bi, i: (bi, jnp.maximum(i * rb - 1, 0), 1)),
            pl.BlockSpec((1, HIST, EV_LORA),
                         lambda bi, i: (bi, jnp.maximum(i * rb - 1, 0), 6 * d // EV_LORA)),
            row((1, 3 * d)), row((1, EV_LORA)), row((1, d)), row((LANES, d)), row((1, d)),
            row((LANES, d)), row((2 * LANES, d)), row((1, d)), row((1, d)), row((1, d)),
        ],
        out_specs=[out] * 8,
        out_shape=[jax.ShapeDtypeStruct((b, s, d), BF16)] * 5
        + [jax.ShapeDtypeStruct((b, s, d), F32)] + [jax.ShapeDtypeStruct((b, s, d), BF16)] * 2,
        compiler_params=_cparams(("parallel", "parallel")),
        name="rwkv_prep",
    )(p, p, p, p, mu, mul, w0, w2, a0, a2, g2, k_k, k_a, r_k)


WKV_LEVELS = 6
_MASK_STRICT = WKV_LEVELS
_MASK_INCL = WKV_LEVELS + 1
_MASK_EYE = WKV_LEVELS + 2


def _wkv_kernel(r_ref, k_ref, v_ref, kk_ref, a_ref, lw_ref, y_ref, t_ref, mask_ref):
    n = WKV_CHUNK
    nb, ts = y_ref.shape[0], y_ref.shape[1]
    npair = y_ref.shape[2] // LANES

    @pl.when(pl.program_id(2) == 0)
    def _():
        t_ref[...] = jnp.zeros_like(t_ref)
        row = lax.broadcasted_iota(jnp.int32, (2 * n, 2 * n), 0)
        col = lax.broadcasted_iota(jnp.int32, (2 * n, 2 * n), 1)
        same = (row // n) == (col // n)
        x = row ^ col
        level = jnp.zeros_like(x)
        for bit in range(1, WKV_LEVELS):
            level = level + jnp.where(x >= (1 << bit), 1, 0)
        level = jnp.where(same & (row > col), level, -1)
        for lev in range(WKV_LEVELS):
            mask_ref[lev] = jnp.where(level == lev, 1.0, 0.0)
        mask_ref[_MASK_STRICT] = jnp.where(level >= 0, 1.0, 0.0)
        mask_ref[_MASK_INCL] = jnp.where(same & (row >= col), 1.0, 0.0)
        mask_ref[_MASK_EYE] = jnp.where(row == col, 1.0, 0.0)

    head0 = lax.broadcasted_iota(jnp.int32, (n, LANES), 1) < RWKV_HEAD
    tri = jnp.where(lax.broadcasted_iota(jnp.int32, (n, n), 0)
                    >= lax.broadcasted_iota(jnp.int32, (n, n), 1), 1.0, 0.0).astype(BF16)

    def stack(t):
        z = jnp.zeros_like(t)
        return jnp.concatenate([jnp.where(head0, t, z), jnp.where(head0, z, t)], axis=0).astype(BF16)

    def operands(sl, bi, q, cw):
        lanes = slice(q * LANES, (q + 1) * LANES)
        r = r_ref[bi, sl, lanes].astype(F32)
        k = k_ref[bi, sl, lanes].astype(F32)
        kk = kk_ref[bi, sl, lanes].astype(F32)
        lw = lw_ref[bi, sl, lanes]
        cw_end = cw[n - 1:n, :]
        p_inv = jnp.exp(-cw)
        p_to_end = jnp.exp(cw_end - cw)
        kka = kk * a_ref[bi, sl, lanes].astype(F32)
        xar = jnp.concatenate([stack(-kk * jnp.exp(cw - lw)), stack(r * jnp.exp(cw))], axis=0)
        ybk = jnp.concatenate([stack(kka * p_inv), stack(k * p_inv)], axis=0)
        return dict(xar=xar, ybk=ybk, bt=stack(kka * p_to_end), kt=stack(k * p_to_end),
                    vs=stack(v_ref[bi, sl, lanes].astype(F32)), decay=jnp.exp(cw_end))

    def cumulative(sl, bi):
        lw = lw_ref[bi, sl, :]
        hi = lw.astype(BF16)
        r1 = lw - hi.astype(F32)
        mid = r1.astype(BF16)
        lo = (r1 - mid.astype(F32)).astype(BF16)
        return _dot(tri, hi) + _dot(tri, mid) + _dot(tri, lo)

    def chunk(c, carry):
        sl = pl.ds(pl.multiple_of(c * n, n), n)
        cw = [cumulative(sl, bi) for bi in range(nb)]
        strict = mask_ref[_MASK_STRICT] > 0.5
        incl = mask_ref[_MASK_INCL] > 0.5

        def scores(group):
            ops = [operands(sl, bi, q, cw[bi][:, q * LANES:(q + 1) * LANES]) for bi, q in group]
            sc = [_dot_nt(o["xar"], o["ybk"]) for o in ops]
            for o, s in zip(ops, sc):
                o["a_ab"] = jnp.where(strict, s[0:2 * n, 0:2 * n], 0.0)
                o["a_k"] = jnp.concatenate([jnp.where(strict, s[0:2 * n, 2 * n:4 * n], 0.0),
                                            jnp.where(incl, s[2 * n:4 * n, 2 * n:4 * n], 0.0)],
                                           axis=0).astype(BF16)
                o["a_rb"] = jnp.where(incl, s[2 * n:4 * n, 0:2 * n], 0.0).astype(BF16)
            return ops

        def inverse(ops):
            m = [mask_ref[_MASK_EYE] + o["a_ab"] * mask_ref[0] for o in ops]
            for lev in range(1, WKV_LEVELS):
                mb = [x.astype(BF16) for x in m]
                em = [_dot((o["a_ab"] * mask_ref[lev]).astype(BF16), x).astype(BF16)
                      for o, x in zip(ops, mb)]
                m = [x + _dot(xb, y) for x, xb, y in zip(m, mb, em)]
            return m

        def finish(group, ops, m):
            t = [t_ref[bi * npair + q] for bi, q in group]
            xtv = [_dot(jnp.concatenate([o["xar"], o["a_k"]], axis=1),
                        jnp.concatenate([x.astype(BF16), o["vs"]], axis=0)) for o, x in zip(ops, t)]
            u = [_dot(x.astype(BF16), y[0:2 * n].astype(BF16)).astype(BF16) for x, y in zip(m, xtv)]
            for i, (bi, q) in enumerate(group):
                o = ops[i]
                ys = xtv[i][2 * n:4 * n] + _dot(o["a_rb"], u[i])
                y_ref[bi, sl, q * LANES:(q + 1) * LANES] = ys[0:n] + ys[n:2 * n]
                t_ref[bi * npair + q] = \
                    t[i] * jnp.transpose(jnp.broadcast_to(o["decay"], (LANES, LANES))) \
                    + _dot_tn(jnp.concatenate([o["bt"], o["kt"]], axis=0),
                              jnp.concatenate([u[i], o["vs"]], axis=0))

        group = [(bi, q) for bi in range(nb) for q in range(npair)]
        ops = scores(group)
        finish(group, ops, inverse(ops))
        return carry

    lax.fori_loop(0, ts // n, chunk, 0)


def _wkv(r, k, v, kk, a, lw, *, ts, npair, nb):
    b, s, d = r.shape
    wd = npair * LANES
    spec = pl.BlockSpec((nb, ts, wd), lambda bi, p, i: (bi, i, p))
    return pl.pallas_call(
        _wkv_kernel,
        grid=(b // nb, d // wd, s // ts),
        in_specs=[spec] * 6,
        out_specs=spec,
        out_shape=jax.ShapeDtypeStruct((b, s, d), F32),
        scratch_shapes=[pltpu.VMEM((nb * npair, LANES, LANES), F32),
                        pltpu.VMEM((WKV_LEVELS + 3, LANES, LANES), F32)],
        compiler_params=_cparams(("parallel", "parallel", "arbitrary")),
        name="wkv7",
    )(r, k, v, kk, a, lw)


def _even_out_kernel(ya_ref, y_ref, bonus_ref, g_ref, lnw_ref, lnb_ref, w_ref, h_ref, o_ref):
    ka = ya_ref.shape[1]
    y = y_ref[...]
    inv_n = 1.0 / RWKV_HEAD
    s1 = _head_sum(y)
    s2 = _head_sum(y * y)
    acc = h_ref[...] + _dot(ya_ref[...], w_ref[0:ka, :])
    mean = s1 * inv_n
    var = jnp.maximum(s2 * inv_n - mean * mean, 0.0)
    yn = (y - mean) * lax.rsqrt(var + GN_EPS) * lnw_ref[...] + lnb_ref[...]
    yb = ((yn + bonus_ref[...].astype(F32)) * g_ref[...].astype(F32)).astype(BF16)
    o_ref[...] = acc + _dot(yb, w_ref[ka:, :])


def _even_out(ya, y, bonus, g, ln_w, ln_b, w, layer, h, *, tm):
    m, n = h.shape
    ka, kb = ya.shape[1], y.shape[1]
    blk = lambda width: pl.BlockSpec((tm, width), lambda i: (i, 0))
    row = pl.BlockSpec((1, kb), lambda i: (0, 0))
    return pl.pallas_call(
        _even_out_kernel,
        grid=(m // tm,),
        in_specs=[blk(ka), blk(kb), blk(kb), blk(kb), row, row,
                  pl.BlockSpec((None, ka + kb, n), lambda i: (layer, 0, 0)), blk(n)],
        out_specs=blk(n),
        out_shape=jax.ShapeDtypeStruct((m, n), F32),
        compiler_params=_cparams(("parallel",)),
        name="even_out",
    )(ya, y, bonus, g, ln_w, ln_b, w, h)


def _pool_kernel(u_ref, up_ref, w_ref, sc_ref, o_ref, buf_ref):
    ts = o_ref.shape[1]
    hist = max(POOL_WINDOWS)
    i = pl.program_id(1)
    u = u_ref[0].astype(F32)
    _fill_history(buf_ref, up_ref[0].astype(F32), u, hist, i == 0)
    pos = i * ts + lax.broadcasted_iota(jnp.int32, (ts, 1), 0)
    outs = []
    for gi, win in enumerate(POOL_WINDOWS):
        cols = pl.ds(gi * POOL_GROUP, POOL_GROUP)
        acc = buf_ref[pl.ds(hist, ts), cols]
        for j in range(1, win):
            acc = acc + buf_ref[pl.ds(hist - j, ts), cols]
        count = jnp.minimum(pos + 1, win).astype(F32)
        dg = acc / count - u[:, gi * POOL_GROUP:(gi + 1) * POOL_GROUP]
        outs.append(_dot(dg.astype(BF16), w_ref[gi]))
    y = jnp.concatenate(outs, axis=1) * sc_ref[...]
    o_ref[0] = y.astype(o_ref.dtype)


def _pool_mixer(p, pool_w, pool_scale, *, ts):
    b, s, _ = p.shape
    hist = max(POOL_WINDOWS)
    rb = ts // hist
    return pl.pallas_call(
        _pool_kernel,
        grid=(b, s // ts),
        in_specs=[
            pl.BlockSpec((1, ts, POOL_DIM), lambda bi, i: (bi, i, 0)),
            pl.BlockSpec((1, hist, POOL_DIM), lambda bi, i: (bi, jnp.maximum(i * rb - 1, 0), 0)),
            pl.BlockSpec((len(POOL_WINDOWS), POOL_GROUP, POOL_GROUP), lambda bi, i: (0, 0, 0)),
            pl.BlockSpec((1, POOL_DIM), lambda bi, i: (0, 0)),
        ],
        out_specs=pl.BlockSpec((1, ts, POOL_DIM), lambda bi, i: (bi, i, 0)),
        out_shape=jax.ShapeDtypeStruct((b, s, POOL_DIM), BF16),
        scratch_shapes=[pltpu.VMEM((ts + hist, POOL_DIM), F32)],
        compiler_params=_cparams(("parallel", "parallel")),
        name="pool_mixer",
    )(p, p, pool_w, pool_scale)


def _mla_prep_kernel(ql_ref, kvl_ref, kpe_ref, kper_ref, ang_ref, qn_ref, kvn_ref,
                     wq_ref, wqr_ref, wk_ref, wv_ref, q_out, k_out, v_out):
    cos = jnp.cos(ang_ref[...])
    sin = jnp.sin(ang_ref[...])
    qn = _rms(ql_ref[...].astype(F32), qn_ref[...]).astype(BF16)
    kvn = _rms(kvl_ref[...].astype(F32), kvn_ref[...]).astype(BF16)
    scale = (QK_NOPE + QK_ROPE) ** -0.5 * LOG2_E
    qa = _dot(qn, wq_ref[...])
    qr = _dot(qn, wqr_ref[...])
    ka = _dot(kvn, wk_ref[...])
    kpe = kpe_ref[...].astype(F32) * cos + kper_ref[...].astype(F32) * sin
    for h in range(MLA_HEADS):
        lo = h * QK_PAD
        q_out[:, lo:lo + LANES] = (qa[:, lo:lo + LANES] * scale).astype(BF16)
        q_pe = qa[:, lo + LANES:lo + 2 * LANES] * cos + qr[:, h * LANES:(h + 1) * LANES] * sin
        q_out[:, lo + LANES:lo + 2 * LANES] = (q_pe * scale).astype(BF16)
        k_out[:, lo:lo + LANES] = ka[:, h * QK_NOPE:(h + 1) * QK_NOPE].astype(BF16)
        k_out[:, lo + LANES:lo + 2 * LANES] = kpe.astype(BF16)
    vt = _dot_nt(wv_ref[...], kvn).astype(BF16)
    ones = jnp.ones((ATTN_SUM_ROWS, vt.shape[1]), BF16)
    for h in range(MLA_HEADS):
        v_out[h * VT_ROWS:h * VT_ROWS + V_HEAD, :] = vt[h * V_HEAD:(h + 1) * V_HEAD, :]
        v_out[h * VT_ROWS + V_HEAD:(h + 1) * VT_ROWS, :] = ones


def _mla_prep(p, ang, q_norm, kv_norm, wq, wqr, wk, wv, layer, *, tm):
    m = p.shape[0]
    hq = MLA_HEADS * QK_PAD
    hv = MLA_HEADS * VT_ROWS
    row = lambda a: pl.BlockSpec(a.shape, lambda i: (0, 0))
    full = lambda a: pl.BlockSpec((None,) + a.shape[1:], lambda i: (layer, 0, 0))
    return pl.pallas_call(
        _mla_prep_kernel,
        grid=(m // tm,),
        in_specs=[
            pl.BlockSpec((tm, Q_LORA), lambda i: (i, 1)),
            pl.BlockSpec((tm, KV_LORA), lambda i: (i, 2)),
            pl.BlockSpec((tm, LANES), lambda i: (i, 12)),
            pl.BlockSpec((tm, LANES), lambda i: (i, 13)),
            pl.BlockSpec((tm, LANES), lambda i: (i, 0)),
            row(q_norm), row(kv_norm), full(wq), full(wqr), full(wk), full(wv),
        ],
        out_specs=[pl.BlockSpec((tm, hq), lambda i: (i, 0)),
                   pl.BlockSpec((tm, hq), lambda i: (i, 0)),
                   pl.BlockSpec((hv, tm), lambda i: (0, i))],
        out_shape=[jax.ShapeDtypeStruct((m, hq), BF16),
                   jax.ShapeDtypeStruct((m, hq), BF16),
                   jax.ShapeDtypeStruct((hv, m), BF16)],
        compiler_params=_cparams(("parallel",)),
        name="mla_prep",
    )(p, p, p, p, ang, q_norm, kv_norm, wq, wqr, wk, wv)


def _attn_kernel(q_ref, k_ref, vt_ref, o_ref, *, tq, tk, nh):
    qi = pl.program_id(2)
    heads = range(nh)
    qs = [q_ref[0, :, h * QK_PAD:(h + 1) * QK_PAD] for h in heads]
    def keys(j):
        return pl.ds(pl.multiple_of(j * tk, tk), tk)

    def scores(j):
        return [_dot_nt(k_ref[0, keys(j), h * QK_PAD:(h + 1) * QK_PAD], qs[h]) for h in heads]

    def softmax_pv(j, s, m, acc, masked):
        if masked:
            key = j * tk + lax.broadcasted_iota(jnp.int32, (tk, tq), 0)
            qry = qi * tq + lax.broadcasted_iota(jnp.int32, (tk, tq), 1)
            s = [jnp.where(key <= qry, x, -1e30) for x in s]
        m_new = [jnp.maximum(a, jnp.max(x, axis=0, keepdims=True)) for a, x in zip(m, s)]
        alpha = [jnp.exp2(a - b) for a, b in zip(m, m_new)]
        p = [jnp.exp2(x - a).astype(BF16) for x, a in zip(s, m_new)]
        pv = [_dot(vt_ref[h * VT_ROWS:(h + 1) * VT_ROWS, keys(j)], p[h]) for h in heads]
        return m_new, [a * b + x for a, b, x in zip(alpha, acc, pv)]

    def two_blocks(j, m, acc, last_masked):
        s0, s1 = scores(j), scores(j + 1)
        m, acc = softmax_pv(j, s0, m, acc, False)
        return softmax_pv(j + 1, s1, m, acc, last_masked)

    def finish(acc):
        for h in heads:
            out = acc[h][0:V_HEAD] / acc[h][V_HEAD:V_HEAD + 1]
            o_ref[0, :, h * V_HEAD:(h + 1) * V_HEAD] = jnp.transpose(out).astype(o_ref.dtype)

    init = ([jnp.full((1, tq), -1e30, F32) for _ in heads],
            [jnp.zeros((V_HEAD + ATTN_SUM_ROWS, tq), F32) for _ in heads])
    pairs = qi // 2
    m, acc = lax.fori_loop(0, pairs, lambda i, c: two_blocks(2 * i, *c, False), init)

    @pl.when(qi % 2 == 1)
    def _():
        finish(two_blocks(qi - 1, m, acc, True)[1])

    @pl.when(qi % 2 == 0)
    def _():
        finish(softmax_pv(qi, scores(qi), m, acc, True)[1])


def _attention(q, k, vt, *, tq, tk, nh):
    b, s, _ = q.shape
    assert tq == tk, "one key block per query tile sits on the causal diagonal"
    return pl.pallas_call(
        functools.partial(_attn_kernel, tq=tq, tk=tk, nh=nh),
        grid=(b, MLA_HEADS // nh, s // tq),
        in_specs=[
            pl.BlockSpec((1, tq, nh * QK_PAD), lambda bi, h, i: (bi, i, h)),
            pl.BlockSpec((1, s, nh * QK_PAD), lambda bi, h, i: (bi, 0, h)),
            pl.BlockSpec((nh * VT_ROWS, s), lambda bi, h, i: (h, bi)),
        ],
        out_specs=pl.BlockSpec((1, tq, nh * V_HEAD), lambda bi, h, i: (bi, i, h)),
        out_shape=jax.ShapeDtypeStruct((b, s, MLA_HEADS * V_HEAD), BF16),
        compiler_params=_cparams(("parallel", "parallel", "arbitrary")),
        name="mla_attention",
    )(q, k, vt)


def _pad_last(w, n):
    return jnp.pad(w, [(0, 0)] * (w.ndim - 1) + [(0, n - w.shape[-1])])


def _pad_rows(w, n):
    return jnp.pad(w, ((0, n - w.shape[0]), (0, 0)))


def _rot_cols(w):
    half = w.shape[-1] // 2
    return jnp.concatenate([-w[..., half:], w[..., :half]], axis=-1)


def _even_weights(w_in, mu):
    c = 3 * CONV_DIM + 3 * RWKV_DIM
    split = lambda t, base: (t[..., base:base + W_LORA],
                             t[..., base + W_LORA:base + W_LORA + A_LORA],
                             t[..., base + W_LORA + A_LORA:])
    xw, xa, xg = split(w_in, c)
    w = jnp.concatenate([w_in[..., :c], _pad_last(xw, LANES), _pad_last(xa, LANES),
                         _pad_last(xg, EV_COLS - c - 2 * LANES)], axis=-1).astype(BF16)
    r = 3 * RWKV_DIM
    mw, ma, mg = split(mu, r)
    mu_l = jnp.concatenate([_pad_last(mw, LANES), _pad_last(ma, LANES), _pad_last(mg, 2 * LANES)],
                           axis=-1)
    return w, mu[:, None, :r], mu_l[:, None, :]


def _odd_weights(w_in, w_uq, w_ukv):
    n = w_in.shape[0]
    o3 = POOL_DIM + Q_LORA + KV_LORA
    kpe = w_in[..., o3:]
    w = jnp.concatenate([w_in[..., :o3], _pad_last(kpe, LANES), _pad_last(_rot_cols(kpe), LANES)],
                        axis=-1).astype(BF16)
    uq = w_uq.reshape(n, Q_LORA, MLA_HEADS, QK_NOPE + QK_ROPE)
    wq = _pad_last(uq, QK_PAD).reshape(n, Q_LORA, MLA_HEADS * QK_PAD).astype(BF16)
    wqr = _pad_last(_rot_cols(uq[..., QK_NOPE:]), LANES)
    wqr = wqr.reshape(n, Q_LORA, MLA_HEADS * LANES).astype(BF16)
    ukv = w_ukv.reshape(n, KV_LORA, MLA_HEADS, QK_NOPE + V_HEAD)
    wk = ukv[..., :QK_NOPE].reshape(n, KV_LORA, MLA_HEADS * QK_NOPE).astype(BF16)
    wv = ukv[..., QK_NOPE:].reshape(n, KV_LORA, MLA_HEADS * V_HEAD)
    return w, wq, wqr, wk, jnp.swapaxes(wv, 1, 2).astype(BF16)


def _tile(n, pref):
    return pref if n % pref == 0 else n


def _tiles(b, s):
    m = b * s
    return dict(
        rows_wide=_tile(m, 1024),
        rows=_tile(m, 512),
        ffn_cols=512,
        even_in_cols=EV_COLS // 3,
        odd_in_cols=OD_COLS,
        seq_mixers=_tile(s, 1024),
        seq_prep=_tile(s, 256),
        seq_wkv=_tile(s, 256),
        wkv_pairs=RWKV_DIM // LANES,
        wkv_batch=2 if b % 2 == 0 else 1,
        attn=_tile(s, 512),
        attn_heads=4,
    )


def kernel(x, positions, ev_norm, ev_w_in, ev_conv_w, ev_mu, ev_w0, ev_w2, ev_a0, ev_a2, ev_g2, ev_k_k, ev_k_a, ev_r_k, ev_ln_w, ev_ln_b, ev_w_out, od_norm, od_w_in, od_pool_w, od_pool_scale, od_q_norm, od_w_uq, od_kv_norm, od_w_ukv, od_w_out, ffn_norm, ffn_w_gate, ffn_w_up, ffn_w_down, final_norm):
    b, s, d = x.shape
    m = b * s
    depth = ffn_norm.shape[0]
    t = _tiles(b, s)

    inv = 1.0 / (ROPE_THETA ** (jnp.arange(0, QK_ROPE, 2, dtype=F32) / QK_ROPE))
    inv = jnp.concatenate([inv, inv, jnp.zeros((LANES - QK_ROPE,), F32)])
    ang = (positions.astype(F32)[..., None] * inv).reshape(m, LANES)

    ev_w, ev_mu_main, ev_mu_lora = _even_weights(ev_w_in, ev_mu)
    od_w, wq, wqr, wk, wv = _odd_weights(od_w_in, od_w_uq, od_w_ukv)
    ev_wo = ev_w_out.astype(BF16)
    od_wo = od_w_out.astype(BF16)
    wg, wu, wd = ffn_w_gate.astype(BF16), ffn_w_up.astype(BF16), ffn_w_down
    row = lambda t: t[None, :]

    h = x.reshape(m, d)
    for layer in range(depth):
        j = layer // 2
        if layer % 2 == 0:
            p = _norm_matmul(h, row(ev_norm[j]), ev_w, j, tm=t["rows_wide"], tn=t["even_in_cols"],
                             out_dtype=BF16)
            p = p.reshape(b, s, EV_COLS)
            ya = _conv_mixer(p, ev_conv_w[j].T, ts=t["seq_mixers"], tc=CONV_DIM)
            r, k, v, kk, a, lw, g, bonus = _rwkv_prep(
                p, ev_mu_main[j], ev_mu_lora[j], row(ev_w0[j]),
                _pad_rows(ev_w2[j], LANES).astype(BF16), row(ev_a0[j]),
                _pad_rows(ev_a2[j], LANES).astype(BF16), _pad_rows(ev_g2[j], 2 * LANES).astype(BF16),
                row(ev_k_k[j]), row(ev_k_a[j]), row(ev_r_k[j].reshape(-1)), ts=t["seq_prep"])
            y = _wkv(r, k, v, kk, a, lw, ts=t["seq_wkv"], npair=t["wkv_pairs"], nb=t["wkv_batch"])
            h = _even_out(ya.reshape(m, -1), y.reshape(m, -1), bonus.reshape(m, -1),
                          g.reshape(m, -1), row(ev_ln_w[j]), row(ev_ln_b[j]), ev_wo, j, h,
                          tm=t["rows"])
        else:
            p = _norm_matmul(h, row(od_norm[j]), od_w, j, tm=t["rows_wide"], tn=t["odd_in_cols"],
                             out_dtype=BF16)
            yc = _pool_mixer(p.reshape(b, s, OD_COLS), od_pool_w[j].astype(BF16),
                             row(od_pool_scale[j]), ts=t["seq_mixers"])
            q, kq, vv = _mla_prep(p, ang, row(od_q_norm[j]), row(od_kv_norm[j]),
                                  wq, wqr, wk, wv, j, tm=t["rows"])
            hq = MLA_HEADS * QK_PAD
            yd = _attention(q.reshape(b, s, hq), kq.reshape(b, s, hq), vv, tq=t["attn"],
                            tk=t["attn"], nh=t["attn_heads"])
            h = _proj_residual(yc.reshape(m, -1), yd.reshape(m, -1), od_wo, j, h, tm=t["rows"], tn=d)
        h = _ffn(h, row(ffn_norm[layer]), row(final_norm), wg, wu, wd, layer, tm=t["rows_wide"],
                 tf=t["ffn_cols"], norm_out=layer == depth - 1)
    return h.reshape(b, s, d)
```

```python
import functools

import jax
import jax.numpy as jnp
from jax import lax
from jax.experimental import pallas as pl
from jax.experimental.pallas import tpu as pltpu

F32 = jnp.float32
BF16 = jnp.bfloat16

NORM_EPS = 1e-6
CONV_DIM = 1024
RWKV_DIM = 1024
RWKV_HEAD = 64
W_LORA = 64
A_LORA = 64
GN_EPS = 64e-5
POOL_WINDOWS = (2, 4, 8, 16)
POOL_GROUP = 128
POOL_DIM = 512
MLA_HEADS = 12
Q_LORA = 512
KV_LORA = 512
QK_NOPE = 128
QK_ROPE = 64
V_HEAD = 128
ROPE_THETA = 10000.0
LOG2_E = 1.4426950408889634
EXP_NEG_HALF = 0.6065306597126334

LANES = 128
HIST = 16
VMEM_LIMIT = 60 * 1024 * 1024
QK_PAD = 256
ATTN_SUM_ROWS = 16
VT_ROWS = V_HEAD + ATTN_SUM_ROWS
WKV_CHUNK = 64
EV_COLS = 6912
EV_LORA = 512
OD_COLS = 1792


def _cparams(sem):
    return pltpu.CompilerParams(dimension_semantics=sem, vmem_limit_bytes=VMEM_LIMIT)


def _rms(x, g):
    ms = jnp.mean(x * x, axis=-1, keepdims=True)
    return x * lax.rsqrt(ms + NORM_EPS) * g


def _sigmoid(x):
    return 1.0 / (1.0 + jnp.exp(-x))


def _dot(a, b):
    return jnp.dot(a, b, preferred_element_type=F32)


def _dot_nt(a, b):
    return lax.dot_general(a, b, (((1,), (1,)), ((), ())), preferred_element_type=F32)


def _dot_tn(a, b):
    return lax.dot_general(a, b, (((0,), (0,)), ((), ())), preferred_element_type=F32)


def _norm_mm_kernel(x_ref, g_ref, w_ref, o_ref, xn_ref):
    @pl.when(pl.program_id(1) == 0)
    def _():
        xn_ref[...] = _rms(x_ref[...], g_ref[...]).astype(BF16)

    o_ref[...] = _dot(xn_ref[...], w_ref[...]).astype(o_ref.dtype)


def _norm_matmul(x, g, w, layer, *, tm, tn, out_dtype):
    m, k = x.shape
    n = w.shape[2]
    return pl.pallas_call(
        _norm_mm_kernel,
        grid=(m // tm, n // tn),
        in_specs=[
            pl.BlockSpec((tm, k), lambda i, j: (i, 0)),
            pl.BlockSpec((1, k), lambda i, j: (0, 0)),
            pl.BlockSpec((None, k, tn), lambda i, j: (layer, 0, j)),
        ],
        out_specs=pl.BlockSpec((tm, tn), lambda i, j: (i, j)),
        out_shape=jax.ShapeDtypeStruct((m, n), out_dtype),
        scratch_shapes=[pltpu.VMEM((tm, k), BF16)],
        compiler_params=_cparams(("parallel", "arbitrary")),
        name="norm_matmul",
    )(x, g, w)


def _proj_res_kernel(a_ref, b_ref, w_ref, h_ref, o_ref):
    ka = a_ref.shape[1]
    o_ref[...] = h_ref[...] + _dot(a_ref[...], w_ref[0:ka, :]) + _dot(b_ref[...], w_ref[ka:, :])


def _proj_residual(a, b, w, layer, h, *, tm, tn):
    m, n = h.shape
    ka, kb = a.shape[1], b.shape[1]
    return pl.pallas_call(
        _proj_res_kernel,
        grid=(m // tm, n // tn),
        in_specs=[
            pl.BlockSpec((tm, ka), lambda i, j: (i, 0)),
            pl.BlockSpec((tm, kb), lambda i, j: (i, 0)),
            pl.BlockSpec((None, ka + kb, tn), lambda i, j: (layer, 0, j)),
            pl.BlockSpec((tm, tn), lambda i, j: (i, j)),
        ],
        out_specs=pl.BlockSpec((tm, tn), lambda i, j: (i, j)),
        out_shape=jax.ShapeDtypeStruct((m, n), F32),
        compiler_params=_cparams(("parallel", "arbitrary")),
        name="proj_residual",
    )(a, b, w, h)


def _ffn_kernel(x_ref, g_ref, gout_ref, wg_ref, wu_ref, wd_ref, o_ref, xn_ref, *, norm_out):
    @pl.when(pl.program_id(1) == 0)
    def _():
        x = x_ref[...]
        xn_ref[...] = _rms(x, g_ref[...]).astype(BF16)
        o_ref[...] = x

    xn = xn_ref[...]
    a = _dot(xn, wg_ref[...])
    b = _dot(xn, wu_ref[...])
    act = (a * _sigmoid(a) * b).astype(BF16)
    o_ref[...] += _dot(act, wd_ref[...].astype(BF16))

    if norm_out:
        @pl.when(pl.program_id(1) == pl.num_programs(1) - 1)
        def _():
            o_ref[...] = _rms(o_ref[...], gout_ref[...])


def _ffn(x, g, g_out, wg, wu, wd, layer, *, tm, tf, norm_out):
    m, d = x.shape
    f = wg.shape[2]
    return pl.pallas_call(
        functools.partial(_ffn_kernel, norm_out=norm_out),
        grid=(m // tm, f // tf),
        in_specs=[
            pl.BlockSpec((tm, d), lambda i, j: (i, 0)),
            pl.BlockSpec((1, d), lambda i, j: (0, 0)),
            pl.BlockSpec((1, d), lambda i, j: (0, 0)),
            pl.BlockSpec((None, d, tf), lambda i, j: (layer, 0, j)),
            pl.BlockSpec((None, d, tf), lambda i, j: (layer, 0, j)),
            pl.BlockSpec((None, tf, d), lambda i, j: (layer, j, 0)),
        ],
        out_specs=pl.BlockSpec((tm, d), lambda i, j: (i, 0)),
        out_shape=jax.ShapeDtypeStruct((m, d), F32),
        scratch_shapes=[pltpu.VMEM((tm, d), BF16)],
        compiler_params=_cparams(("parallel", "arbitrary")),
        name="ffn",
    )(x, g, g_out, wg, wu, wd)


def _fill_history(buf_ref, prev, cur, hist, is_first):
    buf_ref[pl.ds(0, hist), :] = jnp.where(is_first, 0.0, prev)
    buf_ref[pl.ds(hist, cur.shape[0]), :] = cur


def _conv_kernel(bg_ref, cg_ref, hh_ref, cgp_ref, hhp_ref, w_ref, o_ref, buf_ref):
    ts = o_ref.shape[1]
    u = cg_ref[0].astype(F32) * hh_ref[0].astype(F32)
    up = cgp_ref[0].astype(F32) * hhp_ref[0].astype(F32)
    _fill_history(buf_ref, up, u, HIST, pl.program_id(1) == 0)
    u1 = buf_ref[pl.ds(HIST - 1, ts), :]
    u2 = buf_ref[pl.ds(HIST - 2, ts), :]
    w = w_ref[...]
    y = w[2:3] * u + w[1:2] * u1 + w[0:1] * u2
    o_ref[0] = (bg_ref[0].astype(F32) * y).astype(o_ref.dtype)


def _conv_mixer(p, conv_w_t, *, ts, tc):
    b, s, _ = p.shape
    nc = CONV_DIM // tc
    rb = ts // HIST

    def cur(off):
        return pl.BlockSpec((1, ts, tc), lambda bi, i, c: (bi, i, off * nc + c))

    def prev(off):
        return pl.BlockSpec((1, HIST, tc),
                            lambda bi, i, c: (bi, jnp.maximum(i * rb - 1, 0), off * nc + c))

    return pl.pallas_call(
        _conv_kernel,
        grid=(b, s // ts, nc),
        in_specs=[cur(0), cur(1), cur(2), prev(1), prev(2),
                  pl.BlockSpec((3, tc), lambda bi, i, c: (0, c))],
        out_specs=pl.BlockSpec((1, ts, tc), lambda bi, i, c: (bi, i, c)),
        out_shape=jax.ShapeDtypeStruct((b, s, CONV_DIM), BF16),
        scratch_shapes=[pltpu.VMEM((ts + HIST, tc), F32)],
        compiler_params=_cparams(("parallel", "parallel", "parallel")),
        name="conv_mixer",
    )(p, p, p, p, p, conv_w_t)


def _head_sum(x, split=True):
    r = lax.broadcasted_iota(jnp.int32, (LANES, LANES), 0) // RWKV_HEAD
    c = lax.broadcasted_iota(jnp.int32, (LANES, LANES), 1) // RWKV_HEAD
    ones = jnp.where(r == c, 1.0, 0.0).astype(BF16)
    hi = x.astype(BF16)
    lo = (x - hi.astype(F32)).astype(BF16) if split else None
    parts = []
    for g in range(x.shape[1] // LANES):
        sl = slice(g * LANES, (g + 1) * LANES)
        part = _dot(hi[:, sl], ones)
        parts.append(part + _dot(lo[:, sl], ones) if split else part)
    return jnp.concatenate(parts, axis=1)


def _token_shift(xb, prev_rows, is_first):
    ts = xb.shape[0]
    r = lax.broadcasted_iota(jnp.int32, (ts, ts), 0)
    c = lax.broadcasted_iota(jnp.int32, (ts, ts), 1)
    xs = _dot(jnp.where(r == c + 1, 1.0, 0.0).astype(BF16), xb)
    hist = prev_rows.shape[0]
    last = jnp.where(is_first, 0.0, prev_rows[hist - 1:hist, :].astype(F32))
    top = jnp.where(lax.broadcasted_iota(jnp.int32, (HIST, 1), 0) == 0, last, xs[0:HIST])
    return jnp.concatenate([top, xs[HIST:]], axis=0)


def _rwkv_prep_kernel(x_ref, xl_ref, xp_ref, xlp_ref, mu_ref, mul_ref, w0_ref, w2_ref, a0_ref,
                      a2_ref, g2_ref, kk_ref, ka_ref, rk_ref,
                      r_out, k_out, v_out, kkn_out, a_out, lw_out, g_out, bonus_out):
    first = pl.program_id(1) == 0
    d = RWKV_DIM

    x = x_ref[0].astype(F32)
    xm = x + (_token_shift(x_ref[0], xp_ref[0], first) - x) * mu_ref[...]
    xl = xl_ref[0].astype(F32)
    xlm = xl + (_token_shift(xl_ref[0], xlp_ref[0], first) - xl) * mul_ref[...]

    r = xm[:, 0:d]
    k = xm[:, d:2 * d]
    v = xm[:, 2 * d:3 * d]
    xwa = xlm[:, 0:LANES]
    xg = xlm[:, LANES:3 * LANES]

    y = w0_ref[...] + _dot(jnp.tanh(xwa).astype(BF16), w2_ref[...])
    lw = -EXP_NEG_HALF * _sigmoid(y)
    a = _sigmoid(a0_ref[...] + _dot(xwa.astype(BF16), a2_ref[...]))
    g = _dot(_sigmoid(xg).astype(BF16), g2_ref[...])

    kk = k * kk_ref[...]
    kk = kk * lax.rsqrt(jnp.maximum(_head_sum(kk * kk, split=False), 1e-24))
    k = k * (1.0 + (a - 1.0) * ka_ref[...])
    bonus = _head_sum(r * k * rk_ref[...], split=False) * v

    r_out[0] = r.astype(r_out.dtype)
    k_out[0] = k.astype(k_out.dtype)
    v_out[0] = v.astype(v_out.dtype)
    kkn_out[0] = kk.astype(kkn_out.dtype)
    a_out[0] = a.astype(a_out.dtype)
    lw_out[0] = lw
    g_out[0] = g.astype(g_out.dtype)
    bonus_out[0] = bonus.astype(bonus_out.dtype)


def _rwkv_prep(p, mu, mul, w0, w2, a0, a2, g2, k_k, k_a, r_k, *, ts):
    b, s, _ = p.shape
    d = RWKV_DIM
    rb = ts // HIST
    row = lambda shape: pl.BlockSpec(shape, lambda bi, i: (0, 0))
    out = pl.BlockSpec((1, ts, d), lambda bi, i: (bi, i, 0))
    return pl.pallas_call(
        _rwkv_prep_kernel,
        grid=(b, s // ts),
        in_specs=[
            pl.BlockSpec((1, ts, 3 * d), lambda bi, i: (bi, i, 1)),
            pl.BlockSpec((1, ts, EV_LORA), lambda bi, i: (bi, i, 6 * d // EV_LORA)),
            pl.BlockSpec((1, HIST, 3 * d), lambda bi, i: (bi, jnp.maximum(i * rb - 1, 0), 1)),
            pl.BlockSpec((1, HIST, EV_LORA),
                         lambda bi, i: (bi, jnp.maximum(i * rb - 1, 0), 6 * d // EV_LORA)),
            row((1, 3 * d)), row((1, EV_LORA)), row((1, d)), row((LANES, d)), row((1, d)),
            row((LANES, d)), row((2 * LANES, d)), row((1, d)), row((1, d)), row((1, d)),
        ],
        out_specs=[out] * 8,
        out_shape=[jax.ShapeDtypeStruct((b, s, d), BF16)] * 5
        + [jax.ShapeDtypeStruct((b, s, d), F32)] + [jax.ShapeDtypeStruct((b, s, d), BF16)] * 2,
        compiler_params=_cparams(("parallel", "parallel")),
        name="rwkv_prep",
    )(p, p, p, p, mu, mul, w0, w2, a0, a2, g2, k_k, k_a, r_k)


WKV_LEVELS = 6
_MASK_STRICT = WKV_LEVELS
_MASK_INCL = WKV_LEVELS + 1
_MASK_EYE = WKV_LEVELS + 2


def _wkv_kernel(r_ref, k_ref, v_ref, kk_ref, a_ref, lw_ref, y_ref, t_ref, mask_ref):
    n = WKV_CHUNK
    nb, ts = y_ref.shape[0], y_ref.shape[1]
    npair = y_ref.shape[2] // LANES

    @pl.when(pl.program_id(2) == 0)
    def _():
        t_ref[...] = jnp.zeros_like(t_ref)
        row = lax.broadcasted_iota(jnp.int32, (2 * n, 2 * n), 0)
        col = lax.broadcasted_iota(jnp.int32, (2 * n, 2 * n), 1)
        same = (row // n) == (col // n)
        x = row ^ col
        level = jnp.zeros_like(x)
        for bit in range(1, WKV_LEVELS):
            level = level + jnp.where(x >= (1 << bit), 1, 0)
        level = jnp.where(same & (row > col), level, -1)
        for lev in range(WKV_LEVELS):
            mask_ref[lev] = jnp.where(level == lev, 1.0, 0.0)
        mask_ref[_MASK_STRICT] = jnp.where(level >= 0, 1.0, 0.0)
        mask_ref[_MASK_INCL] = jnp.where(same & (row >= col), 1.0, 0.0)
        mask_ref[_MASK_EYE] = jnp.where(row == col, 1.0, 0.0)

    head0 = lax.broadcasted_iota(jnp.int32, (n, LANES), 1) < RWKV_HEAD
    tri = jnp.where(lax.broadcasted_iota(jnp.int32, (n, n), 0)
                    >= lax.broadcasted_iota(jnp.int32, (n, n), 1), 1.0, 0.0).astype(BF16)

    def stack(t):
        z = jnp.zeros_like(t)
        return jnp.concatenate([jnp.where(head0, t, z), jnp.where(head0, z, t)], axis=0).astype(BF16)

    def operands(sl, bi, q, cw):
        lanes = slice(q * LANES, (q + 1) * LANES)
        r = r_ref[bi, sl, lanes].astype(F32)
        k = k_ref[bi, sl, lanes].astype(F32)
        kk = kk_ref[bi, sl, lanes].astype(F32)
        lw = lw_ref[bi, sl, lanes]
        cw_end = cw[n - 1:n, :]
        p_inv = jnp.exp(-cw)
        p_to_end = jnp.exp(cw_end - cw)
        kka = kk * a_ref[bi, sl, lanes].astype(F32)
        xar = jnp.concatenate([stack(-kk * jnp.exp(cw - lw)), stack(r * jnp.exp(cw))], axis=0)
        ybk = jnp.concatenate([stack(kka * p_inv), stack(k * p_inv)], axis=0)
        return dict(xar=xar, ybk=ybk, bt=stack(kka * p_to_end), kt=stack(k * p_to_end),
                    vs=stack(v_ref[bi, sl, lanes].astype(F32)), decay=jnp.exp(cw_end))

    def cumulative(sl, bi):
        lw = lw_ref[bi, sl, :]
        hi = lw.astype(BF16)
        r1 = lw - hi.astype(F32)
        mid = r1.astype(BF16)
        lo = (r1 - mid.astype(F32)).astype(BF16)
        return _dot(tri, hi) + _dot(tri, mid) + _dot(tri, lo)

    def chunk(c, carry):
        sl = pl.ds(pl.multiple_of(c * n, n), n)
        cw = [cumulative(sl, bi) for bi in range(nb)]
        strict = mask_ref[_MASK_STRICT] > 0.5
        incl = mask_ref[_MASK_INCL] > 0.5

        def scores(group):
            ops = [operands(sl, bi, q, cw[bi][:, q * LANES:(q + 1) * LANES]) for bi, q in group]
            sc = [_dot_nt(o["xar"], o["ybk"]) for o in ops]
            for o, s in zip(ops, sc):
                o["a_ab"] = jnp.where(strict, s[0:2 * n, 0:2 * n], 0.0)
                o["a_k"] = jnp.concatenate([jnp.where(strict, s[0:2 * n, 2 * n:4 * n], 0.0),
                                            jnp.where(incl, s[2 * n:4 * n, 2 * n:4 * n], 0.0)],
                                           axis=0).astype(BF16)
                o["a_rb"] = jnp.where(incl, s[2 * n:4 * n, 0:2 * n], 0.0).astype(BF16)
            return ops

        def inverse(ops):
            m = [mask_ref[_MASK_EYE] + o["a_ab"] * mask_ref[0] for o in ops]
            for lev in range(1, WKV_LEVELS):
                mb = [x.astype(BF16) for x in m]
                em = [_dot((o["a_ab"] * mask_ref[lev]).astype(BF16), x).astype(BF16)
                      for o, x in zip(ops, mb)]
                m = [x + _dot(xb, y) for x, xb, y in zip(m, mb, em)]
            return m

        def finish(group, ops, m):
            t = [t_ref[bi * npair + q] for bi, q in group]
            xtv = [_dot(jnp.concatenate([o["xar"], o["a_k"]], axis=1),
                        jnp.concatenate([x.astype(BF16), o["vs"]], axis=0)) for o, x in zip(ops, t)]
            u = [_dot(x.astype(BF16), y[0:2 * n].astype(BF16)).astype(BF16) for x, y in zip(m, xtv)]
            for i, (bi, q) in enumerate(group):
                o = ops[i]
                ys = xtv[i][2 * n:4 * n] + _dot(o["a_rb"], u[i])
                y_ref[bi, sl, q * LANES:(q + 1) * LANES] = ys[0:n] + ys[n:2 * n]
                t_ref[bi * npair + q] = \
                    t[i] * jnp.transpose(jnp.broadcast_to(o["decay"], (LANES, LANES))) \
                    + _dot_tn(jnp.concatenate([o["bt"], o["kt"]], axis=0),
                              jnp.concatenate([u[i], o["vs"]], axis=0))

        group = [(bi, q) for bi in range(nb) for q in range(npair)]
        ops = scores(group)
        finish(group, ops, inverse(ops))
        return carry

    lax.fori_loop(0, ts // n, chunk, 0)


def _wkv(r, k, v, kk, a, lw, *, ts, npair, nb):
    b, s, d = r.shape
    wd = npair * LANES
    spec = pl.BlockSpec((nb, ts, wd), lambda bi, p, i: (bi, i, p))
    return pl.pallas_call(
        _wkv_kernel,
        grid=(b // nb, d // wd, s // ts),
        in_specs=[spec] * 6,
        out_specs=spec,
        out_shape=jax.ShapeDtypeStruct((b, s, d), F32),
        scratch_shapes=[pltpu.VMEM((nb * npair, LANES, LANES), F32),
                        pltpu.VMEM((WKV_LEVELS + 3, LANES, LANES), F32)],
        compiler_params=_cparams(("parallel", "parallel", "arbitrary")),
        name="wkv7",
    )(r, k, v, kk, a, lw)


def _even_out_kernel(ya_ref, y_ref, bonus_ref, g_ref, lnw_ref, lnb_ref, w_ref, h_ref, o_ref):
    ka = ya_ref.shape[1]
    y = y_ref[...]
    inv_n = 1.0 / RWKV_HEAD
    s1 = _head_sum(y)
    s2 = _head_sum(y * y)
    acc = h_ref[...] + _dot(ya_ref[...], w_ref[0:ka, :])
    mean = s1 * inv_n
    var = jnp.maximum(s2 * inv_n - mean * mean, 0.0)
    yn = (y - mean) * lax.rsqrt(var + GN_EPS) * lnw_ref[...] + lnb_ref[...]
    yb = ((yn + bonus_ref[...].astype(F32)) * g_ref[...].astype(F32)).astype(BF16)
    o_ref[...] = acc + _dot(yb, w_ref[ka:, :])


def _even_out(ya, y, bonus, g, ln_w, ln_b, w, layer, h, *, tm):
    m, n = h.shape
    ka, kb = ya.shape[1], y.shape[1]
    blk = lambda width: pl.BlockSpec((tm, width), lambda i: (i, 0))
    row = pl.BlockSpec((1, kb), lambda i: (0, 0))
    return pl.pallas_call(
        _even_out_kernel,
        grid=(m // tm,),
        in_specs=[blk(ka), blk(kb), blk(kb), blk(kb), row, row,
                  pl.BlockSpec((None, ka + kb, n), lambda i: (layer, 0, 0)), blk(n)],
        out_specs=blk(n),
        out_shape=jax.ShapeDtypeStruct((m, n), F32),
        compiler_params=_cparams(("parallel",)),
        name="even_out",
    )(ya, y, bonus, g, ln_w, ln_b, w, h)


def _pool_kernel(u_ref, up_ref, w_ref, sc_ref, o_ref, buf_ref):
    ts = o_ref.shape[1]
    hist = max(POOL_WINDOWS)
    i = pl.program_id(1)
    u = u_ref[0].astype(F32)
    _fill_history(buf_ref, up_ref[0].astype(F32), u, hist, i == 0)
    pos = i * ts + lax.broadcasted_iota(jnp.int32, (ts, 1), 0)
    outs = []
    for gi, win in enumerate(POOL_WINDOWS):
        cols = pl.ds(gi * POOL_GROUP, POOL_GROUP)
        acc = buf_ref[pl.ds(hist, ts), cols]
        for j in range(1, win):
            acc = acc + buf_ref[pl.ds(hist - j, ts), cols]
        count = jnp.minimum(pos + 1, win).astype(F32)
        dg = acc / count - u[:, gi * POOL_GROUP:(gi + 1) * POOL_GROUP]
        outs.append(_dot(dg.astype(BF16), w_ref[gi]))
    y = jnp.concatenate(outs, axis=1) * sc_ref[...]
    o_ref[0] = y.astype(o_ref.dtype)


def _pool_mixer(p, pool_w, pool_scale, *, ts):
    b, s, _ = p.shape
    hist = max(POOL_WINDOWS)
    rb = ts // hist
    return pl.pallas_call(
        _pool_kernel,
        grid=(b, s // ts),
        in_specs=[
            pl.BlockSpec((1, ts, POOL_DIM), lambda bi, i: (bi, i, 0)),
            pl.BlockSpec((1, hist, POOL_DIM), lambda bi, i: (bi, jnp.maximum(i * rb - 1, 0), 0)),
            pl.BlockSpec((len(POOL_WINDOWS), POOL_GROUP, POOL_GROUP), lambda bi, i: (0, 0, 0)),
            pl.BlockSpec((1, POOL_DIM), lambda bi, i: (0, 0)),
        ],
        out_specs=pl.BlockSpec((1, ts, POOL_DIM), lambda bi, i: (bi, i, 0)),
        out_shape=jax.ShapeDtypeStruct((b, s, POOL_DIM), BF16),
        scratch_shapes=[pltpu.VMEM((ts + hist, POOL_DIM), F32)],
        compiler_params=_cparams(("parallel", "parallel")),
        name="pool_mixer",
    )(p, p, pool_w, pool_scale)


def _mla_prep_kernel(ql_ref, kvl_ref, kpe_ref, kper_ref, ang_ref, qn_ref, kvn_ref,
                     wq_ref, wqr_ref, wk_ref, wv_ref, q_out, k_out, v_out):
    cos = jnp.cos(ang_ref[...])
    sin = jnp.sin(ang_ref[...])
    qn = _rms(ql_ref[...].astype(F32), qn_ref[...]).astype(BF16)
    kvn = _rms(kvl_ref[...].astype(F32), kvn_ref[...]).astype(BF16)
    scale = (QK_NOPE + QK_ROPE) ** -0.5 * LOG2_E
    qa = _dot(qn, wq_ref[...])
    qr = _dot(qn, wqr_ref[...])
    ka = _dot(kvn, wk_ref[...])
    kpe = kpe_ref[...].astype(F32) * cos + kper_ref[...].astype(F32) * sin
    for h in range(MLA_HEADS):
        lo = h * QK_PAD
        q_out[:, lo:lo + LANES] = (qa[:, lo:lo + LANES] * scale).astype(BF16)
        q_pe = qa[:, lo + LANES:lo + 2 * LANES] * cos + qr[:, h * LANES:(h + 1) * LANES] * sin
        q_out[:, lo + LANES:lo + 2 * LANES] = (q_pe * scale).astype(BF16)
        k_out[:, lo:lo + LANES] = ka[:, h * QK_NOPE:(h + 1) * QK_NOPE].astype(BF16)
        k_out[:, lo + LANES:lo + 2 * LANES] = kpe.astype(BF16)
    vt = _dot_nt(wv_ref[...], kvn).astype(BF16)
    ones = jnp.ones((ATTN_SUM_ROWS, vt.shape[1]), BF16)
    for h in range(MLA_HEADS):
        v_out[h * VT_ROWS:h * VT_ROWS + V_HEAD, :] = vt[h * V_HEAD:(h + 1) * V_HEAD, :]
        v_out[h * VT_ROWS + V_HEAD:(h + 1) * VT_ROWS, :] = ones


def _mla_prep(p, ang, q_norm, kv_norm, wq, wqr, wk, wv, layer, *, tm):
    m = p.shape[0]
    hq = MLA_HEADS * QK_PAD
    hv = MLA_HEADS * VT_ROWS
    row = lambda a: pl.BlockSpec(a.shape, lambda i: (0, 0))
    full = lambda a: pl.BlockSpec((None,) + a.shape[1:], lambda i: (layer, 0, 0))
    return pl.pallas_call(
        _mla_prep_kernel,
        grid=(m // tm,),
        in_specs=[
            pl.BlockSpec((tm, Q_LORA), lambda i: (i, 1)),
            pl.BlockSpec((tm, KV_LORA), lambda i: (i, 2)),
            pl.BlockSpec((tm, LANES), lambda i: (i, 12)),
            pl.BlockSpec((tm, LANES), lambda i: (i, 13)),
            pl.BlockSpec((tm, LANES), lambda i: (i, 0)),
            row(q_norm), row(kv_norm), full(wq), full(wqr), full(wk), full(wv),
        ],
        out_specs=[pl.BlockSpec((tm, hq), lambda i: (i, 0)),
                   pl.BlockSpec((tm, hq), lambda i: (i, 0)),
                   pl.BlockSpec((hv, tm), lambda i: (0, i))],
        out_shape=[jax.ShapeDtypeStruct((m, hq), BF16),
                   jax.ShapeDtypeStruct((m, hq), BF16),
                   jax.ShapeDtypeStruct((hv, m), BF16)],
        compiler_params=_cparams(("parallel",)),
        name="mla_prep",
    )(p, p, p, p, ang, q_norm, kv_norm, wq, wqr, wk, wv)


def _attn_kernel(q_ref, k_ref, vt_ref, o_ref, *, tq, tk, nh):
    qi = pl.program_id(2)
    heads = range(nh)
    qs = [q_ref[0, :, h * QK_PAD:(h + 1) * QK_PAD] for h in heads]
    def keys(j):
        return pl.ds(pl.multiple_of(j * tk, tk), tk)

    def scores(j):
        return [_dot_nt(k_ref[0, keys(j), h * QK_PAD:(h + 1) * QK_PAD], qs[h]) for h in heads]

    def softmax_pv(j, s, m, acc, masked):
        if masked:
            key = j * tk + lax.broadcasted_iota(jnp.int32, (tk, tq), 0)
            qry = qi * tq + lax.broadcasted_iota(jnp.int32, (tk, tq), 1)
            s = [jnp.where(key <= qry, x, -1e30) for x in s]
        m_new = [jnp.maximum(a, jnp.max(x, axis=0, keepdims=True)) for a, x in zip(m, s)]
        alpha = [jnp.exp2(a - b) for a, b in zip(m, m_new)]
        p = [jnp.exp2(x - a).astype(BF16) for x, a in zip(s, m_new)]
        pv = [_dot(vt_ref[h * VT_ROWS:(h + 1) * VT_ROWS, keys(j)], p[h]) for h in heads]
        return m_new, [a * b + x for a, b, x in zip(alpha, acc, pv)]

    def two_blocks(j, m, acc, last_masked):
        s0, s1 = scores(j), scores(j + 1)
        m, acc = softmax_pv(j, s0, m, acc, False)
        return softmax_pv(j + 1, s1, m, acc, last_masked)

    def finish(acc):
        for h in heads:
            out = acc[h][0:V_HEAD] / acc[h][V_HEAD:V_HEAD + 1]
            o_ref[0, :, h * V_HEAD:(h + 1) * V_HEAD] = jnp.transpose(out).astype(o_ref.dtype)

    init = ([jnp.full((1, tq), -1e30, F32) for _ in heads],
            [jnp.zeros((V_HEAD + ATTN_SUM_ROWS, tq), F32) for _ in heads])
    pairs = qi // 2
    m, acc = lax.fori_loop(0, pairs, lambda i, c: two_blocks(2 * i, *c, False), init)

    @pl.when(qi % 2 == 1)
    def _():
        finish(two_blocks(qi - 1, m, acc, True)[1])

    @pl.when(qi % 2 == 0)
    def _():
        finish(softmax_pv(qi, scores(qi), m, acc, True)[1])


def _attention(q, k, vt, *, tq, tk, nh):
    b, s, _ = q.shape
    assert tq == tk, "one key block per query tile sits on the causal diagonal"
    return pl.pallas_call(
        functools.partial(_attn_kernel, tq=tq, tk=tk, nh=nh),
        grid=(b, MLA_HEADS // nh, s // tq),
        in_specs=[
            pl.BlockSpec((1, tq, nh * QK_PAD), lambda bi, h, i: (bi, i, h)),
            pl.BlockSpec((1, s, nh * QK_PAD), lambda bi, h, i: (bi, 0, h)),
            pl.BlockSpec((nh * VT_ROWS, s), lambda bi, h, i: (h, bi)),
        ],
        out_specs=pl.BlockSpec((1, tq, nh * V_HEAD), lambda bi, h, i: (bi, i, h)),
        out_shape=jax.ShapeDtypeStruct((b, s, MLA_HEADS * V_HEAD), BF16),
        compiler_params=_cparams(("parallel", "parallel", "arbitrary")),
        name="mla_attention",
    )(q, k, vt)


def _pad_last(w, n):
    return jnp.pad(w, [(0, 0)] * (w.ndim - 1) + [(0, n - w.shape[-1])])


def _pad_rows(w, n):
    return jnp.pad(w, ((0, n - w.shape[0]), (0, 0)))


def _rot_cols(w):
    half = w.shape[-1] // 2
    return jnp.concatenate([-w[..., half:], w[..., :half]], axis=-1)


def _even_weights(w_in, mu):
    r = 3 * RWKV_DIM
    w = _pad_last(w_in, EV_COLS).astype(BF16)
    return w, mu[:, None, :r], _pad_last(mu[:, r:], EV_LORA)[:, None, :]


def _odd_weights(w_in, w_uq, w_ukv):
    n = w_in.shape[0]
    o3 = POOL_DIM + Q_LORA + KV_LORA
    kpe = w_in[..., o3:]
    w = jnp.concatenate([w_in[..., :o3], _pad_last(kpe, LANES), _pad_last(_rot_cols(kpe), LANES)],
                        axis=-1).astype(BF16)
    uq = w_uq.reshape(n, Q_LORA, MLA_HEADS, QK_NOPE + QK_ROPE)
    wq = _pad_last(uq, QK_PAD).reshape(n, Q_LORA, MLA_HEADS * QK_PAD).astype(BF16)
    wqr = _pad_last(_rot_cols(uq[..., QK_NOPE:]), LANES)
    wqr = wqr.reshape(n, Q_LORA, MLA_HEADS * LANES).astype(BF16)
    ukv = w_ukv.reshape(n, KV_LORA, MLA_HEADS, QK_NOPE + V_HEAD)
    wk = ukv[..., :QK_NOPE].reshape(n, KV_LORA, MLA_HEADS * QK_NOPE).astype(BF16)
    wv = ukv[..., QK_NOPE:].reshape(n, KV_LORA, MLA_HEADS * V_HEAD)
    return w, wq, wqr, wk, jnp.swapaxes(wv, 1, 2).astype(BF16)


def _tile(n, pref):
    return pref if n % pref == 0 else n


def _tiles(b, s):
    m = b * s
    return dict(
        rows_wide=_tile(m, 1024),
        rows=_tile(m, 512),
        ffn_cols=512,
        even_in_cols=EV_COLS // 3,
        odd_in_cols=OD_COLS,
        seq_mixers=_tile(s, 1024),
        seq_prep=_tile(s, 256),
        seq_wkv=_tile(s, 256),
        wkv_pairs=RWKV_DIM // LANES,
        wkv_batch=2 if b % 2 == 0 else 1,
        attn=_tile(s, 512),
        attn_heads=4,
    )


def kernel(x, positions, ev_norm, ev_w_in, ev_conv_w, ev_mu, ev_w0, ev_w2, ev_a0, ev_a2, ev_g2, ev_k_k, ev_k_a, ev_r_k, ev_ln_w, ev_ln_b, ev_w_out, od_norm, od_w_in, od_pool_w, od_pool_scale, od_q_norm, od_w_uq, od_kv_norm, od_w_ukv, od_w_out, ffn_norm, ffn_w_gate, ffn_w_up, ffn_w_down, final_norm):
    b, s, d = x.shape
    m = b * s
    depth = ffn_norm.shape[0]
    t = _tiles(b, s)

    inv = 1.0 / (ROPE_THETA ** (jnp.arange(0, QK_ROPE, 2, dtype=F32) / QK_ROPE))
    inv = jnp.concatenate([inv, inv, jnp.zeros((LANES - QK_ROPE,), F32)])
    ang = (positions.astype(F32)[..., None] * inv).reshape(m, LANES)

    ev_w, ev_mu_main, ev_mu_lora = _even_weights(ev_w_in, ev_mu)
    od_w, wq, wqr, wk, wv = _odd_weights(od_w_in, od_w_uq, od_w_ukv)
    ev_wo = ev_w_out.astype(BF16)
    od_wo = od_w_out.astype(BF16)
    wg, wu, wd = ffn_w_gate.astype(BF16), ffn_w_up.astype(BF16), ffn_w_down
    row = lambda t: t[None, :]

    h = x.reshape(m, d)
    for layer in range(depth):
        j = layer // 2
        if layer % 2 == 0:
            p = _norm_matmul(h, row(ev_norm[j]), ev_w, j, tm=t["rows_wide"], tn=t["even_in_cols"],
                             out_dtype=BF16)
            p = p.reshape(b, s, EV_COLS)
            ya = _conv_mixer(p, ev_conv_w[j].T, ts=t["seq_mixers"], tc=CONV_DIM)
            r, k, v, kk, a, lw, g, bonus = _rwkv_prep(
                p, ev_mu_main[j], ev_mu_lora[j], row(ev_w0[j]),
                _pad_rows(ev_w2[j], LANES).astype(BF16), row(ev_a0[j]),
                jnp.pad(ev_a2[j], ((W_LORA, LANES - W_LORA - A_LORA), (0, 0))).astype(BF16),
                _pad_rows(ev_g2[j], 2 * LANES).astype(BF16),
                row(ev_k_k[j]), row(ev_k_a[j]), row(ev_r_k[j].reshape(-1)), ts=t["seq_prep"])
            y = _wkv(r, k, v, kk, a, lw, ts=t["seq_wkv"], npair=t["wkv_pairs"], nb=t["wkv_batch"])
            h = _even_out(ya.reshape(m, -1), y.reshape(m, -1), bonus.reshape(m, -1),
                          g.reshape(m, -1), row(ev_ln_w[j]), row(ev_ln_b[j]), ev_wo, j, h,
                          tm=t["rows"])
        else:
            p = _norm_matmul(h, row(od_norm[j]), od_w, j, tm=t["rows_wide"], tn=t["odd_in_cols"],
                             out_dtype=BF16)
            yc = _pool_mixer(p.reshape(b, s, OD_COLS), od_pool_w[j].astype(BF16),
                             row(od_pool_scale[j]), ts=t["seq_mixers"])
            q, kq, vv = _mla_prep(p, ang, row(od_q_norm[j]), row(od_kv_norm[j]),
                                  wq, wqr, wk, wv, j, tm=t["rows"])
            hq = MLA_HEADS * QK_PAD
            yd = _attention(q.reshape(b, s, hq), kq.reshape(b, s, hq), vv, tq=t["attn"],
                            tk=t["attn"], nh=t["attn_heads"])
            h = _proj_residual(yc.reshape(m, -1), yd.reshape(m, -1), od_wo, j, h, tm=t["rows"], tn=d)
        h = _ffn(h, row(ffn_norm[layer]), row(final_norm), wg, wu, wd, layer, tm=t["rows_wide"],
                 tf=t["ffn_cols"], norm_out=layer == depth - 1)
    return h.reshape(b, s, d)
```

```python
import functools

import jax
import jax.numpy as jnp
from jax import lax
from jax.experimental import pallas as pl
from jax.experimental.pallas import tpu as pltpu

F32 = jnp.float32
BF16 = jnp.bfloat16

NORM_EPS = 1e-6
CONV_DIM = 1024
RWKV_DIM = 1024
RWKV_HEAD = 64
W_LORA = 64
A_LORA = 64
GN_EPS = 64e-5
POOL_WINDOWS = (2, 4, 8, 16)
POOL_GROUP = 128
POOL_DIM = 512
MLA_HEADS = 12
Q_LORA = 512
KV_LORA = 512
QK_NOPE = 128
QK_ROPE = 64
V_HEAD = 128
ROPE_THETA = 10000.0
LOG2_E = 1.4426950408889634
EXP_NEG_HALF = 0.6065306597126334

LANES = 128
HIST = 16
VMEM_LIMIT = 60 * 1024 * 1024
QK_PAD = 256
ATTN_SUM_ROWS = 16
VT_ROWS = V_HEAD + ATTN_SUM_ROWS
WKV_CHUNK = 64
EV_COLS = 6912
EV_LORA = 512
OD_COLS = 1792


def _cparams(sem):
    return pltpu.CompilerParams(dimension_semantics=sem, vmem_limit_bytes=VMEM_LIMIT)


def _rms(x, g):
    ms = jnp.mean(x * x, axis=-1, keepdims=True)
    return x * lax.rsqrt(ms + NORM_EPS) * g


def _sigmoid(x):
    return 1.0 / (1.0 + jnp.exp(-x))


def _dot(a, b):
    return jnp.dot(a, b, preferred_element_type=F32)


def _dot_nt(a, b):
    return lax.dot_general(a, b, (((1,), (1,)), ((), ())), preferred_element_type=F32)


def _dot_tn(a, b):
    return lax.dot_general(a, b, (((0,), (0,)), ((), ())), preferred_element_type=F32)


def _norm_mm_kernel(x_ref, g_ref, w_ref, o_ref, xn_ref):
    @pl.when(pl.program_id(1) == 0)
    def _():
        xn_ref[...] = _rms(x_ref[...], g_ref[...]).astype(BF16)

    o_ref[...] = _dot(xn_ref[...], w_ref[...]).astype(o_ref.dtype)


def _norm_matmul(x, g, w, layer, *, tm, tn, out_dtype):
    m, k = x.shape
    n = w.shape[2]
    return pl.pallas_call(
        _norm_mm_kernel,
        grid=(m // tm, n // tn),
        in_specs=[
            pl.BlockSpec((tm, k), lambda i, j: (i, 0)),
            pl.BlockSpec((1, k), lambda i, j: (0, 0)),
            pl.BlockSpec((None, k, tn), lambda i, j: (layer, 0, j)),
        ],
        out_specs=pl.BlockSpec((tm, tn), lambda i, j: (i, j)),
        out_shape=jax.ShapeDtypeStruct((m, n), out_dtype),
        scratch_shapes=[pltpu.VMEM((tm, k), BF16)],
        compiler_params=_cparams(("parallel", "arbitrary")),
        name="norm_matmul",
    )(x, g, w)


def _proj_res_kernel(a_ref, b_ref, w_ref, h_ref, o_ref):
    ka = a_ref.shape[1]
    o_ref[...] = h_ref[...] + _dot(a_ref[...], w_ref[0:ka, :]) + _dot(b_ref[...], w_ref[ka:, :])


def _proj_residual(a, b, w, layer, h, *, tm, tn):
    m, n = h.shape
    ka, kb = a.shape[1], b.shape[1]
    return pl.pallas_call(
        _proj_res_kernel,
        grid=(m // tm, n // tn),
        in_specs=[
            pl.BlockSpec((tm, ka), lambda i, j: (i, 0)),
            pl.BlockSpec((tm, kb), lambda i, j: (i, 0)),
            pl.BlockSpec((None, ka + kb, tn), lambda i, j: (layer, 0, j)),
            pl.BlockSpec((tm, tn), lambda i, j: (i, j)),
        ],
        out_specs=pl.BlockSpec((tm, tn), lambda i, j: (i, j)),
        out_shape=jax.ShapeDtypeStruct((m, n), F32),
        compiler_params=_cparams(("parallel", "arbitrary")),
        name="proj_residual",
    )(a, b, w, h)


def _ffn_kernel(x_ref, g_ref, gout_ref, wg_ref, wu_ref, wd_ref, o_ref, xn_ref, *, norm_out):
    @pl.when(pl.program_id(1) == 0)
    def _():
        x = x_ref[...]
        xn_ref[...] = _rms(x, g_ref[...]).astype(BF16)
        o_ref[...] = x

    xn = xn_ref[...]
    a = _dot(xn, wg_ref[...])
    b = _dot(xn, wu_ref[...])
    act = (a * _sigmoid(a) * b).astype(BF16)
    o_ref[...] += _dot(act, wd_ref[...].astype(BF16))

    if norm_out:
        @pl.when(pl.program_id(1) == pl.num_programs(1) - 1)
        def _():
            o_ref[...] = _rms(o_ref[...], gout_ref[...])


def _ffn(x, g, g_out, wg, wu, wd, layer, *, tm, tf, norm_out):
    m, d = x.shape
    f = wg.shape[2]
    return pl.pallas_call(
        functools.partial(_ffn_kernel, norm_out=norm_out),
        grid=(m // tm, f // tf),
        in_specs=[
            pl.BlockSpec((tm, d), lambda i, j: (i, 0)),
            pl.BlockSpec((1, d), lambda i, j: (0, 0)),
            pl.BlockSpec((1, d), lambda i, j: (0, 0)),
            pl.BlockSpec((None, d, tf), lambda i, j: (layer, 0, j)),
            pl.BlockSpec((None, d, tf), lambda i, j: (layer, 0, j)),
            pl.BlockSpec((None, tf, d), lambda i, j: (layer, j, 0)),
        ],
        out_specs=pl.BlockSpec((tm, d), lambda i, j: (i, 0)),
        out_shape=jax.ShapeDtypeStruct((m, d), F32),
        scratch_shapes=[pltpu.VMEM((tm, d), BF16)],
        compiler_params=_cparams(("parallel", "arbitrary")),
        name="ffn",
    )(x, g, g_out, wg, wu, wd)


def _fill_history(buf_ref, prev, cur, hist, is_first):
    buf_ref[pl.ds(0, hist), :] = jnp.where(is_first, 0.0, prev)
    buf_ref[pl.ds(hist, cur.shape[0]), :] = cur


def _conv_kernel(bg_ref, cg_ref, hh_ref, cgp_ref, hhp_ref, w_ref, o_ref, buf_ref):
    ts = o_ref.shape[1]
    u = cg_ref[0].astype(F32) * hh_ref[0].astype(F32)
    up = cgp_ref[0].astype(F32) * hhp_ref[0].astype(F32)
    _fill_history(buf_ref, up, u, HIST, pl.program_id(1) == 0)
    u1 = buf_ref[pl.ds(HIST - 1, ts), :]
    u2 = buf_ref[pl.ds(HIST - 2, ts), :]
    w = w_ref[...]
    y = w[2:3] * u + w[1:2] * u1 + w[0:1] * u2
    o_ref[0] = (bg_ref[0].astype(F32) * y).astype(o_ref.dtype)


def _conv_mixer(p, conv_w_t, *, ts, tc):
    b, s, _ = p.shape
    nc = CONV_DIM // tc
    rb = ts // HIST

    def cur(off):
        return pl.BlockSpec((1, ts, tc), lambda bi, i, c: (bi, i, off * nc + c))

    def prev(off):
        return pl.BlockSpec((1, HIST, tc),
                            lambda bi, i, c: (bi, jnp.maximum(i * rb - 1, 0), off * nc + c))

    return pl.pallas_call(
        _conv_kernel,
        grid=(b, s // ts, nc),
        in_specs=[cur(0), cur(1), cur(2), prev(1), prev(2),
                  pl.BlockSpec((3, tc), lambda bi, i, c: (0, c))],
        out_specs=pl.BlockSpec((1, ts, tc), lambda bi, i, c: (bi, i, c)),
        out_shape=jax.ShapeDtypeStruct((b, s, CONV_DIM), BF16),
        scratch_shapes=[pltpu.VMEM((ts + HIST, tc), F32)],
        compiler_params=_cparams(("parallel", "parallel", "parallel")),
        name="conv_mixer",
    )(p, p, p, p, p, conv_w_t)


def _head_sum(x, split=True):
    r = lax.broadcasted_iota(jnp.int32, (LANES, LANES), 0) // RWKV_HEAD
    c = lax.broadcasted_iota(jnp.int32, (LANES, LANES), 1) // RWKV_HEAD
    ones = jnp.where(r == c, 1.0, 0.0).astype(BF16)
    hi = x.astype(BF16)
    lo = (x - hi.astype(F32)).astype(BF16) if split else None
    parts = []
    for g in range(x.shape[1] // LANES):
        sl = slice(g * LANES, (g + 1) * LANES)
        part = _dot(hi[:, sl], ones)
        parts.append(part + _dot(lo[:, sl], ones) if split else part)
    return jnp.concatenate(parts, axis=1)


def _token_shift(xb, prev_rows, is_first):
    ts = xb.shape[0]
    r = lax.broadcasted_iota(jnp.int32, (ts, ts), 0)
    c = lax.broadcasted_iota(jnp.int32, (ts, ts), 1)
    xs = _dot(jnp.where(r == c + 1, 1.0, 0.0).astype(BF16), xb)
    hist = prev_rows.shape[0]
    last = jnp.where(is_first, 0.0, prev_rows[hist - 1:hist, :].astype(F32))
    top = jnp.where(lax.broadcasted_iota(jnp.int32, (HIST, 1), 0) == 0, last, xs[0:HIST])
    return jnp.concatenate([top, xs[HIST:]], axis=0)


def _rwkv_prep_kernel(x_ref, xl_ref, xp_ref, xlp_ref, mu_ref, mul_ref, w0_ref, w2_ref, a0_ref,
                      a2_ref, g2_ref, kk_ref, ka_ref, rk_ref,
                      r_out, k_out, v_out, kkn_out, a_out, lw_out, g_out, bonus_out):
    first = pl.program_id(1) == 0
    d = RWKV_DIM

    x = x_ref[0].astype(F32)
    xm = x + (_token_shift(x_ref[0], xp_ref[0], first) - x) * mu_ref[...]
    xl = xl_ref[0].astype(F32)
    xlm = xl + (_token_shift(xl_ref[0], xlp_ref[0], first) - xl) * mul_ref[...]

    r = xm[:, 0:d]
    k = xm[:, d:2 * d]
    v = xm[:, 2 * d:3 * d]
    xwa = xlm[:, 0:LANES]
    xg = xlm[:, LANES:3 * LANES]

    y = w0_ref[...] + _dot(jnp.tanh(xwa).astype(BF16), w2_ref[...])
    lw = -EXP_NEG_HALF * _sigmoid(y)
    a = _sigmoid(a0_ref[...] + _dot(xwa.astype(BF16), a2_ref[...]))
    g = _dot(_sigmoid(xg).astype(BF16), g2_ref[...])

    kk = k * kk_ref[...]
    kk = kk * lax.rsqrt(jnp.maximum(_head_sum(kk * kk, split=False), 1e-24))
    k = k * (1.0 + (a - 1.0) * ka_ref[...])
    bonus = _head_sum(r * k * rk_ref[...], split=False) * v

    r_out[0] = r.astype(r_out.dtype)
    k_out[0] = k.astype(k_out.dtype)
    v_out[0] = v.astype(v_out.dtype)
    kkn_out[0] = kk.astype(kkn_out.dtype)
    a_out[0] = a.astype(a_out.dtype)
    lw_out[0] = lw
    g_out[0] = g.astype(g_out.dtype)
    bonus_out[0] = bonus.astype(bonus_out.dtype)


def _rwkv_prep(p, mu, mul, w0, w2, a0, a2, g2, k_k, k_a, r_k, *, ts):
    b, s, _ = p.shape
    d = RWKV_DIM
    rb = ts // HIST
    row = lambda shape: pl.BlockSpec(shape, lambda bi, i: (0, 0))
    out = pl.BlockSpec((1, ts, d), lambda bi, i: (bi, i, 0))
    return pl.pallas_call(
        _rwkv_prep_kernel,
        grid=(b, s // ts),
        in_specs=[
            pl.BlockSpec((1, ts, 3 * d), lambda bi, i: (bi, i, 1)),
            pl.BlockSpec((1, ts, EV_LORA), lambda bi, i: (bi, i, 6 * d // EV_LORA)),
            pl.BlockSpec((1, HIST, 3 * d), lambda bi, i: (bi, jnp.maximum(i * rb - 1, 0), 1)),
            pl.BlockSpec((1, HIST, EV_LORA),
                         lambda bi, i: (bi, jnp.maximum(i * rb - 1, 0), 6 * d // EV_LORA)),
            row((1, 3 * d)), row((1, EV_LORA)), row((1, d)), row((LANES, d)), row((1, d)),
            row((LANES, d)), row((2 * LANES, d)), row((1, d)), row((1, d)), row((1, d)),
        ],
        out_specs=[out] * 8,
        out_shape=[jax.ShapeDtypeStruct((b, s, d), BF16)] * 5
        + [jax.ShapeDtypeStruct((b, s, d), F32)] + [jax.ShapeDtypeStruct((b, s, d), BF16)] * 2,
        compiler_params=_cparams(("parallel", "parallel")),
        name="rwkv_prep",
    )(p, p, p, p, mu, mul, w0, w2, a0, a2, g2, k_k, k_a, r_k)


WKV_LEVELS = 6
_MASK_STRICT = WKV_LEVELS
_MASK_INCL = WKV_LEVELS + 1
_MASK_EYE = WKV_LEVELS + 2


def _wkv_kernel(r_ref, k_ref, v_ref, kk_ref, a_ref, lw_ref, y_ref, t_ref, mask_ref):
    n = WKV_CHUNK
    nb, ts = y_ref.shape[0], y_ref.shape[1]
    npair = y_ref.shape[2] // LANES

    @pl.when(pl.program_id(2) == 0)
    def _():
        t_ref[...] = jnp.zeros_like(t_ref)
        row = lax.broadcasted_iota(jnp.int32, (2 * n, 2 * n), 0)
        col = lax.broadcasted_iota(jnp.int32, (2 * n, 2 * n), 1)
        same = (row // n) == (col // n)
        x = row ^ col
        level = jnp.zeros_like(x)
        for bit in range(1, WKV_LEVELS):
            level = level + jnp.where(x >= (1 << bit), 1, 0)
        level = jnp.where(same & (row > col), level, -1)
        for lev in range(WKV_LEVELS):
            mask_ref[lev] = jnp.where(level == lev, 1.0, 0.0)
        mask_ref[_MASK_STRICT] = jnp.where(level >= 0, 1.0, 0.0)
        mask_ref[_MASK_INCL] = jnp.where(same & (row >= col), 1.0, 0.0)
        mask_ref[_MASK_EYE] = jnp.where(row == col, 1.0, 0.0)

    head0 = lax.broadcasted_iota(jnp.int32, (n, LANES), 1) < RWKV_HEAD
    tri = jnp.where(lax.broadcasted_iota(jnp.int32, (n, n), 0)
                    >= lax.broadcasted_iota(jnp.int32, (n, n), 1), 1.0, 0.0).astype(BF16)

    def stack(t):
        z = jnp.zeros_like(t)
        return jnp.concatenate([jnp.where(head0, t, z), jnp.where(head0, z, t)], axis=0).astype(BF16)

    def operands(sl, bi, q, cw):
        lanes = slice(q * LANES, (q + 1) * LANES)
        r = r_ref[bi, sl, lanes].astype(F32)
        k = k_ref[bi, sl, lanes].astype(F32)
        kk = kk_ref[bi, sl, lanes].astype(F32)
        lw = lw_ref[bi, sl, lanes]
        cw_end = cw[n - 1:n, :]
        p_inv = jnp.exp(-cw)
        p_to_end = jnp.exp(cw_end - cw)
        kka = kk * a_ref[bi, sl, lanes].astype(F32)
        xar = jnp.concatenate([stack(-kk * jnp.exp(cw - lw)), stack(r * jnp.exp(cw))], axis=0)
        ybk = jnp.concatenate([stack(kka * p_inv), stack(k * p_inv)], axis=0)
        return dict(xar=xar, ybk=ybk, bt=stack(kka * p_to_end), kt=stack(k * p_to_end),
                    vs=stack(v_ref[bi, sl, lanes].astype(F32)), decay=jnp.exp(cw_end))

    def cumulative(sl, bi):
        lw = lw_ref[bi, sl, :]
        hi = lw.astype(BF16)
        r1 = lw - hi.astype(F32)
        mid = r1.astype(BF16)
        lo = (r1 - mid.astype(F32)).astype(BF16)
        return _dot(tri, hi) + _dot(tri, mid) + _dot(tri, lo)

    def chunk(c, carry):
        sl = pl.ds(pl.multiple_of(c * n, n), n)
        cw = [cumulative(sl, bi) for bi in range(nb)]
        strict = mask_ref[_MASK_STRICT] > 0.5
        incl = mask_ref[_MASK_INCL] > 0.5

        def scores(group):
            ops = [operands(sl, bi, q, cw[bi][:, q * LANES:(q + 1) * LANES]) for bi, q in group]
            sc = [_dot_nt(o["xar"], o["ybk"]) for o in ops]
            for o, s in zip(ops, sc):
                o["a_ab"] = jnp.where(strict, s[0:2 * n, 0:2 * n], 0.0)
                o["a_k"] = jnp.concatenate([jnp.where(strict, s[0:2 * n, 2 * n:4 * n], 0.0),
                                            jnp.where(incl, s[2 * n:4 * n, 2 * n:4 * n], 0.0)],
                                           axis=0).astype(BF16)
                o["a_rb"] = jnp.where(incl, s[2 * n:4 * n, 0:2 * n], 0.0).astype(BF16)
            return ops

        def inverse(ops):
            m = [mask_ref[_MASK_EYE] + o["a_ab"] * mask_ref[0] for o in ops]
            for lev in range(1, WKV_LEVELS):
                mb = [x.astype(BF16) for x in m]
                em = [_dot((o["a_ab"] * mask_ref[lev]).astype(BF16), x).astype(BF16)
                      for o, x in zip(ops, mb)]
                m = [x + _dot(xb, y) for x, xb, y in zip(m, mb, em)]
            return m

        def finish(group, ops, m):
            t = [t_ref[bi * npair + q] for bi, q in group]
            xtv = [_dot(jnp.concatenate([o["xar"], o["a_k"]], axis=1),
                        jnp.concatenate([x.astype(BF16), o["vs"]], axis=0)) for o, x in zip(ops, t)]
            u = [_dot(x.astype(BF16), y[0:2 * n].astype(BF16)).astype(BF16) for x, y in zip(m, xtv)]
            for i, (bi, q) in enumerate(group):
                o = ops[i]
                ys = xtv[i][2 * n:4 * n] + _dot(o["a_rb"], u[i])
                y_ref[bi, sl, q * LANES:(q + 1) * LANES] = ys[0:n] + ys[n:2 * n]
                t_ref[bi * npair + q] = \
                    t[i] * jnp.transpose(jnp.broadcast_to(o["decay"], (LANES, LANES))) \
                    + _dot_tn(jnp.concatenate([o["bt"], o["kt"]], axis=0),
                              jnp.concatenate([u[i], o["vs"]], axis=0))

        group = [(bi, q) for bi in range(nb) for q in range(npair)]
        ops = scores(group)
        finish(group, ops, inverse(ops))
        return carry

    lax.fori_loop(0, ts // n, chunk, 0)


def _wkv(r, k, v, kk, a, lw, *, ts, npair, nb):
    b, s, d = r.shape
    wd = npair * LANES
    spec = pl.BlockSpec((nb, ts, wd), lambda bi, p, i: (bi, i, p))
    return pl.pallas_call(
        _wkv_kernel,
        grid=(b // nb, d // wd, s // ts),
        in_specs=[spec] * 6,
        out_specs=spec,
        out_shape=jax.ShapeDtypeStruct((b, s, d), F32),
        scratch_shapes=[pltpu.VMEM((nb * npair, LANES, LANES), F32),
                        pltpu.VMEM((WKV_LEVELS + 3, LANES, LANES), F32)],
        compiler_params=_cparams(("parallel", "parallel", "arbitrary")),
        name="wkv7",
    )(r, k, v, kk, a, lw)


def _even_out_kernel(ya_ref, y_ref, bonus_ref, g_ref, lnw_ref, lnb_ref, w_ref, h_ref, o_ref):
    ka = ya_ref.shape[1]
    y = y_ref[...]
    inv_n = 1.0 / RWKV_HEAD
    s1 = _head_sum(y)
    s2 = _head_sum(y * y)
    acc = h_ref[...] + _dot(ya_ref[...], w_ref[0:ka, :])
    mean = s1 * inv_n
    var = jnp.maximum(s2 * inv_n - mean * mean, 0.0)
    yn = (y - mean) * lax.rsqrt(var + GN_EPS) * lnw_ref[...] + lnb_ref[...]
    yb = ((yn + bonus_ref[...].astype(F32)) * g_ref[...].astype(F32)).astype(BF16)
    o_ref[...] = acc + _dot(yb, w_ref[ka:, :])


def _even_out(ya, y, bonus, g, ln_w, ln_b, w, layer, h, *, tm):
    m, n = h.shape
    ka, kb = ya.shape[1], y.shape[1]
    blk = lambda width: pl.BlockSpec((tm, width), lambda i: (i, 0))
    row = pl.BlockSpec((1, kb), lambda i: (0, 0))
    return pl.pallas_call(
        _even_out_kernel,
        grid=(m // tm,),
        in_specs=[blk(ka), blk(kb), blk(kb), blk(kb), row, row,
                  pl.BlockSpec((None, ka + kb, n), lambda i: (layer, 0, 0)), blk(n)],
        out_specs=blk(n),
        out_shape=jax.ShapeDtypeStruct((m, n), F32),
        compiler_params=_cparams(("parallel",)),
        name="even_out",
    )(ya, y, bonus, g, ln_w, ln_b, w, h)


def _pool_kernel(u_ref, up_ref, w_ref, sc_ref, o_ref, buf_ref):
    ts = o_ref.shape[1]
    hist = max(POOL_WINDOWS)
    i = pl.program_id(1)
    u = u_ref[0].astype(F32)
    _fill_history(buf_ref, up_ref[0].astype(F32), u, hist, i == 0)
    pos = i * ts + lax.broadcasted_iota(jnp.int32, (ts, 1), 0)
    outs = []
    for gi, win in enumerate(POOL_WINDOWS):
        cols = pl.ds(gi * POOL_GROUP, POOL_GROUP)
        acc = buf_ref[pl.ds(hist, ts), cols]
        for j in range(1, win):
            acc = acc + buf_ref[pl.ds(hist - j, ts), cols]
        count = jnp.minimum(pos + 1, win).astype(F32)
        dg = acc / count - u[:, gi * POOL_GROUP:(gi + 1) * POOL_GROUP]
        outs.append(_dot(dg.astype(BF16), w_ref[gi]))
    y = jnp.concatenate(outs, axis=1) * sc_ref[...]
    o_ref[0] = y.astype(o_ref.dtype)


def _pool_mixer(p, pool_w, pool_scale, *, ts):
    b, s, _ = p.shape
    hist = max(POOL_WINDOWS)
    rb = ts // hist
    return pl.pallas_call(
        _pool_kernel,
        grid=(b, s // ts),
        in_specs=[
            pl.BlockSpec((1, ts, POOL_DIM), lambda bi, i: (bi, i, 0)),
            pl.BlockSpec((1, hist, POOL_DIM), lambda bi, i: (bi, jnp.maximum(i * rb - 1, 0), 0)),
            pl.BlockSpec((len(POOL_WINDOWS), POOL_GROUP, POOL_GROUP), lambda bi, i: (0, 0, 0)),
            pl.BlockSpec((1, POOL_DIM), lambda bi, i: (0, 0)),
        ],
        out_specs=pl.BlockSpec((1, ts, POOL_DIM), lambda bi, i: (bi, i, 0)),
        out_shape=jax.ShapeDtypeStruct((b, s, POOL_DIM), BF16),
        scratch_shapes=[pltpu.VMEM((ts + hist, POOL_DIM), F32)],
        compiler_params=_cparams(("parallel", "parallel")),
        name="pool_mixer",
    )(p, p, pool_w, pool_scale)


def _mla_prep_kernel(ql_ref, kvl_ref, kpe_ref, kper_ref, ang_ref, qn_ref, kvn_ref,
                     wq_ref, wqr_ref, wk_ref, wv_ref, q_out, k_out, v_out):
    cos = jnp.cos(ang_ref[...])
    sin = jnp.sin(ang_ref[...])
    qn = _rms(ql_ref[...].astype(F32), qn_ref[...]).astype(BF16)
    kvn = _rms(kvl_ref[...].astype(F32), kvn_ref[...]).astype(BF16)
    scale = (QK_NOPE + QK_ROPE) ** -0.5 * LOG2_E
    qa = _dot(qn, wq_ref[...])
    qr = _dot(qn, wqr_ref[...])
    ka = _dot(kvn, wk_ref[...])
    kpe = kpe_ref[...].astype(F32) * cos + kper_ref[...].astype(F32) * sin
    for h in range(MLA_HEADS):
        lo = h * QK_PAD
        q_out[:, lo:lo + LANES] = (qa[:, lo:lo + LANES] * scale).astype(BF16)
        q_pe = qa[:, lo + LANES:lo + 2 * LANES] * cos + qr[:, h * LANES:(h + 1) * LANES] * sin
        q_out[:, lo + LANES:lo + 2 * LANES] = (q_pe * scale).astype(BF16)
        k_out[:, lo:lo + LANES] = ka[:, h * QK_NOPE:(h + 1) * QK_NOPE].astype(BF16)
        k_out[:, lo + LANES:lo + 2 * LANES] = kpe.astype(BF16)
    vt = _dot_nt(wv_ref[...], kvn).astype(BF16)
    ones = jnp.ones((ATTN_SUM_ROWS, vt.shape[1]), BF16)
    for h in range(MLA_HEADS):
        v_out[h * VT_ROWS:h * VT_ROWS + V_HEAD, :] = vt[h * V_HEAD:(h + 1) * V_HEAD, :]
        v_out[h * VT_ROWS + V_HEAD:(h + 1) * VT_ROWS, :] = ones


def _mla_prep(p, ang, q_norm, kv_norm, wq, wqr, wk, wv, layer, *, tm):
    m = p.shape[0]
    hq = MLA_HEADS * QK_PAD
    hv = MLA_HEADS * VT_ROWS
    row = lambda a: pl.BlockSpec(a.shape, lambda i: (0, 0))
    full = lambda a: pl.BlockSpec((None,) + a.shape[1:], lambda i: (layer, 0, 0))
    return pl.pallas_call(
        _mla_prep_kernel,
        grid=(m // tm,),
        in_specs=[
            pl.BlockSpec((tm, Q_LORA), lambda i: (i, 1)),
            pl.BlockSpec((tm, KV_LORA), lambda i: (i, 2)),
            pl.BlockSpec((tm, LANES), lambda i: (i, 12)),
            pl.BlockSpec((tm, LANES), lambda i: (i, 13)),
            pl.BlockSpec((tm, LANES), lambda i: (i, 0)),
            row(q_norm), row(kv_norm), full(wq), full(wqr), full(wk), full(wv),
        ],
        out_specs=[pl.BlockSpec((tm, hq), lambda i: (i, 0)),
                   pl.BlockSpec((tm, hq), lambda i: (i, 0)),
                   pl.BlockSpec((hv, tm), lambda i: (0, i))],
        out_shape=[jax.ShapeDtypeStruct((m, hq), BF16),
                   jax.ShapeDtypeStruct((m, hq), BF16),
                   jax.ShapeDtypeStruct((hv, m), BF16)],
        compiler_params=_cparams(("parallel",)),
        name="mla_prep",
    )(p, p, p, p, ang, q_norm, kv_norm, wq, wqr, wk, wv)


def _attn_kernel(q_ref, k_ref, vt_ref, o_ref, *, tq, tk, nh):
    qi = pl.program_id(2)
    heads = range(nh)
    qs = [q_ref[0, :, h * QK_PAD:(h + 1) * QK_PAD] for h in heads]
    def keys(j):
        return pl.ds(pl.multiple_of(j * tk, tk), tk)

    def scores(j):
        return [_dot_nt(k_ref[0, keys(j), h * QK_PAD:(h + 1) * QK_PAD], qs[h]) for h in heads]

    def softmax_pv(j, s, m, acc, masked):
        if masked:
            key = j * tk + lax.broadcasted_iota(jnp.int32, (tk, tq), 0)
            qry = qi * tq + lax.broadcasted_iota(jnp.int32, (tk, tq), 1)
            s = [jnp.where(key <= qry, x, -1e30) for x in s]
        m_new = [jnp.maximum(a, jnp.max(x, axis=0, keepdims=True)) for a, x in zip(m, s)]
        alpha = [jnp.exp2(a - b) for a, b in zip(m, m_new)]
        p = [jnp.exp2(x - a).astype(BF16) for x, a in zip(s, m_new)]
        pv = [_dot(vt_ref[h * VT_ROWS:(h + 1) * VT_ROWS, keys(j)], p[h]) for h in heads]
        return m_new, [a * b + x for a, b, x in zip(alpha, acc, pv)]

    def two_blocks(j, m, acc, last_masked):
        s0, s1 = scores(j), scores(j + 1)
        m, acc = softmax_pv(j, s0, m, acc, False)
        return softmax_pv(j + 1, s1, m, acc, last_masked)

    def finish(acc):
        for h in heads:
            out = acc[h][0:V_HEAD] / acc[h][V_HEAD:V_HEAD + 1]
            o_ref[0, :, h * V_HEAD:(h + 1) * V_HEAD] = jnp.transpose(out).astype(o_ref.dtype)

    init = ([jnp.full((1, tq), -1e30, F32) for _ in heads],
            [jnp.zeros((V_HEAD + ATTN_SUM_ROWS, tq), F32) for _ in heads])
    pairs = qi // 2
    m, acc = lax.fori_loop(0, pairs, lambda i, c: two_blocks(2 * i, *c, False), init)

    @pl.when(qi % 2 == 1)
    def _():
        finish(two_blocks(qi - 1, m, acc, True)[1])

    @pl.when(qi % 2 == 0)
    def _():
        finish(softmax_pv(qi, scores(qi), m, acc, True)[1])


def _attention(q, k, vt, *, tq, tk, nh):
    b, s, _ = q.shape
    assert tq == tk, "one key block per query tile sits on the causal diagonal"
    return pl.pallas_call(
        functools.partial(_attn_kernel, tq=tq, tk=tk, nh=nh),
        grid=(b, MLA_HEADS // nh, s // tq),
        in_specs=[
            pl.BlockSpec((1, tq, nh * QK_PAD), lambda bi, h, i: (bi, i, h)),
            pl.BlockSpec((1, s, nh * QK_PAD), lambda bi, h, i: (bi, 0, h)),
            pl.BlockSpec((nh * VT_ROWS, s), lambda bi, h, i: (h, bi)),
        ],
        out_specs=pl.BlockSpec((1, tq, nh * V_HEAD), lambda bi, h, i: (bi, i, h)),
        out_shape=jax.ShapeDtypeStruct((b, s, MLA_HEADS * V_HEAD), BF16),
        compiler_params=_cparams(("parallel", "parallel", "arbitrary")),
        name="mla_attention",
    )(q, k, vt)


def _pad_last(w, n):
    return jnp.pad(w, [(0, 0)] * (w.ndim - 1) + [(0, n - w.shape[-1])])


def _pad_rows(w, n):
    return jnp.pad(w, ((0, n - w.shape[0]), (0, 0)))


def _rot_cols(w):
    half = w.shape[-1] // 2
    return jnp.concatenate([-w[..., half:], w[..., :half]], axis=-1)


def _cast_pad_kernel(main_ref, tail_ref, o_ref):
    split = main_ref.shape[1]
    o_ref[:, 0:split] = main_ref[...].astype(o_ref.dtype)
    o_ref[:, split:] = tail_ref[...]


def _cast_pad(w, cols, *, tr):
    layers, rows, n = w.shape
    split = n // LANES * LANES
    tail = _pad_last(w[..., split:], cols - split).astype(BF16)
    return pl.pallas_call(
        _cast_pad_kernel,
        grid=(layers, rows // tr),
        in_specs=[pl.BlockSpec((None, tr, split), lambda l, i: (l, i, 0)),
                  pl.BlockSpec((None, tr, cols - split), lambda l, i: (l, i, 0))],
        out_specs=pl.BlockSpec((None, tr, cols), lambda l, i: (l, i, 0)),
        out_shape=jax.ShapeDtypeStruct((layers, rows, cols), BF16),
        compiler_params=_cparams(("parallel", "parallel")),
        name="cast_pad",
    )(w, tail)


def _even_weights(w_in, mu):
    r = 3 * RWKV_DIM
    w = _cast_pad(w_in, EV_COLS, tr=256)
    return w, mu[:, None, :r], _pad_last(mu[:, r:], EV_LORA)[:, None, :]


def _odd_weights(w_in, w_uq, w_ukv):
    n = w_in.shape[0]
    o3 = POOL_DIM + Q_LORA + KV_LORA
    kpe = w_in[..., o3:]
    w = jnp.concatenate([w_in[..., :o3], _pad_last(kpe, LANES), _pad_last(_rot_cols(kpe), LANES)],
                        axis=-1).astype(BF16)
    uq = w_uq.reshape(n, Q_LORA, MLA_HEADS, QK_NOPE + QK_ROPE)
    wq = _pad_last(uq, QK_PAD).reshape(n, Q_LORA, MLA_HEADS * QK_PAD).astype(BF16)
    wqr = _pad_last(_rot_cols(uq[..., QK_NOPE:]), LANES)
    wqr = wqr.reshape(n, Q_LORA, MLA_HEADS * LANES).astype(BF16)
    ukv = w_ukv.reshape(n, KV_LORA, MLA_HEADS, QK_NOPE + V_HEAD)
    wk = ukv[..., :QK_NOPE].reshape(n, KV_LORA, MLA_HEADS * QK_NOPE).astype(BF16)
    wv = ukv[..., QK_NOPE:].reshape(n, KV_LORA, MLA_HEADS * V_HEAD)
    return w, wq, wqr, wk, jnp.swapaxes(wv, 1, 2).astype(BF16)


def _tile(n, pref):
    return pref if n % pref == 0 else n


def _tiles(b, s):
    m = b * s
    return dict(
        rows_wide=_tile(m, 1024),
        rows=_tile(m, 512),
        ffn_cols=512,
        even_in_cols=EV_COLS // 3,
        odd_in_cols=OD_COLS,
        seq_mixers=_tile(s, 1024),
        seq_prep=_tile(s, 256),
        seq_wkv=_tile(s, 256),
        wkv_pairs=RWKV_DIM // LANES,
        wkv_batch=2 if b % 2 == 0 else 1,
        attn=_tile(s, 512),
        attn_heads=4,
    )


def kernel(x, positions, ev_norm, ev_w_in, ev_conv_w, ev_mu, ev_w0, ev_w2, ev_a0, ev_a2, ev_g2, ev_k_k, ev_k_a, ev_r_k, ev_ln_w, ev_ln_b, ev_w_out, od_norm, od_w_in, od_pool_w, od_pool_scale, od_q_norm, od_w_uq, od_kv_norm, od_w_ukv, od_w_out, ffn_norm, ffn_w_gate, ffn_w_up, ffn_w_down, final_norm):
    b, s, d = x.shape
    m = b * s
    depth = ffn_norm.shape[0]
    t = _tiles(b, s)

    inv = 1.0 / (ROPE_THETA ** (jnp.arange(0, QK_ROPE, 2, dtype=F32) / QK_ROPE))
    inv = jnp.concatenate([inv, inv, jnp.zeros((LANES - QK_ROPE,), F32)])
    ang = (positions.astype(F32)[..., None] * inv).reshape(m, LANES)

    ev_w, ev_mu_main, ev_mu_lora = _even_weights(ev_w_in, ev_mu)
    od_w, wq, wqr, wk, wv = _odd_weights(od_w_in, od_w_uq, od_w_ukv)
    ev_wo = ev_w_out.astype(BF16)
    od_wo = od_w_out.astype(BF16)
    wg, wu, wd = ffn_w_gate.astype(BF16), ffn_w_up.astype(BF16), ffn_w_down
    row = lambda t: t[None, :]

    h = x.reshape(m, d)
    for layer in range(depth):
        j = layer // 2
        if layer % 2 == 0:
            p = _norm_matmul(h, row(ev_norm[j]), ev_w, j, tm=t["rows_wide"], tn=t["even_in_cols"],
                             out_dtype=BF16)
            p = p.reshape(b, s, EV_COLS)
            ya = _conv_mixer(p, ev_conv_w[j].T, ts=t["seq_mixers"], tc=CONV_DIM)
            r, k, v, kk, a, lw, g, bonus = _rwkv_prep(
                p, ev_mu_main[j], ev_mu_lora[j], row(ev_w0[j]),
                _pad_rows(ev_w2[j], LANES).astype(BF16), row(ev_a0[j]),
                jnp.pad(ev_a2[j], ((W_LORA, LANES - W_LORA - A_LORA), (0, 0))).astype(BF16),
                _pad_rows(ev_g2[j], 2 * LANES).astype(BF16),
                row(ev_k_k[j]), row(ev_k_a[j]), row(ev_r_k[j].reshape(-1)), ts=t["seq_prep"])
            y = _wkv(r, k, v, kk, a, lw, ts=t["seq_wkv"], npair=t["wkv_pairs"], nb=t["wkv_batch"])
            h = _even_out(ya.reshape(m, -1), y.reshape(m, -1), bonus.reshape(m, -1),
                          g.reshape(m, -1), row(ev_ln_w[j]), row(ev_ln_b[j]), ev_wo, j, h,
                          tm=t["rows"])
        else:
            p = _norm_matmul(h, row(od_norm[j]), od_w, j, tm=t["rows_wide"], tn=t["odd_in_cols"],
                             out_dtype=BF16)
            yc = _pool_mixer(p.reshape(b, s, OD_COLS), od_pool_w[j].astype(BF16),
                             row(od_pool_scale[j]), ts=t["seq_mixers"])
            q, kq, vv = _mla_prep(p, ang, row(od_q_norm[j]), row(od_kv_norm[j]),
                                  wq, wqr, wk, wv, j, tm=t["rows"])
            hq = MLA_HEADS * QK_PAD
            yd = _attention(q.reshape(b, s, hq), kq.reshape(b, s, hq), vv, tq=t["attn"],
                            tk=t["attn"], nh=t["attn_heads"])
            h = _proj_residual(yc.reshape(m, -1), yd.reshape(m, -1), od_wo, j, h, tm=t["rows"], tn=d)
        h = _ffn(h, row(ffn_norm[layer]), row(final_norm), wg, wu, wd, layer, tm=t["rows_wide"],
                 tf=t["ffn_cols"], norm_out=layer == depth - 1)
    return h.reshape(b, s, d)
```

```python
import functools

import jax
import jax.numpy as jnp
from jax import lax
from jax.experimental import pallas as pl
from jax.experimental.pallas import tpu as pltpu

F32 = jnp.float32
BF16 = jnp.bfloat16

NORM_EPS = 1e-6
CONV_DIM = 1024
RWKV_DIM = 1024
RWKV_HEAD = 64
W_LORA = 64
A_LORA = 64
GN_EPS = 64e-5
POOL_WINDOWS = (2, 4, 8, 16)
POOL_GROUP = 128
POOL_DIM = 512
MLA_HEADS = 12
Q_LORA = 512
KV_LORA = 512
QK_NOPE = 128
QK_ROPE = 64
V_HEAD = 128
ROPE_THETA = 10000.0
LOG2_E = 1.4426950408889634
EXP_NEG_HALF = 0.6065306597126334

LANES = 128
HIST = 16
VMEM_LIMIT = 60 * 1024 * 1024
QK_PAD = 256
ATTN_SUM_ROWS = 16
VT_ROWS = V_HEAD + ATTN_SUM_ROWS
ATTN_Q_SPLIT = 256
WKV_CHUNK = 64
EV_COLS = 6912
EV_LORA = 512
OD_COLS = 1792


def _cparams(sem):
    return pltpu.CompilerParams(dimension_semantics=sem, vmem_limit_bytes=VMEM_LIMIT)


def _rms(x, g):
    ms = jnp.mean(x * x, axis=-1, keepdims=True)
    return x * lax.rsqrt(ms + NORM_EPS) * g


def _sigmoid(x):
    return 1.0 / (1.0 + jnp.exp(-x))


def _dot(a, b):
    return jnp.dot(a, b, preferred_element_type=F32)


def _dot_nt(a, b):
    return lax.dot_general(a, b, (((1,), (1,)), ((), ())), preferred_element_type=F32)


def _dot_tn(a, b):
    return lax.dot_general(a, b, (((0,), (0,)), ((), ())), preferred_element_type=F32)


def _norm_mm_kernel(x_ref, g_ref, w_ref, o_ref, xn_ref):
    @pl.when(pl.program_id(1) == 0)
    def _():
        xn_ref[...] = _rms(x_ref[...], g_ref[...]).astype(BF16)

    o_ref[...] = _dot(xn_ref[...], w_ref[...]).astype(o_ref.dtype)


def _norm_matmul(x, g, w, layer, *, tm, tn, out_dtype):
    m, k = x.shape
    n = w.shape[2]
    return pl.pallas_call(
        _norm_mm_kernel,
        grid=(m // tm, n // tn),
        in_specs=[
            pl.BlockSpec((tm, k), lambda i, j: (i, 0)),
            pl.BlockSpec((1, k), lambda i, j: (0, 0)),
            pl.BlockSpec((None, k, tn), lambda i, j: (layer, 0, j)),
        ],
        out_specs=pl.BlockSpec((tm, tn), lambda i, j: (i, j)),
        out_shape=jax.ShapeDtypeStruct((m, n), out_dtype),
        scratch_shapes=[pltpu.VMEM((tm, k), BF16)],
        compiler_params=_cparams(("parallel", "arbitrary")),
        name="norm_matmul",
    )(x, g, w)


def _proj_res_kernel(a_ref, b_ref, w_ref, h_ref, o_ref):
    ka = a_ref.shape[1]
    o_ref[...] = h_ref[...] + _dot(a_ref[...], w_ref[0:ka, :]) + _dot(b_ref[...], w_ref[ka:, :])


def _proj_residual(a, b, w, layer, h, *, tm, tn):
    m, n = h.shape
    ka, kb = a.shape[1], b.shape[1]
    return pl.pallas_call(
        _proj_res_kernel,
        grid=(m // tm, n // tn),
        in_specs=[
            pl.BlockSpec((tm, ka), lambda i, j: (i, 0)),
            pl.BlockSpec((tm, kb), lambda i, j: (i, 0)),
            pl.BlockSpec((None, ka + kb, tn), lambda i, j: (layer, 0, j)),
            pl.BlockSpec((tm, tn), lambda i, j: (i, j)),
        ],
        out_specs=pl.BlockSpec((tm, tn), lambda i, j: (i, j)),
        out_shape=jax.ShapeDtypeStruct((m, n), F32),
        compiler_params=_cparams(("parallel", "arbitrary")),
        name="proj_residual",
    )(a, b, w, h)


def _ffn_kernel(x_ref, g_ref, gout_ref, wg_ref, wu_ref, wd_ref, o_ref, xn_ref, *, norm_out):
    @pl.when(pl.program_id(1) == 0)
    def _():
        x = x_ref[...]
        xn_ref[...] = _rms(x, g_ref[...]).astype(BF16)
        o_ref[...] = x

    xn = xn_ref[...]
    a = _dot(xn, wg_ref[...])
    b = _dot(xn, wu_ref[...])
    act = (a * _sigmoid(a) * b).astype(BF16)
    o_ref[...] += _dot(act, wd_ref[...].astype(BF16))

    if norm_out:
        @pl.when(pl.program_id(1) == pl.num_programs(1) - 1)
        def _():
            o_ref[...] = _rms(o_ref[...], gout_ref[...])


def _ffn(x, g, g_out, wg, wu, wd, layer, *, tm, tf, norm_out):
    m, d = x.shape
    f = wg.shape[2]
    return pl.pallas_call(
        functools.partial(_ffn_kernel, norm_out=norm_out),
        grid=(m // tm, f // tf),
        in_specs=[
            pl.BlockSpec((tm, d), lambda i, j: (i, 0)),
            pl.BlockSpec((1, d), lambda i, j: (0, 0)),
            pl.BlockSpec((1, d), lambda i, j: (0, 0)),
            pl.BlockSpec((None, d, tf), lambda i, j: (layer, 0, j)),
            pl.BlockSpec((None, d, tf), lambda i, j: (layer, 0, j)),
            pl.BlockSpec((None, tf, d), lambda i, j: (layer, j, 0)),
        ],
        out_specs=pl.BlockSpec((tm, d), lambda i, j: (i, 0)),
        out_shape=jax.ShapeDtypeStruct((m, d), F32),
        scratch_shapes=[pltpu.VMEM((tm, d), BF16)],
        compiler_params=_cparams(("parallel", "arbitrary")),
        name="ffn",
    )(x, g, g_out, wg, wu, wd)


def _fill_history(buf_ref, prev, cur, hist, is_first):
    buf_ref[pl.ds(0, hist), :] = jnp.where(is_first, 0.0, prev)
    buf_ref[pl.ds(hist, cur.shape[0]), :] = cur


def _conv_kernel(bg_ref, cg_ref, hh_ref, cgp_ref, hhp_ref, w_ref, o_ref, buf_ref):
    ts = o_ref.shape[1]
    u = cg_ref[0].astype(F32) * hh_ref[0].astype(F32)
    up = cgp_ref[0].astype(F32) * hhp_ref[0].astype(F32)
    _fill_history(buf_ref, up, u, HIST, pl.program_id(1) == 0)
    u1 = buf_ref[pl.ds(HIST - 1, ts), :]
    u2 = buf_ref[pl.ds(HIST - 2, ts), :]
    w = w_ref[...]
    y = w[2:3] * u + w[1:2] * u1 + w[0:1] * u2
    o_ref[0] = (bg_ref[0].astype(F32) * y).astype(o_ref.dtype)


def _conv_mixer(p, conv_w_t, *, ts, tc):
    b, s, _ = p.shape
    nc = CONV_DIM // tc
    rb = ts // HIST

    def cur(off):
        return pl.BlockSpec((1, ts, tc), lambda bi, i, c: (bi, i, off * nc + c))

    def prev(off):
        return pl.BlockSpec((1, HIST, tc),
                            lambda bi, i, c: (bi, jnp.maximum(i * rb - 1, 0), off * nc + c))

    return pl.pallas_call(
        _conv_kernel,
        grid=(b, s // ts, nc),
        in_specs=[cur(0), cur(1), cur(2), prev(1), prev(2),
                  pl.BlockSpec((3, tc), lambda bi, i, c: (0, c))],
        out_specs=pl.BlockSpec((1, ts, tc), lambda bi, i, c: (bi, i, c)),
        out_shape=jax.ShapeDtypeStruct((b, s, CONV_DIM), BF16),
        scratch_shapes=[pltpu.VMEM((ts + HIST, tc), F32)],
        compiler_params=_cparams(("parallel", "parallel", "parallel")),
        name="conv_mixer",
    )(p, p, p, p, p, conv_w_t)


def _head_sum(x, split=True):
    r = lax.broadcasted_iota(jnp.int32, (LANES, LANES), 0) // RWKV_HEAD
    c = lax.broadcasted_iota(jnp.int32, (LANES, LANES), 1) // RWKV_HEAD
    ones = jnp.where(r == c, 1.0, 0.0).astype(BF16)
    hi = x.astype(BF16)
    lo = (x - hi.astype(F32)).astype(BF16) if split else None
    parts = []
    for g in range(x.shape[1] // LANES):
        sl = slice(g * LANES, (g + 1) * LANES)
        part = _dot(hi[:, sl], ones)
        parts.append(part + _dot(lo[:, sl], ones) if split else part)
    return jnp.concatenate(parts, axis=1)


def _token_shift(xb, prev_rows, is_first):
    ts = xb.shape[0]
    r = lax.broadcasted_iota(jnp.int32, (ts, ts), 0)
    c = lax.broadcasted_iota(jnp.int32, (ts, ts), 1)
    xs = _dot(jnp.where(r == c + 1, 1.0, 0.0).astype(BF16), xb)
    hist = prev_rows.shape[0]
    last = jnp.where(is_first, 0.0, prev_rows[hist - 1:hist, :].astype(F32))
    top = jnp.where(lax.broadcasted_iota(jnp.int32, (HIST, 1), 0) == 0, last, xs[0:HIST])
    return jnp.concatenate([top, xs[HIST:]], axis=0)


def _rwkv_prep_kernel(x_ref, xl_ref, xp_ref, xlp_ref, mu_ref, mul_ref, w0_ref, w2_ref, a0_ref,
                      a2_ref, g2_ref, kk_ref, ka_ref, rk_ref,
                      r_out, k_out, v_out, kkn_out, a_out, lw_out, g_out, bonus_out):
    first = pl.program_id(1) == 0
    d = RWKV_DIM

    x = x_ref[0].astype(F32)
    xm = x + (_token_shift(x_ref[0], xp_ref[0], first) - x) * mu_ref[...]
    xl = xl_ref[0].astype(F32)
    xlm = xl + (_token_shift(xl_ref[0], xlp_ref[0], first) - xl) * mul_ref[...]

    r = xm[:, 0:d]
    k = xm[:, d:2 * d]
    v = xm[:, 2 * d:3 * d]
    xwa = xlm[:, 0:LANES]
    xg = xlm[:, LANES:3 * LANES]

    y = w0_ref[...] + _dot(jnp.tanh(xwa).astype(BF16), w2_ref[...])
    lw = -EXP_NEG_HALF * _sigmoid(y)
    a = _sigmoid(a0_ref[...] + _dot(xwa.astype(BF16), a2_ref[...]))
    g = _dot(_sigmoid(xg).astype(BF16), g2_ref[...])

    kk = k * kk_ref[...]
    kk = kk * lax.rsqrt(jnp.maximum(_head_sum(kk * kk, split=False), 1e-24))
    k = k * (1.0 + (a - 1.0) * ka_ref[...])
    bonus = _head_sum(r * k * rk_ref[...], split=False) * v

    r_out[0] = r.astype(r_out.dtype)
    k_out[0] = k.astype(k_out.dtype)
    v_out[0] = v.astype(v_out.dtype)
    kkn_out[0] = kk.astype(kkn_out.dtype)
    a_out[0] = a.astype(a_out.dtype)
    lw_out[0] = lw
    g_out[0] = g.astype(g_out.dtype)
    bonus_out[0] = bonus.astype(bonus_out.dtype)


def _rwkv_prep(p, mu, mul, w0, w2, a0, a2, g2, k_k, k_a, r_k, *, ts):
    b, s, _ = p.shape
    d = RWKV_DIM
    rb = ts // HIST
    row = lambda shape: pl.BlockSpec(shape, lambda bi, i: (0, 0))
    out = pl.BlockSpec((1, ts, d), lambda bi, i: (bi, i, 0))
    return pl.pallas_call(
        _rwkv_prep_kernel,
        grid=(b, s // ts),
        in_specs=[
            pl.BlockSpec((1, ts, 3 * d), lambda bi, i: (bi, i, 1)),
            pl.BlockSpec((1, ts, EV_LORA), lambda bi, i: (bi, i, 6 * d // EV_LORA)),
            pl.BlockSpec((1, HIST, 3 * d), lambda bi, i: (bi, jnp.maximum(i * rb - 1, 0), 1)),
            pl.BlockSpec((1, HIST, EV_LORA),
                         lambda bi, i: (bi, jnp.maximum(i * rb - 1, 0), 6 * d // EV_LORA)),
            row((1, 3 * d)), row((1, EV_LORA)), row((1, d)), row((LANES, d)), row((1, d)),
            row((LANES, d)), row((2 * LANES, d)), row((1, d)), row((1, d)), row((1, d)),
        ],
        out_specs=[out] * 8,
        out_shape=[jax.ShapeDtypeStruct((b, s, d), BF16)] * 5
        + [jax.ShapeDtypeStruct((b, s, d), F32)] + [jax.ShapeDtypeStruct((b, s, d), BF16)] * 2,
        compiler_params=_cparams(("parallel", "parallel")),
        name="rwkv_prep",
    )(p, p, p, p, mu, mul, w0, w2, a0, a2, g2, k_k, k_a, r_k)


WKV_LEVELS = 6
_MASK_STRICT = WKV_LEVELS
_MASK_INCL = WKV_LEVELS + 1
_MASK_EYE = WKV_LEVELS + 2


def _wkv_kernel(r_ref, k_ref, v_ref, kk_ref, a_ref, lw_ref, y_ref, t_ref, mask_ref):
    n = WKV_CHUNK
    nb, ts = y_ref.shape[0], y_ref.shape[1]
    npair = y_ref.shape[2] // LANES

    @pl.when(pl.program_id(2) == 0)
    def _():
        t_ref[...] = jnp.zeros_like(t_ref)
        row = lax.broadcasted_iota(jnp.int32, (2 * n, 2 * n), 0)
        col = lax.broadcasted_iota(jnp.int32, (2 * n, 2 * n), 1)
        same = (row // n) == (col // n)
        x = row ^ col
        level = jnp.zeros_like(x)
        for bit in range(1, WKV_LEVELS):
            level = level + jnp.where(x >= (1 << bit), 1, 0)
        level = jnp.where(same & (row > col), level, -1)
        for lev in range(WKV_LEVELS):
            mask_ref[lev] = jnp.where(level == lev, 1.0, 0.0)
        mask_ref[_MASK_STRICT] = jnp.where(level >= 0, 1.0, 0.0)
        mask_ref[_MASK_INCL] = jnp.where(same & (row >= col), 1.0, 0.0)
        mask_ref[_MASK_EYE] = jnp.where(row == col, 1.0, 0.0)

    head0 = lax.broadcasted_iota(jnp.int32, (n, LANES), 1) < RWKV_HEAD
    tri = jnp.where(lax.broadcasted_iota(jnp.int32, (n, n), 0)
                    >= lax.broadcasted_iota(jnp.int32, (n, n), 1), 1.0, 0.0).astype(BF16)

    def stack(t):
        z = jnp.zeros_like(t)
        return jnp.concatenate([jnp.where(head0, t, z), jnp.where(head0, z, t)], axis=0).astype(BF16)

    def operands(sl, bi, q, cw):
        lanes = slice(q * LANES, (q + 1) * LANES)
        r = r_ref[bi, sl, lanes].astype(F32)
        k = k_ref[bi, sl, lanes].astype(F32)
        kk = kk_ref[bi, sl, lanes].astype(F32)
        lw = lw_ref[bi, sl, lanes]
        cw_end = cw[n - 1:n, :]
        p_inv = jnp.exp(-cw)
        p_to_end = jnp.exp(cw_end - cw)
        kka = kk * a_ref[bi, sl, lanes].astype(F32)
        xar = jnp.concatenate([stack(-kk * jnp.exp(cw - lw)), stack(r * jnp.exp(cw))], axis=0)
        ybk = jnp.concatenate([stack(kka * p_inv), stack(k * p_inv)], axis=0)
        return dict(xar=xar, ybk=ybk, bt=stack(kka * p_to_end), kt=stack(k * p_to_end),
                    vs=stack(v_ref[bi, sl, lanes].astype(F32)), decay=jnp.exp(cw_end))

    def cumulative(sl, bi):
        lw = lw_ref[bi, sl, :]
        hi = lw.astype(BF16)
        r1 = lw - hi.astype(F32)
        mid = r1.astype(BF16)
        lo = (r1 - mid.astype(F32)).astype(BF16)
        return _dot(tri, hi) + _dot(tri, mid) + _dot(tri, lo)

    def chunk(c, carry):
        sl = pl.ds(pl.multiple_of(c * n, n), n)
        cw = [cumulative(sl, bi) for bi in range(nb)]
        strict = mask_ref[_MASK_STRICT] > 0.5
        incl = mask_ref[_MASK_INCL] > 0.5

        def scores(group):
            ops = [operands(sl, bi, q, cw[bi][:, q * LANES:(q + 1) * LANES]) for bi, q in group]
            sc = [_dot_nt(o["xar"], o["ybk"]) for o in ops]
            for o, s in zip(ops, sc):
                o["a_ab"] = jnp.where(strict, s[0:2 * n, 0:2 * n], 0.0)
                o["a_k"] = jnp.concatenate([jnp.where(strict, s[0:2 * n, 2 * n:4 * n], 0.0),
                                            jnp.where(incl, s[2 * n:4 * n, 2 * n:4 * n], 0.0)],
                                           axis=0).astype(BF16)
                o["a_rb"] = jnp.where(incl, s[2 * n:4 * n, 0:2 * n], 0.0).astype(BF16)
            return ops

        def inverse(ops):
            m = [mask_ref[_MASK_EYE] + o["a_ab"] * mask_ref[0] for o in ops]
            for lev in range(1, WKV_LEVELS):
                mb = [x.astype(BF16) for x in m]
                em = [_dot((o["a_ab"] * mask_ref[lev]).astype(BF16), x).astype(BF16)
                      for o, x in zip(ops, mb)]
                m = [x + _dot(xb, y) for x, xb, y in zip(m, mb, em)]
            return m

        def finish(group, ops, m):
            t = [t_ref[bi * npair + q] for bi, q in group]
            xtv = [_dot(jnp.concatenate([o["xar"], o["a_k"]], axis=1),
                        jnp.concatenate([x.astype(BF16), o["vs"]], axis=0)) for o, x in zip(ops, t)]
            u = [_dot(x.astype(BF16), y[0:2 * n].astype(BF16)).astype(BF16) for x, y in zip(m, xtv)]
            for i, (bi, q) in enumerate(group):
                o = ops[i]
                ys = xtv[i][2 * n:4 * n] + _dot(o["a_rb"], u[i])
                y_ref[bi, sl, q * LANES:(q + 1) * LANES] = ys[0:n] + ys[n:2 * n]
                t_ref[bi * npair + q] = \
                    t[i] * jnp.transpose(jnp.broadcast_to(o["decay"], (LANES, LANES))) \
                    + _dot_tn(jnp.concatenate([o["bt"], o["kt"]], axis=0),
                              jnp.concatenate([u[i], o["vs"]], axis=0))

        group = [(bi, q) for bi in range(nb) for q in range(npair)]
        ops = scores(group)
        finish(group, ops, inverse(ops))
        return carry

    lax.fori_loop(0, ts // n, chunk, 0)


def _wkv(r, k, v, kk, a, lw, *, ts, npair, nb):
    b, s, d = r.shape
    wd = npair * LANES
    spec = pl.BlockSpec((nb, ts, wd), lambda bi, p, i: (bi, i, p))
    return pl.pallas_call(
        _wkv_kernel,
        grid=(b // nb, d // wd, s // ts),
        in_specs=[spec] * 6,
        out_specs=spec,
        out_shape=jax.ShapeDtypeStruct((b, s, d), F32),
        scratch_shapes=[pltpu.VMEM((nb * npair, LANES, LANES), F32),
                        pltpu.VMEM((WKV_LEVELS + 3, LANES, LANES), F32)],
        compiler_params=_cparams(("parallel", "parallel", "arbitrary")),
        name="wkv7",
    )(r, k, v, kk, a, lw)


def _even_out_kernel(ya_ref, y_ref, bonus_ref, g_ref, lnw_ref, lnb_ref, w_ref, h_ref, o_ref):
    ka = ya_ref.shape[1]
    y = y_ref[...]
    inv_n = 1.0 / RWKV_HEAD
    s1 = _head_sum(y)
    s2 = _head_sum(y * y)
    acc = h_ref[...] + _dot(ya_ref[...], w_ref[0:ka, :])
    mean = s1 * inv_n
    var = jnp.maximum(s2 * inv_n - mean * mean, 0.0)
    yn = (y - mean) * lax.rsqrt(var + GN_EPS) * lnw_ref[...] + lnb_ref[...]
    yb = ((yn + bonus_ref[...].astype(F32)) * g_ref[...].astype(F32)).astype(BF16)
    o_ref[...] = acc + _dot(yb, w_ref[ka:, :])


def _even_out(ya, y, bonus, g, ln_w, ln_b, w, layer, h, *, tm):
    m, n = h.shape
    ka, kb = ya.shape[1], y.shape[1]
    blk = lambda width: pl.BlockSpec((tm, width), lambda i: (i, 0))
    row = pl.BlockSpec((1, kb), lambda i: (0, 0))
    return pl.pallas_call(
        _even_out_kernel,
        grid=(m // tm,),
        in_specs=[blk(ka), blk(kb), blk(kb), blk(kb), row, row,
                  pl.BlockSpec((None, ka + kb, n), lambda i: (layer, 0, 0)), blk(n)],
        out_specs=blk(n),
        out_shape=jax.ShapeDtypeStruct((m, n), F32),
        compiler_params=_cparams(("parallel",)),
        name="even_out",
    )(ya, y, bonus, g, ln_w, ln_b, w, h)


def _pool_kernel(u_ref, up_ref, w_ref, sc_ref, o_ref, buf_ref):
    ts = o_ref.shape[1]
    hist = max(POOL_WINDOWS)
    i = pl.program_id(1)
    u = u_ref[0].astype(F32)
    _fill_history(buf_ref, up_ref[0].astype(F32), u, hist, i == 0)
    pos = i * ts + lax.broadcasted_iota(jnp.int32, (ts, 1), 0)
    outs = []
    for gi, win in enumerate(POOL_WINDOWS):
        cols = pl.ds(gi * POOL_GROUP, POOL_GROUP)
        acc = buf_ref[pl.ds(hist, ts), cols]
        for j in range(1, win):
            acc = acc + buf_ref[pl.ds(hist - j, ts), cols]
        count = jnp.minimum(pos + 1, win).astype(F32)
        dg = acc / count - u[:, gi * POOL_GROUP:(gi + 1) * POOL_GROUP]
        outs.append(_dot(dg.astype(BF16), w_ref[gi]))
    y = jnp.concatenate(outs, axis=1) * sc_ref[...]
    o_ref[0] = y.astype(o_ref.dtype)


def _pool_mixer(p, pool_w, pool_scale, *, ts):
    b, s, _ = p.shape
    hist = max(POOL_WINDOWS)
    rb = ts // hist
    return pl.pallas_call(
        _pool_kernel,
        grid=(b, s // ts),
        in_specs=[
            pl.BlockSpec((1, ts, POOL_DIM), lambda bi, i: (bi, i, 0)),
            pl.BlockSpec((1, hist, POOL_DIM), lambda bi, i: (bi, jnp.maximum(i * rb - 1, 0), 0)),
            pl.BlockSpec((len(POOL_WINDOWS), POOL_GROUP, POOL_GROUP), lambda bi, i: (0, 0, 0)),
            pl.BlockSpec((1, POOL_DIM), lambda bi, i: (0, 0)),
        ],
        out_specs=pl.BlockSpec((1, ts, POOL_DIM), lambda bi, i: (bi, i, 0)),
        out_shape=jax.ShapeDtypeStruct((b, s, POOL_DIM), BF16),
        scratch_shapes=[pltpu.VMEM((ts + hist, POOL_DIM), F32)],
        compiler_params=_cparams(("parallel", "parallel")),
        name="pool_mixer",
    )(p, p, pool_w, pool_scale)


def _mla_prep_kernel(ql_ref, kvl_ref, kpe_ref, kper_ref, ang_ref, qn_ref, kvn_ref,
                     wq_ref, wqr_ref, wk_ref, wv_ref, q_out, k_out, v_out):
    cos = jnp.cos(ang_ref[...])
    sin = jnp.sin(ang_ref[...])
    qn = _rms(ql_ref[...].astype(F32), qn_ref[...]).astype(BF16)
    kvn = _rms(kvl_ref[...].astype(F32), kvn_ref[...]).astype(BF16)
    scale = (QK_NOPE + QK_ROPE) ** -0.5 * LOG2_E
    qa = _dot(qn, wq_ref[...])
    qr = _dot(qn, wqr_ref[...])
    ka = _dot(kvn, wk_ref[...])
    kpe = kpe_ref[...].astype(F32) * cos + kper_ref[...].astype(F32) * sin
    for h in range(MLA_HEADS):
        lo = h * QK_PAD
        q_out[:, lo:lo + LANES] = (qa[:, lo:lo + LANES] * scale).astype(BF16)
        q_pe = qa[:, lo + LANES:lo + 2 * LANES] * cos + qr[:, h * LANES:(h + 1) * LANES] * sin
        q_out[:, lo + LANES:lo + 2 * LANES] = (q_pe * scale).astype(BF16)
        k_out[:, lo:lo + LANES] = ka[:, h * QK_NOPE:(h + 1) * QK_NOPE].astype(BF16)
        k_out[:, lo + LANES:lo + 2 * LANES] = kpe.astype(BF16)
    vt = _dot_nt(wv_ref[...], kvn).astype(BF16)
    ones = jnp.ones((ATTN_SUM_ROWS, vt.shape[1]), BF16)
    for h in range(MLA_HEADS):
        v_out[h * VT_ROWS:h * VT_ROWS + V_HEAD, :] = vt[h * V_HEAD:(h + 1) * V_HEAD, :]
        v_out[h * VT_ROWS + V_HEAD:(h + 1) * VT_ROWS, :] = ones


def _mla_prep(p, ang, q_norm, kv_norm, wq, wqr, wk, wv, layer, *, tm):
    m = p.shape[0]
    hq = MLA_HEADS * QK_PAD
    hv = MLA_HEADS * VT_ROWS
    row = lambda a: pl.BlockSpec(a.shape, lambda i: (0, 0))
    full = lambda a: pl.BlockSpec((None,) + a.shape[1:], lambda i: (layer, 0, 0))
    return pl.pallas_call(
        _mla_prep_kernel,
        grid=(m // tm,),
        in_specs=[
            pl.BlockSpec((tm, Q_LORA), lambda i: (i, 1)),
            pl.BlockSpec((tm, KV_LORA), lambda i: (i, 2)),
            pl.BlockSpec((tm, LANES), lambda i: (i, 12)),
            pl.BlockSpec((tm, LANES), lambda i: (i, 13)),
            pl.BlockSpec((tm, LANES), lambda i: (i, 0)),
            row(q_norm), row(kv_norm), full(wq), full(wqr), full(wk), full(wv),
        ],
        out_specs=[pl.BlockSpec((tm, hq), lambda i: (i, 0)),
                   pl.BlockSpec((tm, hq), lambda i: (i, 0)),
                   pl.BlockSpec((hv, tm), lambda i: (0, i))],
        out_shape=[jax.ShapeDtypeStruct((m, hq), BF16),
                   jax.ShapeDtypeStruct((m, hq), BF16),
                   jax.ShapeDtypeStruct((hv, m), BF16)],
        compiler_params=_cparams(("parallel",)),
        name="mla_prep",
    )(p, p, p, p, ang, q_norm, kv_norm, wq, wqr, wk, wv)


def _attn_kernel(q_ref, k_ref, vt_ref, o_ref, *, tq, tk, nh):
    qi = pl.program_id(2)
    heads = range(nh)
    def keys(j):
        return pl.ds(pl.multiple_of(j * tk, tk), tk)

    def scores(j):
        return [_dot_nt(k_ref[0, keys(j), h * QK_PAD:(h + 1) * QK_PAD],
                        q_ref[0, :, h * QK_PAD:(h + 1) * QK_PAD]) for h in heads]

    def softmax_pv(j, s, m, acc, masked):
        if masked:
            key = j * tk + lax.broadcasted_iota(jnp.int32, (tk, tq), 0)
            qry = qi * tq + lax.broadcasted_iota(jnp.int32, (tk, tq), 1)
            s = [jnp.where(key <= qry, x, -1e30) for x in s]
        width = min(ATTN_Q_SPLIT, tq)
        nq = tq // width
        parts = [(h, slice(c * width, (c + 1) * width)) for h in heads for c in range(nq)]
        sp = [s[h][:, c] for h, c in parts]
        mp = [m[h][:, c] for h, c in parts]
        m_new = [jnp.maximum(a, jnp.max(x, axis=0, keepdims=True)) for a, x in zip(mp, sp)]
        alpha = [jnp.exp2(a - b) for a, b in zip(mp, m_new)]
        p = [jnp.exp2(x - a).astype(BF16) for x, a in zip(sp, m_new)]
        pv = [_dot(vt_ref[h * VT_ROWS:(h + 1) * VT_ROWS, keys(j)], x) for (h, _), x in zip(parts, p)]
        acc_new = [a * acc[h][:, c] + x for a, (h, c), x in zip(alpha, parts, pv)]
        join = lambda xs: [jnp.concatenate(xs[h * nq:(h + 1) * nq], axis=1) for h in heads]
        return join(m_new), join(acc_new)

    def two_blocks(j, m, acc, last_masked):
        s0, s1 = scores(j), scores(j + 1)
        m, acc = softmax_pv(j, s0, m, acc, False)
        return softmax_pv(j + 1, s1, m, acc, last_masked)

    def finish(acc):
        for h in heads:
            out = acc[h][0:V_HEAD] / acc[h][V_HEAD:V_HEAD + 1]
            o_ref[0, :, h * V_HEAD:(h + 1) * V_HEAD] = jnp.transpose(out).astype(o_ref.dtype)

    init = ([jnp.full((1, tq), -1e30, F32) for _ in heads],
            [jnp.zeros((V_HEAD + ATTN_SUM_ROWS, tq), F32) for _ in heads])
    pairs = qi // 2
    m, acc = lax.fori_loop(0, pairs, lambda i, c: two_blocks(2 * i, *c, False), init)

    @pl.when(qi % 2 == 1)
    def _():
        finish(two_blocks(qi - 1, m, acc, True)[1])

    @pl.when(qi % 2 == 0)
    def _():
        finish(softmax_pv(qi, scores(qi), m, acc, True)[1])


def _attention(q, k, vt, *, tq, tk, nh):
    b, s, _ = q.shape
    assert tq == tk, "one key block per query tile sits on the causal diagonal"
    return pl.pallas_call(
        functools.partial(_attn_kernel, tq=tq, tk=tk, nh=nh),
        grid=(b, MLA_HEADS // nh, s // tq),
        in_specs=[
            pl.BlockSpec((1, tq, nh * QK_PAD), lambda bi, h, i: (bi, i, h)),
            pl.BlockSpec((1, s, nh * QK_PAD), lambda bi, h, i: (bi, 0, h)),
            pl.BlockSpec((nh * VT_ROWS, s), lambda bi, h, i: (h, bi)),
        ],
        out_specs=pl.BlockSpec((1, tq, nh * V_HEAD), lambda bi, h, i: (bi, i, h)),
        out_shape=jax.ShapeDtypeStruct((b, s, MLA_HEADS * V_HEAD), BF16),
        compiler_params=_cparams(("parallel", "parallel", "arbitrary")),
        name="mla_attention",
    )(q, k, vt)


def _pad_last(w, n):
    return jnp.pad(w, [(0, 0)] * (w.ndim - 1) + [(0, n - w.shape[-1])])


def _pad_rows(w, n):
    return jnp.pad(w, ((0, n - w.shape[0]), (0, 0)))


def _rot_cols(w):
    half = w.shape[-1] // 2
    return jnp.concatenate([-w[..., half:], w[..., :half]], axis=-1)


def _even_weights(w_in, mu):
    r = 3 * RWKV_DIM
    w = _pad_last(w_in, EV_COLS).astype(BF16)
    return w, mu[:, None, :r], _pad_last(mu[:, r:], EV_LORA)[:, None, :]


def _odd_weights(w_in, w_uq, w_ukv):
    n = w_in.shape[0]
    o3 = POOL_DIM + Q_LORA + KV_LORA
    kpe = w_in[..., o3:]
    w = jnp.concatenate([w_in[..., :o3], _pad_last(kpe, LANES), _pad_last(_rot_cols(kpe), LANES)],
                        axis=-1).astype(BF16)
    uq = w_uq.reshape(n, Q_LORA, MLA_HEADS, QK_NOPE + QK_ROPE)
    wq = _pad_last(uq, QK_PAD).reshape(n, Q_LORA, MLA_HEADS * QK_PAD).astype(BF16)
    wqr = _pad_last(_rot_cols(uq[..., QK_NOPE:]), LANES)
    wqr = wqr.reshape(n, Q_LORA, MLA_HEADS * LANES).astype(BF16)
    ukv = w_ukv.reshape(n, KV_LORA, MLA_HEADS, QK_NOPE + V_HEAD)
    wk = ukv[..., :QK_NOPE].reshape(n, KV_LORA, MLA_HEADS * QK_NOPE).astype(BF16)
    wv = ukv[..., QK_NOPE:].reshape(n, KV_LORA, MLA_HEADS * V_HEAD)
    return w, wq, wqr, wk, jnp.swapaxes(wv, 1, 2).astype(BF16)


def _tile(n, pref):
    return pref if n % pref == 0 else n


def _tiles(b, s):
    m = b * s
    return dict(
        rows_wide=_tile(m, 1024),
        rows=_tile(m, 512),
        ffn_cols=512,
        even_in_cols=EV_COLS // 3,
        odd_in_cols=OD_COLS,
        seq_mixers=_tile(s, 1024),
        seq_prep=_tile(s, 256),
        seq_wkv=_tile(s, 256),
        wkv_pairs=RWKV_DIM // LANES,
        wkv_batch=2 if b % 2 == 0 else 1,
        attn=_tile(s, 512),
        attn_heads=4,
    )


def kernel(x, positions, ev_norm, ev_w_in, ev_conv_w, ev_mu, ev_w0, ev_w2, ev_a0, ev_a2, ev_g2, ev_k_k, ev_k_a, ev_r_k, ev_ln_w, ev_ln_b, ev_w_out, od_norm, od_w_in, od_pool_w, od_pool_scale, od_q_norm, od_w_uq, od_kv_norm, od_w_ukv, od_w_out, ffn_norm, ffn_w_gate, ffn_w_up, ffn_w_down, final_norm):
    b, s, d = x.shape
    m = b * s
    depth = ffn_norm.shape[0]
    t = _tiles(b, s)

    inv = 1.0 / (ROPE_THETA ** (jnp.arange(0, QK_ROPE, 2, dtype=F32) / QK_ROPE))
    inv = jnp.concatenate([inv, inv, jnp.zeros((LANES - QK_ROPE,), F32)])
    ang = (positions.astype(F32)[..., None] * inv).reshape(m, LANES)

    ev_w, ev_mu_main, ev_mu_lora = _even_weights(ev_w_in, ev_mu)
    od_w, wq, wqr, wk, wv = _odd_weights(od_w_in, od_w_uq, od_w_ukv)
    ev_wo = ev_w_out.astype(BF16)
    od_wo = od_w_out.astype(BF16)
    wg, wu, wd = ffn_w_gate.astype(BF16), ffn_w_up.astype(BF16), ffn_w_down
    row = lambda t: t[None, :]

    h = x.reshape(m, d)
    for layer in range(depth):
        j = layer // 2
        if layer % 2 == 0:
            p = _norm_matmul(h, row(ev_norm[j]), ev_w, j, tm=t["rows_wide"], tn=t["even_in_cols"],
                             out_dtype=BF16)
            p = p.reshape(b, s, EV_COLS)
            ya = _conv_mixer(p, ev_conv_w[j].T, ts=t["seq_mixers"], tc=CONV_DIM)
            r, k, v, kk, a, lw, g, bonus = _rwkv_prep(
                p, ev_mu_main[j], ev_mu_lora[j], row(ev_w0[j]),
                _pad_rows(ev_w2[j], LANES).astype(BF16), row(ev_a0[j]),
                jnp.pad(ev_a2[j], ((W_LORA, LANES - W_LORA - A_LORA), (0, 0))).astype(BF16),
                _pad_rows(ev_g2[j], 2 * LANES).astype(BF16),
                row(ev_k_k[j]), row(ev_k_a[j]), row(ev_r_k[j].reshape(-1)), ts=t["seq_prep"])
            y = _wkv(r, k, v, kk, a, lw, ts=t["seq_wkv"], npair=t["wkv_pairs"], nb=t["wkv_batch"])
            h = _even_out(ya.reshape(m, -1), y.reshape(m, -1), bonus.reshape(m, -1),
                          g.reshape(m, -1), row(ev_ln_w[j]), row(ev_ln_b[j]), ev_wo, j, h,
                          tm=t["rows"])
        else:
            p = _norm_matmul(h, row(od_norm[j]), od_w, j, tm=t["rows_wide"], tn=t["odd_in_cols"],
                             out_dtype=BF16)
            yc = _pool_mixer(p.reshape(b, s, OD_COLS), od_pool_w[j].astype(BF16),
                             row(od_pool_scale[j]), ts=t["seq_mixers"])
            q, kq, vv = _mla_prep(p, ang, row(od_q_norm[j]), row(od_kv_norm[j]),
                                  wq, wqr, wk, wv, j, tm=t["rows"])
            hq = MLA_HEADS * QK_PAD
            yd = _attention(q.reshape(b, s, hq), kq.reshape(b, s, hq), vv, tq=t["attn"],
                            tk=t["attn"], nh=t["attn_heads"])
            h = _proj_residual(yc.reshape(m, -1), yd.reshape(m, -1), od_wo, j, h, tm=t["rows"], tn=d)
        h = _ffn(h, row(ffn_norm[layer]), row(final_norm), wg, wu, wd, layer, tm=t["rows_wide"],
                 tf=t["ffn_cols"], norm_out=layer == depth - 1)
    return h.reshape(b, s, d)
```
